```python
import math
import jax, jax.numpy as jnp
from jax import lax
import numpy as np

D_MODEL = 1024
BATCH = 8
SEQ = 2048
DEPTH = 2

HEAD_DIM = 64
ROPE_THETA = 10000.0
NORM_EPS = 1e-6
Q_BLOCK = 128

MLA_HEADS = 6
MLA_Q_RANK = 384
MLA_KV_RANK = 256
MLA_NOPE_DIM = 64
MLA_ROPE_DIM = 32
MLA_V_DIM = 64

SWA_HEADS = 6
SWA_KV_HEADS = 2
WINDOW = 128

DIFF_HEADS = 4
DIFF_QK_DIM = 32
DIFF_V_DIM = 64

MIX_WIDTH = MLA_HEADS * MLA_V_DIM + SWA_HEADS * HEAD_DIM + DIFF_HEADS * DIFF_V_DIM

IN_SPLITS = (
    MLA_Q_RANK, MLA_KV_RANK, MLA_ROPE_DIM,
    SWA_HEADS * HEAD_DIM, SWA_KV_HEADS * HEAD_DIM, SWA_KV_HEADS * HEAD_DIM,
    DIFF_HEADS * 2 * DIFF_QK_DIM, DIFF_HEADS * 2 * DIFF_QK_DIM, DIFF_HEADS * DIFF_V_DIM,
)
IN_WIDTH = sum(IN_SPLITS)

N_GROUPS = 4
EXPERTS_PER_GROUP = 8
TOP_K = 2
EXPERT_FF = 256

kernel_name = "hymba_style_mla_swa_diff_hmoe_encoder"


def rms_norm(x, g):
    xf = x.astype(jnp.float32)
    y = xf * lax.rsqrt(jnp.mean(xf * xf, axis=-1, keepdims=True) + NORM_EPS)
    return (y * g.astype(jnp.float32)).astype(x.dtype)


def rope_tables(seq, dim):
    inv = 1.0 / (ROPE_THETA ** (jnp.arange(0, dim, 2, dtype=jnp.float32) / dim))
    ang = jnp.arange(seq, dtype=jnp.float32)[:, None] * inv[None, :]
    return jnp.cos(ang), jnp.sin(ang)


def apply_rope(x, cos, sin):
    shape = (cos.shape[0],) + (1,) * (x.ndim - 3) + (cos.shape[1],)
    c = cos.reshape(shape).astype(x.dtype)
    s = sin.reshape(shape).astype(x.dtype)
    x1, x2 = jnp.split(x, 2, axis=-1)
    return jnp.concatenate([x1 * c - x2 * s, x2 * c + x1 * s], axis=-1)


def sweep_query_blocks(fn, *qs):
    B, S = qs[0].shape[:2]
    nb = S // Q_BLOCK
    blocked = tuple(jnp.moveaxis(q.reshape((B, nb, Q_BLOCK) + q.shape[2:]), 1, 0) for q in qs)
    out = lax.map(lambda args: fn(*args), blocked)
    return jnp.moveaxis(out, 0, 1).reshape((B, S) + out.shape[3:])


def mla_mixer(c_q, c_kv, k_rope, q_norm, kv_norm, w_uq, w_ukv, rope):
    B, S, _ = c_q.shape
    q = (rms_norm(c_q, q_norm) @ w_uq).reshape(B, S, MLA_HEADS, MLA_NOPE_DIM + MLA_ROPE_DIM)
    q_nope = q[..., :MLA_NOPE_DIM]
    q_rope = apply_rope(q[..., MLA_NOPE_DIM:], *rope)
    kv = (rms_norm(c_kv, kv_norm) @ w_ukv).reshape(B, S, MLA_HEADS, MLA_NOPE_DIM + MLA_V_DIM)
    k_nope = kv[..., :MLA_NOPE_DIM]
    v = kv[..., MLA_NOPE_DIM:]
    k_rope = apply_rope(k_rope, *rope)
    scale = (MLA_NOPE_DIM + MLA_ROPE_DIM) ** -0.5

    def block(qn, qr):
        s = (jnp.einsum('bqhd,bkhd->bhqk', qn, k_nope)
             + jnp.einsum('bqhr,bkr->bhqk', qr, k_rope))
        p = jax.nn.softmax(s.astype(jnp.float32) * scale, axis=-1)
        return jnp.einsum('bhqk,bkhd->bqhd', p.astype(v.dtype), v)

    o = sweep_query_blocks(block, q_nope, q_rope)
    return o.reshape(B, S, MLA_HEADS * MLA_V_DIM)


def swa_mixer(q, k, v, sink, rope):
    B, S, _ = q.shape
    G = SWA_HEADS // SWA_KV_HEADS
    nb = S // WINDOW
    q = apply_rope(q.reshape(B, S, SWA_HEADS, HEAD_DIM), *rope)
    k = apply_rope(k.reshape(B, S, SWA_KV_HEADS, HEAD_DIM), *rope)
    v = v.reshape(B, S, SWA_KV_HEADS, HEAD_DIM)
    qb = q.reshape(B, nb, WINDOW, SWA_KV_HEADS, G, HEAD_DIM)

    def band(t):
        tp = jnp.pad(t, ((0, 0), (WINDOW, WINDOW), (0, 0), (0, 0)))
        tp = tp.reshape(B, nb + 2, WINDOW, SWA_KV_HEADS, HEAD_DIM)
        return jnp.concatenate([tp[:, :-2], tp[:, 1:-1], tp[:, 2:]], axis=2)

    kb, vb = band(k), band(v)
    s = jnp.einsum('bnqhgd,bnkhd->bnhgqk', qb, kb).astype(jnp.float32) * (HEAD_DIM ** -0.5)
    a = jnp.arange(WINDOW)[:, None]
    c = jnp.arange(3 * WINDOW)[None, :]
    rel = c - a
    in_window = (rel >= 0) & (rel <= 2 * WINDOW)
    key_pos = jnp.arange(nb)[:, None] * WINDOW - WINDOW + jnp.arange(3 * WINDOW)[None, :]
    in_range = (key_pos >= 0) & (key_pos < S)
    valid = in_window[None] & in_range[:, None, :]
    s = jnp.where(valid[None, :, None, None], s, -jnp.inf)
    sink_logit = jnp.broadcast_to(
        sink.astype(jnp.float32).reshape(1, 1, SWA_KV_HEADS, G, 1, 1), s.shape[:-1] + (1,))
    p = jax.nn.softmax(jnp.concatenate([s, sink_logit], axis=-1), axis=-1)[..., :-1]
    o = jnp.einsum('bnhgqk,bnkhd->bnqhgd', p.astype(vb.dtype), vb)
    return o.reshape(B, S, SWA_HEADS * HEAD_DIM)


def diff_mixer(q, k, v, lq1, lk1, lq2, lk2, subln, lambda_init, rope):
    B, S, _ = q.shape
    q = apply_rope(q.reshape(B, S, DIFF_HEADS, 2, DIFF_QK_DIM), *rope)
    k = apply_rope(k.reshape(B, S, DIFF_HEADS, 2, DIFF_QK_DIM), *rope)
    v = v.reshape(B, S, DIFF_HEADS, DIFF_V_DIM)
    lam = (jnp.exp(jnp.sum(lq1.astype(jnp.float32) * lk1.astype(jnp.float32)))
           - jnp.exp(jnp.sum(lq2.astype(jnp.float32) * lk2.astype(jnp.float32)))
           + lambda_init)
    scale = DIFF_QK_DIM ** -0.5

    def block(qb):
        s = jnp.einsum('bqhcd,bkhcd->bhcqk', qb, k)
        p = jax.nn.softmax(s.astype(jnp.float32) * scale, axis=-1)
        attn = p[:, :, 0] - lam * p[:, :, 1]
        return jnp.einsum('bhqk,bkhd->bqhd', attn.astype(v.dtype), v)

    o = sweep_query_blocks(block, q)
    o = rms_norm(o, subln) * (1.0 - lambda_init)
    return o.reshape(B, S, DIFF_HEADS * DIFF_V_DIM)


def hier_moe(h, w_rg, b_rg, w_re, b_re, w_gate, w_up, w_down):
    B, S, D = h.shape
    t = h.reshape(B * S, D)
    T = t.shape[0]
    g_logits = (t @ w_rg).astype(jnp.float32) + b_rg.astype(jnp.float32)
    g_prob = jax.nn.softmax(g_logits, axis=-1)
    g_idx = jnp.argmax(g_logits, axis=-1)
    g_gate = jnp.take_along_axis(g_prob, g_idx[:, None], axis=-1)
    e_logits = ((t @ w_re).astype(jnp.float32) + b_re.astype(jnp.float32)).reshape(
        T, N_GROUPS, EXPERTS_PER_GROUP)
    e_in = jnp.take_along_axis(e_logits, g_idx[:, None, None], axis=1)[:, 0]
    top_val, top_idx = lax.top_k(e_in, TOP_K)
    top_w = jax.nn.softmax(top_val, axis=-1) * g_gate
    comb_e = jnp.sum(jax.nn.one_hot(top_idx, EXPERTS_PER_GROUP) * top_w[..., None], axis=1)
    comb = (jax.nn.one_hot(g_idx, N_GROUPS)[:, :, None] * comb_e[:, None, :]).astype(t.dtype)
    y = jnp.zeros_like(t)
    for g in range(N_GROUPS):
        a = jnp.einsum('td,edf->tef', t, w_gate[g])
        u = jnp.einsum('td,edf->tef', t, w_up[g])
        hid = jax.nn.silu(a) * u * comb[:, g, :, None]
        y = y + jnp.einsum('tef,efd->td', hid, w_down[g])
    return y.reshape(B, S, D)


def _split_points():
    pts, acc = [], 0
    for w in IN_SPLITS[:-1]:
        acc += w
        pts.append(acc)
    return pts


def setup_inputs(seed: int = 0) -> dict:
    key = jax.random.key(seed)
    ks = jax.random.split(key, 32)

    def nrm(k, shape, scale):
        return jax.random.normal(k, shape, jnp.float32) * scale

    def gain(k, shape):
        return 1.0 + 0.02 * jax.random.normal(k, shape, jnp.float32)

    L, D, E, G, F = DEPTH, D_MODEL, EXPERTS_PER_GROUP, N_GROUPS, EXPERT_FF
    return {
        "x": nrm(ks[0], (BATCH, SEQ, D), 1.0),
        "attn_norm": gain(ks[1], (L, D)),
        "w_in": nrm(ks[2], (L, D, IN_WIDTH), D ** -0.5),
        "mla_q_norm": gain(ks[3], (L, MLA_Q_RANK)),
        "mla_kv_norm": gain(ks[4], (L, MLA_KV_RANK)),
        "mla_w_uq": nrm(ks[5], (L, MLA_Q_RANK, MLA_HEADS * (MLA_NOPE_DIM + MLA_ROPE_DIM)), MLA_Q_RANK ** -0.5),
        "mla_w_ukv": nrm(ks[6], (L, MLA_KV_RANK, MLA_HEADS * (MLA_NOPE_DIM + MLA_V_DIM)), MLA_KV_RANK ** -0.5),
        "swa_sink": nrm(ks[7], (L, SWA_HEADS), 0.5),
        "diff_lq1": nrm(ks[8], (L, DIFF_QK_DIM), 0.1),
        "diff_lk1": nrm(ks[9], (L, DIFF_QK_DIM), 0.1),
        "diff_lq2": nrm(ks[10], (L, DIFF_QK_DIM), 0.1),
        "diff_lk2": nrm(ks[11], (L, DIFF_QK_DIM), 0.1),
        "diff_subln": gain(ks[12], (L, DIFF_V_DIM)),
        "w_out": nrm(ks[13], (L, MIX_WIDTH, D), MIX_WIDTH ** -0.5),
        "ffn_norm": gain(ks[14], (L, D)),
        "router_group": nrm(ks[15], (L, D, G), D ** -0.5),
        "router_group_bias": nrm(ks[16], (L, G), 0.01),
        "router_expert": nrm(ks[17], (L, D, G * E), D ** -0.5),
        "router_expert_bias": nrm(ks[18], (L, G * E), 0.01),
        "w_gate": nrm(ks[19], (L, G, E, D, F), D ** -0.5),
        "w_up": nrm(ks[20], (L, G, E, D, F), D ** -0.5),
        "w_down": nrm(ks[21], (L, G, E, F, D), F ** -0.5),
        "final_norm": gain(ks[22], (D,)),
    }


def reference(x, attn_norm, w_in, mla_q_norm, mla_kv_norm, mla_w_uq, mla_w_ukv, swa_sink,
              diff_lq1, diff_lk1, diff_lq2, diff_lk2, diff_subln, w_out, ffn_norm,
              router_group, router_group_bias, router_expert, router_expert_bias,
              w_gate, w_up, w_down, final_norm):
    S = x.shape[1]
    rope32 = rope_tables(S, MLA_ROPE_DIM)
    rope64 = rope_tables(S, HEAD_DIM)
    points = _split_points()
    for l in range(DEPTH):
        h = rms_norm(x, attn_norm[l])
        (mq, mkv, mkr, sq, sk, sv, dq, dk, dv) = jnp.split(h @ w_in[l], points, axis=-1)
        o_mla = mla_mixer(mq, mkv, mkr, mla_q_norm[l], mla_kv_norm[l], mla_w_uq[l], mla_w_ukv[l], rope32)
        o_swa = swa_mixer(sq, sk, sv, swa_sink[l], rope64)
        lambda_init = 0.8 - 0.6 * math.exp(-0.3 * l)
        o_diff = diff_mixer(dq, dk, dv, diff_lq1[l], diff_lk1[l], diff_lq2[l], diff_lk2[l],
                            diff_subln[l], lambda_init, rope32)
        x = x + jnp.concatenate([o_mla, o_swa, o_diff], axis=-1) @ w_out[l]
        x = x + hier_moe(rms_norm(x, ffn_norm[l]), router_group[l], router_group_bias[l],
                         router_expert[l], router_expert_bias[l], w_gate[l], w_up[l], w_down[l])
    return rms_norm(x, final_norm)
```

```python
import functools
import math

import jax
import jax.numpy as jnp
from jax import lax
from jax.experimental import pallas as pl
from jax.experimental.pallas import tpu as pltpu

D_MODEL = 1024
SEQ = 2048
HEAD_DIM = 64
ROPE_THETA = 10000.0
NORM_EPS = 1e-6
MLA_HEADS = 6
MLA_Q_RANK = 384
MLA_KV_RANK = 256
MLA_NOPE_DIM = 64
MLA_ROPE_DIM = 32
MLA_V_DIM = 64
SWA_HEADS = 6
SWA_KV_HEADS = 2
WINDOW = 128
DIFF_HEADS = 4
DIFF_QK_DIM = 32
DIFF_V_DIM = 64
N_GROUPS = 4
EXPERTS_PER_GROUP = 8
N_EXPERTS = N_GROUPS * EXPERTS_PER_GROUP
EXPERT_FF = 256

LANES = 128
ROW_TILES = D_MODEL // LANES
EXPERT_LANE0 = 8
PROJ_TILE = 512
ATTN_Q_TILE = 512
SWA_Q_TILE = 256
MOE_TILE = 256
MOVE_TILE = 256
VMEM_LIMIT = 56 * 1024 * 1024

SEG_CQ = 0
SEG_CKV = SEG_CQ + MLA_Q_RANK
SEG_KR = SEG_CKV + MLA_KV_RANK
SEG_SQ = SEG_KR + LANES
SEG_SK = SEG_SQ + SWA_HEADS * HEAD_DIM
SEG_SV = SEG_SK + LANES
SEG_DQ = SEG_SV + LANES
SEG_DK = SEG_DQ + 2 * LANES
SEG_DV = SEG_DK + 2 * LANES
PROJ_WIDTH = SEG_DV + 2 * LANES

_NT = (((1,), (1,)), ((), ()))


def _rms(x, g):
    return x * lax.rsqrt(jnp.mean(x * x, axis=-1, keepdims=True) + NORM_EPS) * g


def _load_rows(ref, n):
    return jnp.concatenate([ref[pl.ds(c, n, stride=ROW_TILES), :] for c in range(ROW_TILES)], axis=1)


def _store_rows(ref, val):
    n = val.shape[0]
    for c in range(ROW_TILES):
        ref[pl.ds(c, n, stride=ROW_TILES), :] = val[:, c * LANES:(c + 1) * LANES]


def _rope(x, cos, sin, first_half, shift):
    up = pltpu.roll(x, LANES - shift, 1)
    dn = pltpu.roll(x, shift, 1)
    return x * cos + jnp.where(first_half, up, dn) * sin


def _proj_kernel(x_ref, g_ref, w1_ref, qn_ref, kvn_ref, wuq_ref, wukv_ref,
                 cm_ref, sm_ref, cs_ref, ss_ref, cd_ref, sd_ref,
                 qm_ref, km_ref, vm_ref, qs_ref, ks_ref, vs_ref,
                 qd_ref, kd_ref, vd_ref):
    x = x_ref[...]
    h = _rms(x, g_ref[...]).astype(jnp.bfloat16)
    u = jnp.dot(h, w1_ref[...], preferred_element_type=jnp.float32)

    lane = lax.broadcasted_iota(jnp.int32, (x.shape[0], LANES), 1)
    first_m = (lane >= 64) & (lane < 80)
    first_s = (lane % 64) < 32
    first_d = (lane % 32) < 16
    cm, sm = cm_ref[...], sm_ref[...]
    cs, ss = cs_ref[...], ss_ref[...]
    cd, sd = cd_ref[...], sd_ref[...]

    cq = _rms(u[:, SEG_CQ:SEG_CQ + MLA_Q_RANK], qn_ref[...]).astype(jnp.bfloat16)
    q = jnp.dot(cq, wuq_ref[...], preferred_element_type=jnp.float32)
    ckv = _rms(u[:, SEG_CKV:SEG_CKV + MLA_KV_RANK], kvn_ref[...]).astype(jnp.bfloat16)
    kv = jnp.dot(ckv, wukv_ref[...], preferred_element_type=jnp.float32)
    kr = _rope(u[:, SEG_KR:SEG_KR + LANES], cm, sm, first_m, 16)
    for hd in range(MLA_HEADS):
        sl = slice(hd * LANES, (hd + 1) * LANES)
        qm_ref[:, sl] = _rope(q[:, sl], cm, sm, first_m, 16).astype(jnp.bfloat16)
        km_ref[:, sl] = (kv[:, sl] + kr).astype(jnp.bfloat16)
    vm_ref[...] = kv[:, MLA_HEADS * LANES:].astype(jnp.bfloat16)

    for j in range(SWA_HEADS // 2):
        sl = slice(SEG_SQ + j * LANES, SEG_SQ + (j + 1) * LANES)
        qs_ref[:, j * LANES:(j + 1) * LANES] = _rope(u[:, sl], cs, ss, first_s, 32).astype(jnp.bfloat16)
    ks_ref[...] = _rope(u[:, SEG_SK:SEG_SK + LANES], cs, ss, first_s, 32).astype(jnp.bfloat16)
    vs_ref[...] = u[:, SEG_SV:SEG_SV + LANES].astype(jnp.bfloat16)

    for j in range(2):
        sq = slice(SEG_DQ + j * LANES, SEG_DQ + (j + 1) * LANES)
        sk = slice(SEG_DK + j * LANES, SEG_DK + (j + 1) * LANES)
        qd_ref[:, j * LANES:(j + 1) * LANES] = _rope(u[:, sq], cd, sd, first_d, 16).astype(jnp.bfloat16)
        kd_ref[:, j * LANES:(j + 1) * LANES] = _rope(u[:, sk], cd, sd, first_d, 16).astype(jnp.bfloat16)
    vd_ref[...] = u[:, SEG_DV:SEG_DV + 2 * LANES].astype(jnp.bfloat16)


def _proj_call(x, g, w1, qn, kvn, wuq, wukv, tables):
    T = x.shape[0]
    tm = PROJ_TILE
    n_seq = SEQ // tm
    tok = lambda i: (i, 0)
    const = lambda i: (0, 0)
    tab = lambda i: (i % n_seq, 0)
    widths = (6 * LANES, 6 * LANES, 3 * LANES, 3 * LANES, LANES, LANES,
              2 * LANES, 2 * LANES, 2 * LANES)
    return pl.pallas_call(
        _proj_kernel,
        grid=(T // tm,),
        in_specs=[
            pl.BlockSpec((tm, D_MODEL), tok),
            pl.BlockSpec((1, D_MODEL), const),
            pl.BlockSpec((D_MODEL, PROJ_WIDTH), const),
            pl.BlockSpec((1, MLA_Q_RANK), const),
            pl.BlockSpec((1, MLA_KV_RANK), const),
            pl.BlockSpec((MLA_Q_RANK, 6 * LANES), const),
            pl.BlockSpec((MLA_KV_RANK, 9 * LANES), const),
        ] + [pl.BlockSpec((tm, LANES), tab)] * 6,
        out_specs=[pl.BlockSpec((tm, w), tok) for w in widths],
        out_shape=[jax.ShapeDtypeStruct((T, w), jnp.bfloat16) for w in widths],
        compiler_params=pltpu.CompilerParams(
            dimension_semantics=("arbitrary",), vmem_limit_bytes=VMEM_LIMIT),
        name="proj",
    )(x, g, w1, qn, kvn, wuq, wukv, *tables)


def _softmax_pv(q, k, v):
    s = lax.dot_general(q, k, _NT, preferred_element_type=jnp.float32)
    m = jnp.max(s, axis=-1, keepdims=True)
    p = jnp.exp(s - m)
    l = jnp.sum(p, axis=-1, keepdims=True)
    o = jnp.dot(p.astype(jnp.bfloat16), v, preferred_element_type=jnp.float32)
    return o / l


def _mla_kernel(q_ref, k_ref, v_ref, o_ref):
    v = v_ref[...]
    lane = lax.broadcasted_iota(jnp.int32, o_ref.shape, 1)
    o0 = _softmax_pv(q_ref[:, :LANES], k_ref[:, :LANES], v)
    o1 = _softmax_pv(q_ref[:, LANES:], k_ref[:, LANES:], v)
    o_ref[...] = jnp.where(lane < MLA_V_DIM, o0, o1).astype(o_ref.dtype)


def _mla_call(q, k, v, batch):
    T = q.shape[0]
    tq = ATTN_Q_TILE
    nq = SEQ // tq
    return pl.pallas_call(
        _mla_kernel,
        grid=(batch, MLA_HEADS // 2, nq),
        in_specs=[
            pl.BlockSpec((tq, 2 * LANES), lambda b, p, i: (b * nq + i, p)),
            pl.BlockSpec((SEQ, 2 * LANES), lambda b, p, i: (b, p)),
            pl.BlockSpec((SEQ, LANES), lambda b, p, i: (b, p)),
        ],
        out_specs=pl.BlockSpec((tq, LANES), lambda b, p, i: (b * nq + i, p)),
        out_shape=jax.ShapeDtypeStruct((T, MLA_HEADS * MLA_V_DIM), jnp.bfloat16),
        compiler_params=pltpu.CompilerParams(
            dimension_semantics=("arbitrary",) * 3, vmem_limit_bytes=VMEM_LIMIT),
        name="mla_attn",
    )(q, k, v)


def _swa_kernel(sink_ref, q_ref, k_ref, v_ref, o_ref):
    i = pl.program_id(1)
    nb = SEQ // WINDOW
    row = lax.broadcasted_iota(jnp.int32, (WINDOW, 3 * WINDOW), 0)
    col = lax.broadcasted_iota(jnp.int32, (WINDOW, 3 * WINDOW), 1)
    lane = lax.broadcasted_iota(jnp.int32, (WINDOW, LANES), 1)
    left = lane < HEAD_DIM
    for sub in range(SWA_Q_TILE // WINDOW):
        n = i * (SWA_Q_TILE // WINDOW) + sub
        prev = pl.multiple_of(jnp.maximum(n - 1, 0) * WINDOW, WINDOW)
        cur = pl.multiple_of(n * WINDOW, WINDOW)
        nxt = pl.multiple_of(jnp.minimum(n + 1, nb - 1) * WINDOW, WINDOW)
        kb = jnp.concatenate([k_ref[pl.ds(prev, WINDOW), :], k_ref[pl.ds(cur, WINDOW), :],
                              k_ref[pl.ds(nxt, WINDOW), :]], axis=0)
        vb = jnp.concatenate([v_ref[pl.ds(prev, WINDOW), :], v_ref[pl.ds(cur, WINDOW), :],
                              v_ref[pl.ds(nxt, WINDOW), :]], axis=0)
        rel = col - row
        valid = (rel >= 0) & (rel <= 2 * WINDOW)
        lo_col = jnp.where(n > 0, 0, WINDOW)
        hi_col = jnp.where(n < nb - 1, 3 * WINDOW, 2 * WINDOW)
        valid = valid & (col >= lo_col) & (col < hi_col)
        for j in range(SWA_HEADS // 2):
            qp = q_ref[sub * WINDOW:(sub + 1) * WINDOW, j * LANES:(j + 1) * LANES]
            outs = []
            for side in range(2):
                head = j + side * (SWA_HEADS // 2)
                qm = jnp.where(left if side == 0 else ~left, qp, jnp.zeros_like(qp))
                s = lax.dot_general(qm, kb, _NT, preferred_element_type=jnp.float32)
                s = jnp.where(valid, s, -jnp.inf)
                sink = sink_ref[head]
                m = jnp.maximum(jnp.max(s, axis=-1, keepdims=True), sink)
                p = jnp.exp(s - m)
                l = jnp.sum(p, axis=-1, keepdims=True) + jnp.exp(sink - m)
                o = jnp.dot(p.astype(jnp.bfloat16), vb, preferred_element_type=jnp.float32)
                outs.append(o / l)
            o_ref[sub * WINDOW:(sub + 1) * WINDOW, j * LANES:(j + 1) * LANES] = (
                jnp.where(left, outs[0], outs[1]).astype(o_ref.dtype))


def _swa_call(sink, q, k, v, batch):
    T = q.shape[0]
    tq = SWA_Q_TILE
    nq = SEQ // tq
    return pl.pallas_call(
        _swa_kernel,
        grid=(batch, nq),
        in_specs=[
            pl.BlockSpec(memory_space=pltpu.SMEM),
            pl.BlockSpec((tq, 3 * LANES), lambda b, i: (b * nq + i, 0)),
            pl.BlockSpec((SEQ, LANES), lambda b, i: (b, 0)),
            pl.BlockSpec((SEQ, LANES), lambda b, i: (b, 0)),
        ],
        out_specs=pl.BlockSpec((tq, 3 * LANES), lambda b, i: (b * nq + i, 0)),
        out_shape=jax.ShapeDtypeStruct((T, SWA_HEADS * HEAD_DIM), jnp.bfloat16),
        compiler_params=pltpu.CompilerParams(
            dimension_semantics=("arbitrary",) * 2, vmem_limit_bytes=VMEM_LIMIT),
        name="swa_attn",
    )(sink, q, k, v)


def _diff_kernel(lam_init, lq1_ref, lk1_ref, lq2_ref, lk2_ref, subln_ref,
                 q_ref, k_ref, v_ref, o_ref):
    lam = (jnp.exp(jnp.sum(lq1_ref[...] * lk1_ref[...], axis=-1, keepdims=True))
           - jnp.exp(jnp.sum(lq2_ref[...] * lk2_ref[...], axis=-1, keepdims=True))
           + lam_init)
    q = q_ref[...]
    k = k_ref[...]
    v = v_ref[...]
    lane = lax.broadcasted_iota(jnp.int32, q.shape, 1)
    zero = jnp.zeros_like(q)
    outs = []
    for side in range(2):
        comp = []
        for c in range(2):
            lo = side * DIFF_V_DIM + c * DIFF_QK_DIM
            qc = jnp.where((lane >= lo) & (lane < lo + DIFF_QK_DIM), q, zero)
            comp.append(_softmax_pv(qc, k, v))
        outs.append(comp[0] - lam * comp[1])
    left = lane < DIFF_V_DIM
    o = jnp.where(left, outs[0], outs[1])
    sq = o * o
    ms_l = jnp.sum(jnp.where(left, sq, 0.0), axis=-1, keepdims=True) * (1.0 / DIFF_V_DIM)
    ms_r = jnp.sum(jnp.where(left, 0.0, sq), axis=-1, keepdims=True) * (1.0 / DIFF_V_DIM)
    r = jnp.where(left, lax.rsqrt(ms_l + NORM_EPS), lax.rsqrt(ms_r + NORM_EPS))
    o_ref[...] = (o * r * subln_ref[...] * (1.0 - lam_init)).astype(o_ref.dtype)


def _diff_call(lam_init, lq1, lk1, lq2, lk2, subln2, q, k, v, batch):
    T = q.shape[0]
    tq = ATTN_Q_TILE
    nq = SEQ // tq
    small = lambda b, p, i: (0, 0)
    return pl.pallas_call(
        functools.partial(_diff_kernel, lam_init),
        grid=(batch, DIFF_HEADS // 2, nq),
        in_specs=[pl.BlockSpec((1, DIFF_QK_DIM), small)] * 4 + [
            pl.BlockSpec((1, LANES), small),
            pl.BlockSpec((tq, LANES), lambda b, p, i: (b * nq + i, p)),
            pl.BlockSpec((SEQ, LANES), lambda b, p, i: (b, p)),
            pl.BlockSpec((SEQ, LANES), lambda b, p, i: (b, p)),
        ],
        out_specs=pl.BlockSpec((tq, LANES), lambda b, p, i: (b * nq + i, p)),
        out_shape=jax.ShapeDtypeStruct((T, DIFF_HEADS * DIFF_V_DIM), jnp.bfloat16),
        compiler_params=pltpu.CompilerParams(
            dimension_semantics=("arbitrary",) * 3, vmem_limit_bytes=VMEM_LIMIT),
        name="diff_attn",
    )(lq1, lk1, lq2, lk2, subln2, q, k, v)


def _route_kernel(x_ref, om_ref, os_ref, od_ref, wm_ref, ws_ref, wd_ref, g_ref,
                  wr_ref, br_ref, x2_ref, h2_ref, ri_ref, rf_ref, cnt_ref, carry_ref):
    step = pl.program_id(0)

    @pl.when(step == 0)
    def _():
        carry_ref[...] = jnp.zeros_like(carry_ref)

    x2 = (x_ref[...]
          + jnp.dot(om_ref[...], wm_ref[...], preferred_element_type=jnp.float32)
          + jnp.dot(os_ref[...], ws_ref[...], preferred_element_type=jnp.float32)
          + jnp.dot(od_ref[...], wd_ref[...], preferred_element_type=jnp.float32))
    x2_ref[...] = x2
    h2 = _rms(x2, g_ref[...])
    _store_rows(h2_ref, h2)

    logits = jnp.dot(h2, wr_ref[...], preferred_element_type=jnp.float32,
                     precision=lax.Precision.HIGHEST) + br_ref[...]
    tm = logits.shape[0]
    lane = lax.broadcasted_iota(jnp.int32, (tm, LANES), 1)
    neg = -jnp.inf
    is_g = lane < N_GROUPS
    gl = jnp.where(is_g, logits, neg)
    gmax = jnp.max(gl, axis=-1, keepdims=True)
    gidx = jnp.min(jnp.where(gl == gmax, lane, LANES), axis=-1, keepdims=True)
    gsum = jnp.sum(jnp.where(is_g, jnp.exp(gl - gmax), 0.0), axis=-1, keepdims=True)
    g_gate = 1.0 / gsum
    lo = EXPERT_LANE0 + EXPERTS_PER_GROUP * gidx
    el = jnp.where((lane >= lo) & (lane < lo + EXPERTS_PER_GROUP), logits, neg)
    m1 = jnp.max(el, axis=-1, keepdims=True)
    i1 = jnp.min(jnp.where(el == m1, lane, LANES), axis=-1, keepdims=True)
    el2 = jnp.where(lane == i1, neg, el)
    m2 = jnp.max(el2, axis=-1, keepdims=True)
    i2 = jnp.min(jnp.where(el2 == m2, lane, LANES), axis=-1, keepdims=True)
    t = jnp.exp(m2 - m1)
    w1 = g_gate / (1.0 + t)
    w2 = g_gate * t / (1.0 + t)

    hit1 = lane == i1
    hit2 = lane == i2
    onehot = jnp.where(hit1 | hit2, 1.0, 0.0)
    r = lax.broadcasted_iota(jnp.int32, (tm, tm), 0)
    c = lax.broadcasted_iota(jnp.int32, (tm, tm), 1)
    tri = jnp.where(c < r, 1.0, 0.0).astype(jnp.bfloat16)
    before = jnp.dot(tri, onehot.astype(jnp.bfloat16),
                     preferred_element_type=jnp.float32) + carry_ref[0:1, :]
    rank1 = jnp.sum(jnp.where(hit1, before, 0.0), axis=-1, keepdims=True)
    rank2 = jnp.sum(jnp.where(hit2, before, 0.0), axis=-1, keepdims=True)
    total = carry_ref[0:1, :] + jnp.sum(onehot, axis=0, keepdims=True)
    carry_ref[...] = jnp.broadcast_to(total, carry_ref.shape)
    cnt_ref[...] = jnp.broadcast_to(total, cnt_ref.shape)

    ri = jnp.where(lane == 0, i1 - EXPERT_LANE0,
                   jnp.where(lane == 1, i2 - EXPERT_LANE0,
                             jnp.where(lane == 2, rank1.astype(jnp.int32),
                                       jnp.where(lane == 3, rank2.astype(jnp.int32), 0))))
    ri_ref[...] = ri
    rf_ref[...] = jnp.where(lane == 0, w1, jnp.where(lane == 1, w2, 0.0))


def _route_call(x, om, os_, od, wm, ws, wd, g, wr, br):
    T = x.shape[0]
    tm = PROJ_TILE
    tok = lambda i: (i, 0)
    const = lambda i: (0, 0)
    return pl.pallas_call(
        _route_kernel,
        grid=(T // tm,),
        in_specs=[
            pl.BlockSpec((tm, D_MODEL), tok),
            pl.BlockSpec((tm, 3 * LANES), tok),
            pl.BlockSpec((tm, 3 * LANES), tok),
            pl.BlockSpec((tm, 2 * LANES), tok),
            pl.BlockSpec((3 * LANES, D_MODEL), const),
            pl.BlockSpec((3 * LANES, D_MODEL), const),
            pl.BlockSpec((2 * LANES, D_MODEL), const),
            pl.BlockSpec((1, D_MODEL), const),
            pl.BlockSpec((D_MODEL, LANES), const),
            pl.BlockSpec((1, LANES), const),
        ],
        out_specs=[
            pl.BlockSpec((tm, D_MODEL), tok),
            pl.BlockSpec((tm * ROW_TILES, LANES), tok),
            pl.BlockSpec((tm, LANES), tok),
            pl.BlockSpec((tm, LANES), tok),
            pl.BlockSpec((8, LANES), const),
        ],
        out_shape=[
            jax.ShapeDtypeStruct((T, D_MODEL), jnp.float32),
            jax.ShapeDtypeStruct((T * ROW_TILES, LANES), jnp.float32),
            jax.ShapeDtypeStruct((T, LANES), jnp.int32),
            jax.ShapeDtypeStruct((T, LANES), jnp.float32),
            jax.ShapeDtypeStruct((8, LANES), jnp.float32),
        ],
        scratch_shapes=[pltpu.VMEM((8, LANES), jnp.float32)],
        compiler_params=pltpu.CompilerParams(
            dimension_semantics=("arbitrary",), vmem_limit_bytes=VMEM_LIMIT),
        name="outproj_route",
    )(x, om, os_, od, wm, ws, wd, g, wr, br)


def _row(ref, idx):
    return ref.at[pl.ds(pl.multiple_of(idx * ROW_TILES, ROW_TILES), ROW_TILES), :]


def _push_kernel(pos_ref, fill_ref, nt_ref, h_ref, xs_ref, zero_ref, sem):
    step = pl.program_id(0)
    tm = h_ref.shape[0] // ROW_TILES
    n_tok = pl.num_programs(0) * tm
    fill_rows = MOE_TILE * ROW_TILES
    max_tiles = xs_ref.shape[0] // fill_rows

    def fill_copy(slot):
        start = pl.multiple_of(slot * ROW_TILES, fill_rows)
        return pltpu.make_async_copy(zero_ref, xs_ref.at[pl.ds(start, fill_rows), :], sem)

    @pl.when(step == 0)
    def _():
        zero_ref[...] = jnp.zeros_like(zero_ref)
        for e in range(N_EXPERTS):
            @pl.when(fill_ref[e] >= 0)
            def _():
                fill_copy(fill_ref[e]).start()

        def tail_start(j, carry):
            fill_copy(j * MOE_TILE).start()
            return carry

        def tail_wait(j, carry):
            fill_copy(j * MOE_TILE).wait()
            return carry

        lax.fori_loop(nt_ref[0], max_tiles, tail_start, 0)
        for e in range(N_EXPERTS):
            @pl.when(fill_ref[e] >= 0)
            def _():
                fill_copy(fill_ref[e]).wait()
        lax.fori_loop(nt_ref[0], max_tiles, tail_wait, 0)

    def issue(r, carry):
        t = step * tm + r
        for k in range(2):
            pltpu.make_async_copy(_row(h_ref, r), _row(xs_ref, pos_ref[k * n_tok + t]), sem).start()
        return carry

    lax.fori_loop(0, tm, issue, 0, unroll=8)

    def drain(r, carry):
        for k in range(2):
            pltpu.make_async_copy(_row(h_ref, 0), _row(xs_ref, 0), sem).wait()
        return carry

    lax.fori_loop(0, tm, drain, 0, unroll=8)


def _push_call(pos, fill, n_tiles, h2, n_slots):
    tm = MOVE_TILE
    n_tok = h2.shape[0] // ROW_TILES
    return pl.pallas_call(
        _push_kernel,
        grid_spec=pltpu.PrefetchScalarGridSpec(
            num_scalar_prefetch=3,
            grid=(n_tok // tm,),
            in_specs=[pl.BlockSpec((tm * ROW_TILES, LANES), lambda i, pos, fill, nt: (i, 0))],
            out_specs=pl.BlockSpec(memory_space=pl.ANY),
            scratch_shapes=[pltpu.VMEM((MOE_TILE * ROW_TILES, LANES), jnp.float32),
                            pltpu.SemaphoreType.DMA(())],
        ),
        out_shape=jax.ShapeDtypeStruct((n_slots * ROW_TILES, LANES), jnp.float32),
        compiler_params=pltpu.CompilerParams(
            dimension_semantics=("arbitrary",), vmem_limit_bytes=VMEM_LIMIT),
        name="moe_push",
    )(pos, fill, n_tiles, h2)


def _expert_kernel(te_ref, nt_ref, xs_ref, wg_ref, wu_ref, wd_ref, ys_ref,
                   wg_s, wu_s, wd_s):
    j = pl.program_id(0)

    @pl.when(j < nt_ref[0])
    def _():
        changed = (j == 0) | (te_ref[j] != te_ref[jnp.maximum(j - 1, 0)])

        @pl.when(changed)
        def _():
            wg_s[...] = wg_ref[...].astype(jnp.bfloat16)
            wu_s[...] = wu_ref[...].astype(jnp.bfloat16)
            wd_s[...] = wd_ref[...].astype(jnp.bfloat16)

        xb = _load_rows(xs_ref, MOE_TILE).astype(jnp.bfloat16)
        a = jnp.dot(xb, wg_s[...], preferred_element_type=jnp.float32)
        u = jnp.dot(xb, wu_s[...], preferred_element_type=jnp.float32)
        hid = (a * jax.nn.sigmoid(a) * u).astype(jnp.bfloat16)
        _store_rows(ys_ref, jnp.dot(hid, wd_s[...], preferred_element_type=jnp.float32))


def _expert_call(tile_expert, n_tiles, xs, wg, wu, wd, layer):
    n_slots = xs.shape[0] // ROW_TILES
    max_tiles = n_slots // MOE_TILE
    base = layer * N_EXPERTS
    row = lambda j, te, nt: (jnp.minimum(j, nt[0] - 1), 0)
    wsel = lambda j, te, nt: (base + te[j], 0, 0)
    return pl.pallas_call(
        _expert_kernel,
        grid_spec=pltpu.PrefetchScalarGridSpec(
            num_scalar_prefetch=2,
            grid=(max_tiles,),
            in_specs=[
                pl.BlockSpec((MOE_TILE * ROW_TILES, LANES), row),
                pl.BlockSpec((None, D_MODEL, EXPERT_FF), wsel),
                pl.BlockSpec((None, D_MODEL, EXPERT_FF), wsel),
                pl.BlockSpec((None, EXPERT_FF, D_MODEL), wsel),
            ],
            out_specs=pl.BlockSpec((MOE_TILE * ROW_TILES, LANES), row),
            scratch_shapes=[pltpu.VMEM((D_MODEL, EXPERT_FF), jnp.bfloat16),
                            pltpu.VMEM((D_MODEL, EXPERT_FF), jnp.bfloat16),
                            pltpu.VMEM((EXPERT_FF, D_MODEL), jnp.bfloat16)],
        ),
        out_shape=jax.ShapeDtypeStruct((n_slots * ROW_TILES, LANES), jnp.float32),
        input_output_aliases={2: 0},
        compiler_params=pltpu.CompilerParams(
            dimension_semantics=("arbitrary",), vmem_limit_bytes=VMEM_LIMIT),
        name="moe_experts",
    )(tile_expert, n_tiles, xs, wg, wu, wd)


def _combine_kernel(final, pos_ref, x2_ref, rf_ref, g_ref, ys_ref, o_ref, g0, g1, sem):
    step = pl.program_id(0)
    tm = x2_ref.shape[0]
    n_tok = pl.num_programs(0) * tm

    def issue(r, carry):
        t = step * tm + r
        pltpu.make_async_copy(_row(ys_ref, pos_ref[t]), _row(g0, r), sem).start()
        pltpu.make_async_copy(_row(ys_ref, pos_ref[n_tok + t]), _row(g1, r), sem).start()
        return carry

    lax.fori_loop(0, tm, issue, 0, unroll=8)

    def drain(r, carry):
        for k in range(2):
            pltpu.make_async_copy(_row(ys_ref, 0), _row(g0, 0), sem).wait()
        return carry

    lax.fori_loop(0, tm, drain, 0, unroll=8)

    rf = rf_ref[...]
    y = x2_ref[...] + rf[:, 0:1] * _load_rows(g0, tm) + rf[:, 1:2] * _load_rows(g1, tm)
    if final:
        y = _rms(y, g_ref[...])
    o_ref[...] = y


def _combine_call(pos, x2, rf, g, ys, final):
    T = x2.shape[0]
    tm = MOVE_TILE
    return pl.pallas_call(
        functools.partial(_combine_kernel, final),
        grid_spec=pltpu.PrefetchScalarGridSpec(
            num_scalar_prefetch=1,
            grid=(T // tm,),
            in_specs=[
                pl.BlockSpec((tm, D_MODEL), lambda i, pos: (i, 0)),
                pl.BlockSpec((tm, LANES), lambda i, pos: (i, 0)),
                pl.BlockSpec((1, D_MODEL), lambda i, pos: (0, 0)),
                pl.BlockSpec(memory_space=pl.ANY),
            ],
            out_specs=pl.BlockSpec((tm, D_MODEL), lambda i, pos: (i, 0)),
            scratch_shapes=[pltpu.VMEM((tm * ROW_TILES, LANES), jnp.float32),
                            pltpu.VMEM((tm * ROW_TILES, LANES), jnp.float32),
                            pltpu.SemaphoreType.DMA(())],
        ),
        out_shape=jax.ShapeDtypeStruct((T, D_MODEL), jnp.float32),
        compiler_params=pltpu.CompilerParams(
            dimension_semantics=("arbitrary",), vmem_limit_bytes=VMEM_LIMIT),
        name="moe_combine",
    )(pos, x2, rf, g, ys)


def _rope_tables():
    def cos_sin(dim):
        inv = 1.0 / (ROPE_THETA ** (jnp.arange(0, dim, 2, dtype=jnp.float32) / dim))
        ang = jnp.arange(SEQ, dtype=jnp.float32)[:, None] * inv[None, :]
        return jnp.cos(ang), jnp.sin(ang)

    c16, s16 = cos_sin(MLA_ROPE_DIM)
    c32, s32 = cos_sin(HEAD_DIM)
    ones = jnp.ones((SEQ, 64), jnp.float32)
    zeros = jnp.zeros((SEQ, 64), jnp.float32)
    pad = jnp.zeros((SEQ, 32), jnp.float32)
    cm = jnp.concatenate([ones, c16, c16, pad], axis=1)
    sm = jnp.concatenate([zeros, -s16, s16, pad], axis=1)
    cs = jnp.tile(jnp.concatenate([c32, c32], axis=1), (1, 2))
    ss = jnp.tile(jnp.concatenate([-s32, s32], axis=1), (1, 2))
    cd = jnp.tile(jnp.concatenate([c16, c16], axis=1), (1, 4))
    sd = jnp.tile(jnp.concatenate([-s16, s16], axis=1), (1, 4))
    return cm, sm, cs, ss, cd, sd


def _layer_weights(w_in, w_uq, w_ukv, w_out):
    f32 = jnp.float32
    cq = w_in[:, 0:384]
    ckv = w_in[:, 384:640]
    kr = w_in[:, 640:672]
    sq = w_in[:, 672:1056] * (HEAD_DIM ** -0.5)
    sk = w_in[:, 1056:1184]
    sv = w_in[:, 1184:1312]
    dq = w_in[:, 1312:1568] * (DIFF_QK_DIM ** -0.5)
    dk = w_in[:, 1568:1824]
    dv = w_in[:, 1824:2080]
    kr_pad = jnp.concatenate([jnp.zeros((D_MODEL, 64), f32), kr, jnp.zeros((D_MODEL, 32), f32)], axis=1)
    half = SWA_HEADS // 2
    sq_h = sq.reshape(D_MODEL, SWA_HEADS, HEAD_DIM)
    sq_perm = jnp.stack([sq_h[:, :half], sq_h[:, half:]], axis=2).reshape(D_MODEL, SWA_HEADS * HEAD_DIM)
    w1 = jnp.concatenate([cq, ckv, kr_pad, sq_perm, sk, sv, dq, dk, dv], axis=1).astype(jnp.bfloat16)

    scale = (MLA_NOPE_DIM + MLA_ROPE_DIM) ** -0.5
    uq = (w_uq * scale).reshape(MLA_Q_RANK, MLA_HEADS, MLA_NOPE_DIM + MLA_ROPE_DIM)
    uq = jnp.concatenate([uq, jnp.zeros((MLA_Q_RANK, MLA_HEADS, 32), f32)], axis=2)
    wuq = uq.reshape(MLA_Q_RANK, MLA_HEADS * LANES).astype(jnp.bfloat16)
    ukv = w_ukv.reshape(MLA_KV_RANK, MLA_HEADS, MLA_NOPE_DIM + MLA_V_DIM)
    uk = jnp.concatenate([ukv[:, :, :MLA_NOPE_DIM], jnp.zeros((MLA_KV_RANK, MLA_HEADS, 64), f32)], axis=2)
    uv = ukv[:, :, MLA_NOPE_DIM:]
    wukv = jnp.concatenate([uk.reshape(MLA_KV_RANK, MLA_HEADS * LANES),
                            uv.reshape(MLA_KV_RANK, MLA_HEADS * MLA_V_DIM)], axis=1).astype(jnp.bfloat16)

    wm = w_out[0:384].astype(jnp.bfloat16)
    wo_s = w_out[384:768].reshape(SWA_HEADS, HEAD_DIM, D_MODEL)
    ws = jnp.stack([wo_s[:half], wo_s[half:]], axis=1).reshape(SWA_HEADS * HEAD_DIM, D_MODEL).astype(jnp.bfloat16)
    wd = w_out[768:1024].astype(jnp.bfloat16)
    return w1, wuq, wukv, wm, ws, wd


def _router_weights(w_rg, b_rg, w_re, b_re):
    f32 = jnp.float32
    wr = jnp.concatenate([w_rg, jnp.zeros((D_MODEL, EXPERT_LANE0 - N_GROUPS), f32), w_re,
                          jnp.zeros((D_MODEL, LANES - EXPERT_LANE0 - N_EXPERTS), f32)], axis=1)
    br = jnp.concatenate([b_rg, jnp.zeros((EXPERT_LANE0 - N_GROUPS,), f32), b_re,
                          jnp.zeros((LANES - EXPERT_LANE0 - N_EXPERTS,), f32)])[None, :]
    return wr, br


def _slot_plan(ri, cnt, n_tok):
    counts = cnt[0, EXPERT_LANE0:EXPERT_LANE0 + N_EXPERTS].astype(jnp.int32)
    padded = ((counts + MOE_TILE - 1) // MOE_TILE) * MOE_TILE
    ends = jnp.cumsum(padded)
    off = ends - padded
    pos = jnp.concatenate([off[ri[:, 0]] + ri[:, 2], off[ri[:, 1]] + ri[:, 3]]).astype(jnp.int32)
    n_tiles = (ends[-1] // MOE_TILE).astype(jnp.int32)
    max_tiles = (2 * n_tok) // MOE_TILE + N_EXPERTS
    starts = jnp.arange(max_tiles, dtype=jnp.int32) * MOE_TILE
    starts = jnp.minimum(starts, ends[-1] - MOE_TILE)
    tile_expert = jnp.sum((ends[None, :] <= starts[:, None]).astype(jnp.int32), axis=1)
    fill = jnp.where(padded > 0, ends - MOE_TILE, -1).astype(jnp.int32)
    return pos, fill, tile_expert, n_tiles[None], max_tiles * MOE_TILE


def kernel(x, attn_norm, w_in, mla_q_norm, mla_kv_norm, mla_w_uq, mla_w_ukv, swa_sink, diff_lq1, diff_lk1, diff_lq2, diff_lk2, diff_subln, w_out, ffn_norm, router_group, router_group_bias, router_expert, router_expert_bias, w_gate, w_up, w_down, final_norm):
    batch, seq, d = x.shape
    assert (seq, d) == (SEQ, D_MODEL)
    depth = w_in.shape[0]
    n_tok = batch * seq
    tables = _rope_tables()
    wg = w_gate.reshape(depth * N_EXPERTS, D_MODEL, EXPERT_FF)
    wu = w_up.reshape(depth * N_EXPERTS, D_MODEL, EXPERT_FF)
    wdn = w_down.reshape(depth * N_EXPERTS, EXPERT_FF, D_MODEL)
    xt = x.reshape(n_tok, D_MODEL)
    for l in range(depth):
        w1, wuq, wukv, wm, ws, wd = _layer_weights(w_in[l], mla_w_uq[l], mla_w_ukv[l], w_out[l])
        qm, km, vm, qs, ks, vs, qd, kd, vd = _proj_call(
            xt, attn_norm[l][None], w1, mla_q_norm[l][None], mla_kv_norm[l][None], wuq, wukv, tables)
        o_mla = _mla_call(qm, km, vm, batch)
        o_swa = _swa_call(swa_sink[l], qs, ks, vs, batch)
        lam_init = 0.8 - 0.6 * math.exp(-0.3 * l)
        subln2 = jnp.concatenate([diff_subln[l], diff_subln[l]])[None]
        o_diff = _diff_call(lam_init, diff_lq1[l][None], diff_lk1[l][None], diff_lq2[l][None],
                            diff_lk2[l][None], subln2, qd, kd, vd, batch)
        wr, br = _router_weights(router_group[l], router_group_bias[l],
                                 router_expert[l], router_expert_bias[l])
        x2, h2, ri, rf, cnt = _route_call(xt, o_mla, o_swa, o_diff, wm, ws, wd,
                                          ffn_norm[l][None], wr, br)
        pos, fill, tile_expert, n_tiles, n_slots = _slot_plan(ri, cnt, n_tok)
        xs = _push_call(pos, fill, n_tiles, h2, n_slots)
        ys = _expert_call(tile_expert, n_tiles, xs, wg, wu, wdn, l)
        xt = _combine_call(pos, x2, rf, final_norm[None], ys, l == depth - 1)
    return xt.reshape(batch, seq, d)
```

```python
import functools
import math

import jax
import jax.numpy as jnp
from jax import lax
from jax.experimental import pallas as pl
from jax.experimental.pallas import tpu as pltpu

D_MODEL = 1024
SEQ = 2048
HEAD_DIM = 64
ROPE_THETA = 10000.0
NORM_EPS = 1e-6
MLA_HEADS = 6
MLA_Q_RANK = 384
MLA_KV_RANK = 256
MLA_NOPE_DIM = 64
MLA_ROPE_DIM = 32
MLA_V_DIM = 64
SWA_HEADS = 6
SWA_KV_HEADS = 2
WINDOW = 128
DIFF_HEADS = 4
DIFF_QK_DIM = 32
DIFF_V_DIM = 64
N_GROUPS = 4
EXPERTS_PER_GROUP = 8
N_EXPERTS = N_GROUPS * EXPERTS_PER_GROUP
EXPERT_FF = 256

LANES = 128
ROW_TILES = D_MODEL // LANES
EXPERT_LANE0 = 8
PROJ_TILE = 512
ATTN_Q_TILE = 512
SWA_Q_TILE = 256
MOE_TILE = 256
MOVE_TILE = 256
ROUTE_CHUNK = 2048
RANK_CHUNK = 512
VMEM_LIMIT = 56 * 1024 * 1024

SEG_CQ = 0
SEG_CKV = SEG_CQ + MLA_Q_RANK
SEG_KR = SEG_CKV + MLA_KV_RANK
SEG_SQ = SEG_KR + LANES
SEG_SK = SEG_SQ + SWA_HEADS * HEAD_DIM
SEG_SV = SEG_SK + LANES
SEG_DQ = SEG_SV + LANES
SEG_DK = SEG_DQ + 2 * LANES
SEG_DV = SEG_DK + 2 * LANES
PROJ_WIDTH = SEG_DV + 2 * LANES

_NT = (((1,), (1,)), ((), ()))
LOG2E = math.log2(math.e)


def _rms(x, g):
    return x * lax.rsqrt(jnp.mean(x * x, axis=-1, keepdims=True) + NORM_EPS) * g


def _load_rows(ref, n):
    return jnp.concatenate([ref[pl.ds(c, n, stride=ROW_TILES), :] for c in range(ROW_TILES)], axis=1)


def _store_rows(ref, val):
    n = val.shape[0]
    for c in range(ROW_TILES):
        ref[pl.ds(c, n, stride=ROW_TILES), :] = val[:, c * LANES:(c + 1) * LANES]


def _rope(x, cos, sin, first_half, shift):
    up = pltpu.roll(x, LANES - shift, 1)
    dn = pltpu.roll(x, shift, 1)
    return x * cos + jnp.where(first_half, up, dn) * sin


def _proj_kernel(x_ref, g_ref, w1_ref, qn_ref, kvn_ref, wuq_ref, wukv_ref,
                 cm_ref, sm_ref, cs_ref, ss_ref, cd_ref, sd_ref,
                 qm_ref, km_ref, vm_ref, qs_ref, ks_ref, vs_ref,
                 qd_ref, kd_ref, vd_ref):
    x = x_ref[...]
    h = _rms(x, g_ref[...]).astype(jnp.bfloat16)
    u = jnp.dot(h, w1_ref[...], preferred_element_type=jnp.float32)

    lane = lax.broadcasted_iota(jnp.int32, (x.shape[0], LANES), 1)
    first_m = (lane >= 64) & (lane < 80)
    first_s = (lane % 64) < 32
    first_d = (lane % 32) < 16
    cm, sm = cm_ref[...], sm_ref[...]
    cs, ss = cs_ref[...], ss_ref[...]
    cd, sd = cd_ref[...], sd_ref[...]

    cq = _rms(u[:, SEG_CQ:SEG_CQ + MLA_Q_RANK], qn_ref[...]).astype(jnp.bfloat16)
    q = jnp.dot(cq, wuq_ref[...], preferred_element_type=jnp.float32)
    ckv = _rms(u[:, SEG_CKV:SEG_CKV + MLA_KV_RANK], kvn_ref[...]).astype(jnp.bfloat16)
    kv = jnp.dot(ckv, wukv_ref[...], preferred_element_type=jnp.float32)
    kr = _rope(u[:, SEG_KR:SEG_KR + LANES], cm, sm, first_m, 16)
    for hd in range(MLA_HEADS):
        sl = slice(hd * LANES, (hd + 1) * LANES)
        qm_ref[:, sl] = _rope(q[:, sl], cm, sm, first_m, 16).astype(jnp.bfloat16)
        km_ref[:, sl] = (kv[:, sl] + kr).astype(jnp.bfloat16)
    vm_ref[...] = kv[:, MLA_HEADS * LANES:].astype(jnp.bfloat16)

    for j in range(SWA_HEADS // 2):
        sl = slice(SEG_SQ + j * LANES, SEG_SQ + (j + 1) * LANES)
        qs_ref[:, j * LANES:(j + 1) * LANES] = _rope(u[:, sl], cs, ss, first_s, 32).astype(jnp.bfloat16)
    ks_ref[...] = _rope(u[:, SEG_SK:SEG_SK + LANES], cs, ss, first_s, 32).astype(jnp.bfloat16)
    vs_ref[...] = u[:, SEG_SV:SEG_SV + LANES].astype(jnp.bfloat16)

    for j in range(2):
        sq = slice(SEG_DQ + j * LANES, SEG_DQ + (j + 1) * LANES)
        sk = slice(SEG_DK + j * LANES, SEG_DK + (j + 1) * LANES)
        qd_ref[:, j * LANES:(j + 1) * LANES] = _rope(u[:, sq], cd, sd, first_d, 16).astype(jnp.bfloat16)
        kd_ref[:, j * LANES:(j + 1) * LANES] = _rope(u[:, sk], cd, sd, first_d, 16).astype(jnp.bfloat16)
    vd_ref[...] = u[:, SEG_DV:SEG_DV + 2 * LANES].astype(jnp.bfloat16)


def _proj_call(x, g, w1, qn, kvn, wuq, wukv, tables):
    T = x.shape[0]
    tm = PROJ_TILE
    n_seq = SEQ // tm
    tok = lambda i: (i, 0)
    const = lambda i: (0, 0)
    tab = lambda i: (i % n_seq, 0)
    widths = (6 * LANES, 6 * LANES, 3 * LANES, 3 * LANES, LANES, LANES,
              2 * LANES, 2 * LANES, 2 * LANES)
    return pl.pallas_call(
        _proj_kernel,
        grid=(T // tm,),
        in_specs=[
            pl.BlockSpec((tm, D_MODEL), tok),
            pl.BlockSpec((1, D_MODEL), const),
            pl.BlockSpec((D_MODEL, PROJ_WIDTH), const),
            pl.BlockSpec((1, MLA_Q_RANK), const),
            pl.BlockSpec((1, MLA_KV_RANK), const),
            pl.BlockSpec((MLA_Q_RANK, 6 * LANES), const),
            pl.BlockSpec((MLA_KV_RANK, 9 * LANES), const),
        ] + [pl.BlockSpec((tm, LANES), tab)] * 6,
        out_specs=[pl.BlockSpec((tm, w), tok) for w in widths],
        out_shape=[jax.ShapeDtypeStruct((T, w), jnp.bfloat16) for w in widths],
        compiler_params=pltpu.CompilerParams(
            dimension_semantics=("arbitrary",), vmem_limit_bytes=VMEM_LIMIT),
        name="proj",
    )(x, g, w1, qn, kvn, wuq, wukv, *tables)


def _softmax_pv(q, k, v):
    s = lax.dot_general(q, k, _NT, preferred_element_type=jnp.float32)
    m = jnp.max(s, axis=-1, keepdims=True)
    p = jnp.exp2(s - m)
    l = jnp.sum(p, axis=-1, keepdims=True)
    o = jnp.dot(p.astype(jnp.bfloat16), v, preferred_element_type=jnp.float32)
    return o / l


def _mla_kernel(q_ref, k_ref, v_ref, o_ref):
    v = v_ref[...]
    lane = lax.broadcasted_iota(jnp.int32, o_ref.shape, 1)
    o0 = _softmax_pv(q_ref[:, :LANES], k_ref[:, :LANES], v)
    o1 = _softmax_pv(q_ref[:, LANES:], k_ref[:, LANES:], v)
    o_ref[...] = jnp.where(lane < MLA_V_DIM, o0, o1).astype(o_ref.dtype)


def _mla_call(q, k, v, batch):
    T = q.shape[0]
    tq = ATTN_Q_TILE
    nq = SEQ // tq
    return pl.pallas_call(
        _mla_kernel,
        grid=(batch, MLA_HEADS // 2, nq),
        in_specs=[
            pl.BlockSpec((tq, 2 * LANES), lambda b, p, i: (b * nq + i, p)),
            pl.BlockSpec((SEQ, 2 * LANES), lambda b, p, i: (b, p)),
            pl.BlockSpec((SEQ, LANES), lambda b, p, i: (b, p)),
        ],
        out_specs=pl.BlockSpec((tq, LANES), lambda b, p, i: (b * nq + i, p)),
        out_shape=jax.ShapeDtypeStruct((T, MLA_HEADS * MLA_V_DIM), jnp.bfloat16),
        compiler_params=pltpu.CompilerParams(
            dimension_semantics=("arbitrary",) * 3, vmem_limit_bytes=VMEM_LIMIT),
        name="mla_attn",
    )(q, k, v)


def _swa_kernel(sink_ref, q_ref, k_ref, v_ref, o_ref):
    i = pl.program_id(1)
    nb = SEQ // WINDOW
    half = SWA_HEADS // 2
    rows = SWA_HEADS * WINDOW
    lane = lax.broadcasted_iota(jnp.int32, (WINDOW, LANES), 1)
    left = lane < HEAD_DIM
    row = lax.broadcasted_iota(jnp.int32, (rows, 3 * WINDOW), 0)
    col = lax.broadcasted_iota(jnp.int32, (rows, 3 * WINDOW), 1)
    rel = col - row % WINDOW
    band = (rel >= 0) & (rel <= 2 * WINDOW)
    head_col = lax.broadcasted_iota(jnp.int32, (rows, 1), 0) // WINDOW
    sink = jnp.zeros((rows, 1), jnp.float32)
    for hd in range(SWA_HEADS):
        sink = jnp.where(head_col == hd, sink_ref[hd], sink)
    for sub in range(SWA_Q_TILE // WINDOW):
        n = i * (SWA_Q_TILE // WINDOW) + sub
        prev = pl.multiple_of(jnp.maximum(n - 1, 0) * WINDOW, WINDOW)
        cur = pl.multiple_of(n * WINDOW, WINDOW)
        nxt = pl.multiple_of(jnp.minimum(n + 1, nb - 1) * WINDOW, WINDOW)
        kb = jnp.concatenate([k_ref[pl.ds(prev, WINDOW), :], k_ref[pl.ds(cur, WINDOW), :],
                              k_ref[pl.ds(nxt, WINDOW), :]], axis=0)
        vb = jnp.concatenate([v_ref[pl.ds(prev, WINDOW), :], v_ref[pl.ds(cur, WINDOW), :],
                              v_ref[pl.ds(nxt, WINDOW), :]], axis=0)
        lo_col = jnp.where(n > 0, 0, WINDOW)
        hi_col = jnp.where(n < nb - 1, 3 * WINDOW, 2 * WINDOW)
        valid = band & (col >= lo_col) & (col < hi_col)
        parts = [q_ref[sub * WINDOW:(sub + 1) * WINDOW, j * LANES:(j + 1) * LANES] for j in range(half)]
        zero = jnp.zeros_like(parts[0])
        qs = jnp.concatenate([jnp.where(left, p_, zero) for p_ in parts]
                             + [jnp.where(left, zero, p_) for p_ in parts], axis=0)
        s = lax.dot_general(qs, kb, _NT, preferred_element_type=jnp.float32)
        s = jnp.where(valid, s, -jnp.inf)
        m = jnp.maximum(jnp.max(s, axis=-1, keepdims=True), sink)
        p = jnp.exp2(s - m)
        l = jnp.sum(p, axis=-1, keepdims=True) + jnp.exp2(sink - m)
        o = jnp.dot(p.astype(jnp.bfloat16), vb, preferred_element_type=jnp.float32) / l
        for j in range(half):
            o_ref[sub * WINDOW:(sub + 1) * WINDOW, j * LANES:(j + 1) * LANES] = jnp.where(
                left, o[j * WINDOW:(j + 1) * WINDOW], o[(j + half) * WINDOW:(j + half + 1) * WINDOW]
            ).astype(o_ref.dtype)


def _swa_call(sink, q, k, v, batch):
    T = q.shape[0]
    tq = SWA_Q_TILE
    nq = SEQ // tq
    return pl.pallas_call(
        _swa_kernel,
        grid=(batch, nq),
        in_specs=[
            pl.BlockSpec(memory_space=pltpu.SMEM),
            pl.BlockSpec((tq, 3 * LANES), lambda b, i: (b * nq + i, 0)),
            pl.BlockSpec((SEQ, LANES), lambda b, i: (b, 0)),
            pl.BlockSpec((SEQ, LANES), lambda b, i: (b, 0)),
        ],
        out_specs=pl.BlockSpec((tq, 3 * LANES), lambda b, i: (b * nq + i, 0)),
        out_shape=jax.ShapeDtypeStruct((T, SWA_HEADS * HEAD_DIM), jnp.bfloat16),
        compiler_params=pltpu.CompilerParams(
            dimension_semantics=("arbitrary",) * 2, vmem_limit_bytes=VMEM_LIMIT),
        name="swa_attn",
    )(sink, q, k, v)


def _diff_kernel(lam_init, lq1_ref, lk1_ref, lq2_ref, lk2_ref, subln_ref,
                 q_ref, k_ref, v_ref, o_ref):
    lam = (jnp.exp(jnp.sum(lq1_ref[...] * lk1_ref[...], axis=-1, keepdims=True))
           - jnp.exp(jnp.sum(lq2_ref[...] * lk2_ref[...], axis=-1, keepdims=True))
           + lam_init)
    q = q_ref[...]
    k = k_ref[...]
    v = v_ref[...]
    lane = lax.broadcasted_iota(jnp.int32, q.shape, 1)
    zero = jnp.zeros_like(q)
    outs = []
    for side in range(2):
        comp = []
        for c in range(2):
            lo = side * DIFF_V_DIM + c * DIFF_QK_DIM
            qc = jnp.where((lane >= lo) & (lane < lo + DIFF_QK_DIM), q, zero)
            comp.append(_softmax_pv(qc, k, v))
        outs.append(comp[0] - lam * comp[1])
    left = lane < DIFF_V_DIM
    o = jnp.where(left, outs[0], outs[1])
    sq = o * o
    ms_l = jnp.sum(jnp.where(left, sq, 0.0), axis=-1, keepdims=True) * (1.0 / DIFF_V_DIM)
    ms_r = jnp.sum(jnp.where(left, 0.0, sq), axis=-1, keepdims=True) * (1.0 / DIFF_V_DIM)
    r = jnp.where(left, lax.rsqrt(ms_l + NORM_EPS), lax.rsqrt(ms_r + NORM_EPS))
    o_ref[...] = (o * r * subln_ref[...] * (1.0 - lam_init)).astype(o_ref.dtype)


def _diff_call(lam_init, lq1, lk1, lq2, lk2, subln2, q, k, v, batch):
    T = q.shape[0]
    tq = ATTN_Q_TILE
    nq = SEQ // tq
    small = lambda b, p, i: (0, 0)
    return pl.pallas_call(
        functools.partial(_diff_kernel, lam_init),
        grid=(batch, DIFF_HEADS // 2, nq),
        in_specs=[pl.BlockSpec((1, DIFF_QK_DIM), small)] * 4 + [
            pl.BlockSpec((1, LANES), small),
            pl.BlockSpec((tq, LANES), lambda b, p, i: (b * nq + i, p)),
            pl.BlockSpec((SEQ, LANES), lambda b, p, i: (b, p)),
            pl.BlockSpec((SEQ, LANES), lambda b, p, i: (b, p)),
        ],
        out_specs=pl.BlockSpec((tq, LANES), lambda b, p, i: (b * nq + i, p)),
        out_shape=jax.ShapeDtypeStruct((T, DIFF_HEADS * DIFF_V_DIM), jnp.bfloat16),
        compiler_params=pltpu.CompilerParams(
            dimension_semantics=("arbitrary",) * 3, vmem_limit_bytes=VMEM_LIMIT),
        name="diff_attn",
    )(lq1, lk1, lq2, lk2, subln2, q, k, v)


def _outproj_kernel(x_ref, om_ref, os_ref, od_ref, wm_ref, ws_ref, wd_ref, g_ref,
                    wr2_ref, br_ref, x2_ref, h2_ref, lt_ref):
    x2 = (x_ref[...]
          + jnp.dot(om_ref[...], wm_ref[...], preferred_element_type=jnp.float32)
          + jnp.dot(os_ref[...], ws_ref[...], preferred_element_type=jnp.float32)
          + jnp.dot(od_ref[...], wd_ref[...], preferred_element_type=jnp.float32))
    x2_ref[...] = x2
    h2 = _rms(x2, g_ref[...])
    _store_rows(h2_ref, h2)

    hi = h2.astype(jnp.bfloat16)
    lo = (h2 - hi.astype(jnp.float32)).astype(jnp.bfloat16)
    a = jnp.dot(hi, wr2_ref[...], preferred_element_type=jnp.float32)
    b = jnp.dot(lo, wr2_ref[:, :LANES], preferred_element_type=jnp.float32)
    logits = a[:, :LANES] + a[:, LANES:] + b + br_ref[...]
    lt_ref[...] = logits.T


def _outproj_call(x, om, os_, od, wm, ws, wd, g, wr2, br):
    T = x.shape[0]
    tm = PROJ_TILE
    tok = lambda i: (i, 0)
    const = lambda i: (0, 0)
    return pl.pallas_call(
        _outproj_kernel,
        grid=(T // tm,),
        in_specs=[
            pl.BlockSpec((tm, D_MODEL), tok),
            pl.BlockSpec((tm, 3 * LANES), tok),
            pl.BlockSpec((tm, 3 * LANES), tok),
            pl.BlockSpec((tm, 2 * LANES), tok),
            pl.BlockSpec((3 * LANES, D_MODEL), const),
            pl.BlockSpec((3 * LANES, D_MODEL), const),
            pl.BlockSpec((2 * LANES, D_MODEL), const),
            pl.BlockSpec((1, D_MODEL), const),
            pl.BlockSpec((D_MODEL, 2 * LANES), const),
            pl.BlockSpec((1, LANES), const),
        ],
        out_specs=[
            pl.BlockSpec((tm, D_MODEL), tok),
            pl.BlockSpec((tm * ROW_TILES, LANES), tok),
            pl.BlockSpec((LANES, tm), lambda i: (0, i)),
        ],
        out_shape=[
            jax.ShapeDtypeStruct((T, D_MODEL), jnp.float32),
            jax.ShapeDtypeStruct((T * ROW_TILES, LANES), jnp.float32),
            jax.ShapeDtypeStruct((LANES, T), jnp.float32),
        ],
        compiler_params=pltpu.CompilerParams(
            dimension_semantics=("arbitrary",), vmem_limit_bytes=VMEM_LIMIT),
        name="outproj",
    )(x, om, os_, od, wm, ws, wd, g, wr2, br)


def _route_kernel(lt_ref, ri_ref, rf_ref, cnt_ref, carry_ref):
    step = pl.program_id(0)

    @pl.when(step == 0)
    def _():
        carry_ref[...] = jnp.zeros_like(carry_ref)

    tc = lt_ref.shape[1]
    neg = -jnp.inf
    row8 = lax.broadcasted_iota(jnp.int32, (8, tc), 0)
    is_g = row8 < N_GROUPS
    gl = jnp.where(is_g, lt_ref[0:8, :], neg)
    gmax = jnp.max(gl, axis=0, keepdims=True)
    gidx = jnp.min(jnp.where(gl == gmax, row8, 8), axis=0, keepdims=True)
    gsum = jnp.sum(jnp.where(is_g, jnp.exp(gl - gmax), 0.0), axis=0, keepdims=True)
    g_gate = 1.0 / gsum
    gidx8 = jnp.broadcast_to(gidx, (8, tc))
    e_in = lt_ref[EXPERT_LANE0:EXPERT_LANE0 + 8, :]
    for g in range(1, N_GROUPS):
        lo = EXPERT_LANE0 + g * EXPERTS_PER_GROUP
        e_in = jnp.where(gidx8 == g, lt_ref[lo:lo + EXPERTS_PER_GROUP, :], e_in)
    m1 = jnp.max(e_in, axis=0, keepdims=True)
    i1 = jnp.min(jnp.where(e_in == m1, row8, 8), axis=0, keepdims=True)
    e_rest = jnp.where(row8 == i1, neg, e_in)
    m2 = jnp.max(e_rest, axis=0, keepdims=True)
    i2 = jnp.min(jnp.where(e_rest == m2, row8, 8), axis=0, keepdims=True)
    t = jnp.exp(m2 - m1)
    w1 = g_gate / (1.0 + t)
    w2 = g_gate * t / (1.0 + t)
    e1 = gidx * EXPERTS_PER_GROUP + i1
    e2 = gidx * EXPERTS_PER_GROUP + i2

    rowe = lax.broadcasted_iota(jnp.int32, (N_EXPERTS, tc), 0)
    hit1 = rowe == e1
    hit2 = rowe == e2
    onehot = jnp.where(hit1 | hit2, 1.0, 0.0)
    r = lax.broadcasted_iota(jnp.int32, (RANK_CHUNK, RANK_CHUNK), 0)
    c = lax.broadcasted_iota(jnp.int32, (RANK_CHUNK, RANK_CHUNK), 1)
    upper = jnp.where(r < c, 1.0, 0.0).astype(jnp.bfloat16)
    carry = carry_ref[:, 0:1]
    parts = []
    for j in range(tc // RANK_CHUNK):
        oh = onehot[:, j * RANK_CHUNK:(j + 1) * RANK_CHUNK]
        parts.append(jnp.dot(oh.astype(jnp.bfloat16), upper, preferred_element_type=jnp.float32) + carry)
        carry = carry + jnp.sum(oh, axis=1, keepdims=True)
    before = jnp.concatenate(parts, axis=1)
    rank1 = jnp.sum(jnp.where(hit1, before, 0.0), axis=0, keepdims=True).astype(jnp.int32)
    rank2 = jnp.sum(jnp.where(hit2, before, 0.0), axis=0, keepdims=True).astype(jnp.int32)
    carry_ref[...] = jnp.broadcast_to(carry, carry_ref.shape)
    cnt_ref[...] = jnp.broadcast_to(carry, cnt_ref.shape)

    ri_ref[...] = jnp.where(row8 == 0, e1, jnp.where(row8 == 1, e2, jnp.where(
        row8 == 2, rank1, jnp.where(row8 == 3, rank2, 0))))
    rf_ref[...] = jnp.where(row8 == 0, w1, jnp.where(row8 == 1, w2, 0.0))


def _route_call(lt):
    T = lt.shape[1]
    tc = ROUTE_CHUNK
    return pl.pallas_call(
        _route_kernel,
        grid=(T // tc,),
        in_specs=[pl.BlockSpec((LANES, tc), lambda i: (0, i))],
        out_specs=[
            pl.BlockSpec((8, tc), lambda i: (0, i)),
            pl.BlockSpec((8, tc), lambda i: (0, i)),
            pl.BlockSpec((N_EXPERTS, LANES), lambda i: (0, 0)),
        ],
        out_shape=[
            jax.ShapeDtypeStruct((8, T), jnp.int32),
            jax.ShapeDtypeStruct((8, T), jnp.float32),
            jax.ShapeDtypeStruct((N_EXPERTS, LANES), jnp.float32),
        ],
        scratch_shapes=[pltpu.VMEM((N_EXPERTS, LANES), jnp.float32)],
        compiler_params=pltpu.CompilerParams(
            dimension_semantics=("arbitrary",), vmem_limit_bytes=VMEM_LIMIT),
        name="route",
    )(lt)


def _plan_kernel(ri_ref, cnt_ref, pos_ref, te_ref, meta_ref):
    T = ri_ref.shape[1]
    counts = cnt_ref[...].astype(jnp.int32)
    padded = (counts + (MOE_TILE - 1)) // MOE_TILE * MOE_TILE
    r = lax.broadcasted_iota(jnp.int32, (N_EXPERTS, N_EXPERTS), 0)
    c = lax.broadcasted_iota(jnp.int32, (N_EXPERTS, N_EXPERTS), 1)
    lower = jnp.where(c < r, 1.0, 0.0)
    off = jnp.dot(lower, padded.astype(jnp.float32), preferred_element_type=jnp.float32,
                  precision=lax.Precision.HIGHEST).astype(jnp.int32)
    ends = off + padded
    total = jnp.max(ends, axis=0, keepdims=True)

    rowe = lax.broadcasted_iota(jnp.int32, (N_EXPERTS, T), 0)
    off_col = off[:, 0:1]
    row8 = lax.broadcasted_iota(jnp.int32, (8, T), 0)
    pos1 = jnp.sum(jnp.where(rowe == ri_ref[0:1, :], off_col, 0), axis=0, keepdims=True) + ri_ref[2:3, :]
    pos2 = jnp.sum(jnp.where(rowe == ri_ref[1:2, :], off_col, 0), axis=0, keepdims=True) + ri_ref[3:4, :]
    pos_ref[...] = jnp.where(row8 == 0, pos1, jnp.where(row8 == 1, pos2, 0))

    n_lanes = te_ref.shape[1]
    start = lax.broadcasted_iota(jnp.int32, (N_EXPERTS, n_lanes), 1) * MOE_TILE
    start = jnp.minimum(start, total[:, 0:1] - MOE_TILE)
    te = jnp.sum((ends[:, 0:1] <= start).astype(jnp.int32), axis=0, keepdims=True)
    te_ref[...] = jnp.broadcast_to(te, te_ref.shape)
    lane = lax.broadcasted_iota(jnp.int32, (N_EXPERTS, LANES), 1)
    fill = jnp.where(padded > 0, ends - MOE_TILE, -1)
    meta_ref[...] = jnp.where(lane == 0, fill, total // MOE_TILE)


def _plan_call(ri, cnt, max_tiles):
    T = ri.shape[1]
    n_lanes = -(-max_tiles // LANES) * LANES
    full = lambda i: (0, 0)
    return pl.pallas_call(
        _plan_kernel,
        grid=(1,),
        in_specs=[pl.BlockSpec((8, T), full), pl.BlockSpec((N_EXPERTS, LANES), full)],
        out_specs=[pl.BlockSpec((8, T), full), pl.BlockSpec((8, n_lanes), full),
                   pl.BlockSpec((N_EXPERTS, LANES), full)],
        out_shape=[jax.ShapeDtypeStruct((8, T), jnp.int32),
                   jax.ShapeDtypeStruct((8, n_lanes), jnp.int32),
                   jax.ShapeDtypeStruct((N_EXPERTS, LANES), jnp.int32)],
        compiler_params=pltpu.CompilerParams(
            dimension_semantics=("arbitrary",), vmem_limit_bytes=VMEM_LIMIT),
        name="slot_plan",
    )(ri, cnt)


def _row(ref, idx):
    return ref.at[pl.ds(pl.multiple_of(idx * ROW_TILES, ROW_TILES), ROW_TILES), :]


def _push_kernel(pos_ref, fill_ref, nt_ref, h_ref, xs_ref, zero_ref, sem):
    step = pl.program_id(0)
    tm = h_ref.shape[0] // ROW_TILES
    n_tok = pl.num_programs(0) * tm
    fill_rows = MOE_TILE * ROW_TILES
    max_tiles = xs_ref.shape[0] // fill_rows

    def fill_copy(slot):
        start = pl.multiple_of(slot * ROW_TILES, fill_rows)
        return pltpu.make_async_copy(zero_ref, xs_ref.at[pl.ds(start, fill_rows), :], sem)

    @pl.when(step == 0)
    def _():
        zero_ref[...] = jnp.zeros_like(zero_ref)
        for e in range(N_EXPERTS):
            @pl.when(fill_ref[e] >= 0)
            def _():
                fill_copy(fill_ref[e]).start()

        def tail_start(j, carry):
            fill_copy(j * MOE_TILE).start()
            return carry

        def tail_wait(j, carry):
            fill_copy(j * MOE_TILE).wait()
            return carry

        lax.fori_loop(nt_ref[0], max_tiles, tail_start, 0)
        for e in range(N_EXPERTS):
            @pl.when(fill_ref[e] >= 0)
            def _():
                fill_copy(fill_ref[e]).wait()
        lax.fori_loop(nt_ref[0], max_tiles, tail_wait, 0)

    def issue(r, carry):
        t = step * tm + r
        for k in range(2):
            pltpu.make_async_copy(_row(h_ref, r), _row(xs_ref, pos_ref[k * n_tok + t]), sem).start(priority=k)
        return carry

    lax.fori_loop(0, tm, issue, 0, unroll=8)

    def drain(r, carry):
        for k in range(2):
            pltpu.make_async_copy(_row(h_ref, 0), _row(xs_ref, 0), sem).wait()
        return carry

    lax.fori_loop(0, tm, drain, 0, unroll=8)


def _push_call(pos, fill, n_tiles, h2, n_slots):
    tm = MOVE_TILE
    n_tok = h2.shape[0] // ROW_TILES
    return pl.pallas_call(
        _push_kernel,
        grid_spec=pltpu.PrefetchScalarGridSpec(
            num_scalar_prefetch=3,
            grid=(n_tok // tm,),
            in_specs=[pl.BlockSpec((tm * ROW_TILES, LANES), lambda i, pos, fill, nt: (i, 0))],
            out_specs=pl.BlockSpec(memory_space=pl.ANY),
            scratch_shapes=[pltpu.VMEM((MOE_TILE * ROW_TILES, LANES), jnp.float32),
                            pltpu.SemaphoreType.DMA(())],
        ),
        out_shape=jax.ShapeDtypeStruct((n_slots * ROW_TILES, LANES), jnp.float32),
        compiler_params=pltpu.CompilerParams(
            dimension_semantics=("arbitrary",), vmem_limit_bytes=VMEM_LIMIT),
        name="moe_push",
    )(pos, fill, n_tiles, h2)


def _expert_kernel(te_ref, nt_ref, xs_ref, wg_ref, wu_ref, wd_ref, ys_ref,
                   wg_s, wu_s, wd_s):
    j = pl.program_id(0)

    @pl.when(j < nt_ref[0])
    def _():
        changed = (j == 0) | (te_ref[j] != te_ref[jnp.maximum(j - 1, 0)])

        @pl.when(changed)
        def _():
            wg_s[...] = wg_ref[...].astype(jnp.bfloat16)
            wu_s[...] = wu_ref[...].astype(jnp.bfloat16)
            wd_s[...] = wd_ref[...].astype(jnp.bfloat16)

        xb = _load_rows(xs_ref, MOE_TILE).astype(jnp.bfloat16)
        a = jnp.dot(xb, wg_s[...], preferred_element_type=jnp.float32)
        u = jnp.dot(xb, wu_s[...], preferred_element_type=jnp.float32)
        hid = (a * jax.nn.sigmoid(a) * u).astype(jnp.bfloat16)
        _store_rows(ys_ref, jnp.dot(hid, wd_s[...], preferred_element_type=jnp.float32))


def _expert_call(tile_expert, n_tiles, xs, wg, wu, wd, layer):
    n_slots = xs.shape[0] // ROW_TILES
    max_tiles = n_slots // MOE_TILE
    base = layer * N_EXPERTS
    row = lambda j, te, nt: (jnp.minimum(j, nt[0] - 1), 0)
    wsel = lambda j, te, nt: (base + te[j], 0, 0)
    return pl.pallas_call(
        _expert_kernel,
        grid_spec=pltpu.PrefetchScalarGridSpec(
            num_scalar_prefetch=2,
            grid=(max_tiles,),
            in_specs=[
                pl.BlockSpec((MOE_TILE * ROW_TILES, LANES), row),
                pl.BlockSpec((None, D_MODEL, EXPERT_FF), wsel),
                pl.BlockSpec((None, D_MODEL, EXPERT_FF), wsel),
                pl.BlockSpec((None, EXPERT_FF, D_MODEL), wsel),
            ],
            out_specs=pl.BlockSpec((MOE_TILE * ROW_TILES, LANES), row),
            scratch_shapes=[pltpu.VMEM((D_MODEL, EXPERT_FF), jnp.bfloat16),
                            pltpu.VMEM((D_MODEL, EXPERT_FF), jnp.bfloat16),
                            pltpu.VMEM((EXPERT_FF, D_MODEL), jnp.bfloat16)],
        ),
        out_shape=jax.ShapeDtypeStruct((n_slots * ROW_TILES, LANES), jnp.float32),
        input_output_aliases={2: 0},
        compiler_params=pltpu.CompilerParams(
            dimension_semantics=("arbitrary",), vmem_limit_bytes=VMEM_LIMIT),
        name="moe_experts",
    )(tile_expert, n_tiles, xs, wg, wu, wd)


def _combine_kernel(final, pos_ref, x2_ref, rf_ref, g_ref, ys_ref, o_ref, g0, g1, sem):
    step = pl.program_id(0)
    tm = x2_ref.shape[0]
    n_tok = pl.num_programs(0) * tm

    def issue(r, carry):
        t = step * tm + r
        pltpu.make_async_copy(_row(ys_ref, pos_ref[t]), _row(g0, r), sem).start(priority=0)
        pltpu.make_async_copy(_row(ys_ref, pos_ref[n_tok + t]), _row(g1, r), sem).start(priority=1)
        return carry

    lax.fori_loop(0, tm, issue, 0, unroll=8)

    def drain(r, carry):
        for k in range(2):
            pltpu.make_async_copy(_row(ys_ref, 0), _row(g0, 0), sem).wait()
        return carry

    lax.fori_loop(0, tm, drain, 0, unroll=8)

    rf = rf_ref[...]
    y = x2_ref[...] + rf[:, 0:1] * _load_rows(g0, tm) + rf[:, 1:2] * _load_rows(g1, tm)
    if final:
        y = _rms(y, g_ref[...])
    o_ref[...] = y


def _combine_call(pos, x2, rf, g, ys, final):
    T = x2.shape[0]
    tm = MOVE_TILE
    return pl.pallas_call(
        functools.partial(_combine_kernel, final),
        grid_spec=pltpu.PrefetchScalarGridSpec(
            num_scalar_prefetch=1,
            grid=(T // tm,),
            in_specs=[
                pl.BlockSpec((tm, D_MODEL), lambda i, pos: (i, 0)),
                pl.BlockSpec((tm, 2), lambda i, pos: (i, 0)),
                pl.BlockSpec((1, D_MODEL), lambda i, pos: (0, 0)),
                pl.BlockSpec(memory_space=pl.ANY),
            ],
            out_specs=pl.BlockSpec((tm, D_MODEL), lambda i, pos: (i, 0)),
            scratch_shapes=[pltpu.VMEM((tm * ROW_TILES, LANES), jnp.float32),
                            pltpu.VMEM((tm * ROW_TILES, LANES), jnp.float32),
                            pltpu.SemaphoreType.DMA(())],
        ),
        out_shape=jax.ShapeDtypeStruct((T, D_MODEL), jnp.float32),
        compiler_params=pltpu.CompilerParams(
            dimension_semantics=("arbitrary",), vmem_limit_bytes=VMEM_LIMIT),
        name="moe_combine",
    )(pos, x2, rf, g, ys)


def _rope_tables():
    def cos_sin(dim):
        inv = 1.0 / (ROPE_THETA ** (jnp.arange(0, dim, 2, dtype=jnp.float32) / dim))
        ang = jnp.arange(SEQ, dtype=jnp.float32)[:, None] * inv[None, :]
        return jnp.cos(ang), jnp.sin(ang)

    c16, s16 = cos_sin(MLA_ROPE_DIM)
    c32, s32 = cos_sin(HEAD_DIM)
    ones = jnp.ones((SEQ, 64), jnp.float32)
    zeros = jnp.zeros((SEQ, 64), jnp.float32)
    pad = jnp.zeros((SEQ, 32), jnp.float32)
    cm = jnp.concatenate([ones, c16, c16, pad], axis=1)
    sm = jnp.concatenate([zeros, -s16, s16, pad], axis=1)
    cs = jnp.tile(jnp.concatenate([c32, c32], axis=1), (1, 2))
    ss = jnp.tile(jnp.concatenate([-s32, s32], axis=1), (1, 2))
    cd = jnp.tile(jnp.concatenate([c16, c16], axis=1), (1, 4))
    sd = jnp.tile(jnp.concatenate([-s16, s16], axis=1), (1, 4))
    return cm, sm, cs, ss, cd, sd


def _layer_weights(w_in, w_uq, w_ukv, w_out):
    f32 = jnp.float32
    cq = w_in[:, 0:384]
    ckv = w_in[:, 384:640]
    kr = w_in[:, 640:672]
    sq = w_in[:, 672:1056] * (LOG2E * HEAD_DIM ** -0.5)
    sk = w_in[:, 1056:1184]
    sv = w_in[:, 1184:1312]
    dq = w_in[:, 1312:1568] * (LOG2E * DIFF_QK_DIM ** -0.5)
    dk = w_in[:, 1568:1824]
    dv = w_in[:, 1824:2080]
    kr_pad = jnp.concatenate([jnp.zeros((D_MODEL, 64), f32), kr, jnp.zeros((D_MODEL, 32), f32)], axis=1)
    half = SWA_HEADS // 2
    sq_h = sq.reshape(D_MODEL, SWA_HEADS, HEAD_DIM)
    sq_perm = jnp.stack([sq_h[:, :half], sq_h[:, half:]], axis=2).reshape(D_MODEL, SWA_HEADS * HEAD_DIM)
    w1 = jnp.concatenate([cq, ckv, kr_pad, sq_perm, sk, sv, dq, dk, dv], axis=1).astype(jnp.bfloat16)

    scale = LOG2E * (MLA_NOPE_DIM + MLA_ROPE_DIM) ** -0.5
    uq = (w_uq * scale).reshape(MLA_Q_RANK, MLA_HEADS, MLA_NOPE_DIM + MLA_ROPE_DIM)
    uq = jnp.concatenate([uq, jnp.zeros((MLA_Q_RANK, MLA_HEADS, 32), f32)], axis=2)
    wuq = uq.reshape(MLA_Q_RANK, MLA_HEADS * LANES).astype(jnp.bfloat16)
    ukv = w_ukv.reshape(MLA_KV_RANK, MLA_HEADS, MLA_NOPE_DIM + MLA_V_DIM)
    uk = jnp.concatenate([ukv[:, :, :MLA_NOPE_DIM], jnp.zeros((MLA_KV_RANK, MLA_HEADS, 64), f32)], axis=2)
    uv = ukv[:, :, MLA_NOPE_DIM:]
    wukv = jnp.concatenate([uk.reshape(MLA_KV_RANK, MLA_HEADS * LANES),
                            uv.reshape(MLA_KV_RANK, MLA_HEADS * MLA_V_DIM)], axis=1).astype(jnp.bfloat16)

    wm = w_out[0:384].astype(jnp.bfloat16)
    wo_s = w_out[384:768].reshape(SWA_HEADS, HEAD_DIM, D_MODEL)
    ws = jnp.stack([wo_s[:half], wo_s[half:]], axis=1).reshape(SWA_HEADS * HEAD_DIM, D_MODEL).astype(jnp.bfloat16)
    wd = w_out[768:1024].astype(jnp.bfloat16)
    return w1, wuq, wukv, wm, ws, wd


def _router_weights(w_rg, b_rg, w_re, b_re):
    f32 = jnp.float32
    wr = jnp.concatenate([w_rg, jnp.zeros((D_MODEL, EXPERT_LANE0 - N_GROUPS), f32), w_re,
                          jnp.zeros((D_MODEL, LANES - EXPERT_LANE0 - N_EXPERTS), f32)], axis=1)
    br = jnp.concatenate([b_rg, jnp.zeros((EXPERT_LANE0 - N_GROUPS,), f32), b_re,
                          jnp.zeros((LANES - EXPERT_LANE0 - N_EXPERTS,), f32)])[None, :]
    hi = wr.astype(jnp.bfloat16)
    lo = (wr - hi.astype(f32)).astype(jnp.bfloat16)
    return jnp.concatenate([hi, lo], axis=1), br


def kernel(x, attn_norm, w_in, mla_q_norm, mla_kv_norm, mla_w_uq, mla_w_ukv, swa_sink, diff_lq1, diff_lk1, diff_lq2, diff_lk2, diff_subln, w_out, ffn_norm, router_group, router_group_bias, router_expert, router_expert_bias, w_gate, w_up, w_down, final_norm):
    batch, seq, d = x.shape
    assert (seq, d) == (SEQ, D_MODEL)
    depth = w_in.shape[0]
    n_tok = batch * seq
    tables = _rope_tables()
    wg = w_gate.reshape(depth * N_EXPERTS, D_MODEL, EXPERT_FF)
    wu = w_up.reshape(depth * N_EXPERTS, D_MODEL, EXPERT_FF)
    wdn = w_down.reshape(depth * N_EXPERTS, EXPERT_FF, D_MODEL)
    xt = x.reshape(n_tok, D_MODEL)
    for l in range(depth):
        w1, wuq, wukv, wm, ws, wd = _layer_weights(w_in[l], mla_w_uq[l], mla_w_ukv[l], w_out[l])
        qm, km, vm, qs, ks, vs, qd, kd, vd = _proj_call(
            xt, attn_norm[l][None], w1, mla_q_norm[l][None], mla_kv_norm[l][None], wuq, wukv, tables)
        o_mla = _mla_call(qm, km, vm, batch)
        o_swa = _swa_call(swa_sink[l] * LOG2E, qs, ks, vs, batch)
        lam_init = 0.8 - 0.6 * math.exp(-0.3 * l)
        subln2 = jnp.concatenate([diff_subln[l], diff_subln[l]])[None]
        o_diff = _diff_call(lam_init, diff_lq1[l][None], diff_lk1[l][None], diff_lq2[l][None],
                            diff_lk2[l][None], subln2, qd, kd, vd, batch)
        wr2, br = _router_weights(router_group[l], router_group_bias[l],
                                  router_expert[l], router_expert_bias[l])
        x2, h2, lt = _outproj_call(xt, o_mla, o_swa, o_diff, wm, ws, wd, ffn_norm[l][None], wr2, br)
        ri, rf, cnt = _route_call(lt)
        max_tiles = (2 * n_tok) // MOE_TILE + N_EXPERTS
        pos8, te8, meta = _plan_call(ri, cnt, max_tiles)
        pos = pos8[0:2].reshape(2 * n_tok)
        fill, n_tiles = meta[:, 0], meta[0, 1:2]
        xs = _push_call(pos, fill, n_tiles, h2, max_tiles * MOE_TILE)
        ys = _expert_call(te8[0, :max_tiles], n_tiles, xs, wg, wu, wdn, l)
        xt = _combine_call(pos, x2, rf[0:2].T, final_norm[None], ys, l == depth - 1)
    return xt.reshape(batch, seq, d)
```

```python
import functools
import math

import jax
import jax.numpy as jnp
from jax import lax
from jax.experimental import pallas as pl
from jax.experimental.pallas import tpu as pltpu

D_MODEL = 1024
SEQ = 2048
HEAD_DIM = 64
ROPE_THETA = 10000.0
NORM_EPS = 1e-6
MLA_HEADS = 6
MLA_Q_RANK = 384
MLA_KV_RANK = 256
MLA_NOPE_DIM = 64
MLA_ROPE_DIM = 32
MLA_V_DIM = 64
SWA_HEADS = 6
SWA_KV_HEADS = 2
WINDOW = 128
DIFF_HEADS = 4
DIFF_QK_DIM = 32
DIFF_V_DIM = 64
N_GROUPS = 4
EXPERTS_PER_GROUP = 8
N_EXPERTS = N_GROUPS * EXPERTS_PER_GROUP
EXPERT_FF = 256

LANES = 128
ROW_TILES = D_MODEL // LANES
EXPERT_LANE0 = 8
PROJ_TILE = 512
ATTN_Q_TILE = 1024
DIFF_Q_TILE = 512
ATTN_SUB = 256
SWA_Q_TILE = 256
MOE_TILE = 256
MOVE_TILE = 256
ROUTE_CHUNK = 2048
RANK_CHUNK = 512
VMEM_LIMIT = 56 * 1024 * 1024

SEG_CQ = 0
SEG_CKV = SEG_CQ + MLA_Q_RANK
SEG_KR = SEG_CKV + MLA_KV_RANK
SEG_SQ = SEG_KR + LANES
SEG_SK = SEG_SQ + SWA_HEADS * HEAD_DIM
SEG_SV = SEG_SK + LANES
SEG_DQ = SEG_SV + LANES
SEG_DK = SEG_DQ + 2 * LANES
SEG_DV = SEG_DK + 2 * LANES
PROJ_WIDTH = SEG_DV + 2 * LANES

_NT = (((1,), (1,)), ((), ()))
LOG2E = math.log2(math.e)


def _rms(x, g):
    return x * lax.rsqrt(jnp.mean(x * x, axis=-1, keepdims=True) + NORM_EPS) * g


def _load_rows(ref, n):
    return jnp.concatenate([ref[pl.ds(c, n, stride=ROW_TILES), :] for c in range(ROW_TILES)], axis=1)


def _store_rows(ref, val):
    n = val.shape[0]
    for c in range(ROW_TILES):
        ref[pl.ds(c, n, stride=ROW_TILES), :] = val[:, c * LANES:(c + 1) * LANES]


def _rope(x, cos, sin, first_half, shift):
    up = pltpu.roll(x, LANES - shift, 1)
    dn = pltpu.roll(x, shift, 1)
    return x * cos + jnp.where(first_half, up, dn) * sin


def _proj_kernel(x_ref, g_ref, w1_ref, qn_ref, kvn_ref, wuq_ref, wukv_ref,
                 cm_ref, sm_ref, cs_ref, ss_ref, cd_ref, sd_ref,
                 qm_ref, km_ref, vm_ref, qs_ref, ks_ref, vs_ref,
                 qd_ref, kd_ref, vd_ref):
    x = x_ref[...]
    h = _rms(x, g_ref[...]).astype(jnp.bfloat16)
    u = jnp.dot(h, w1_ref[...], preferred_element_type=jnp.float32)

    lane = lax.broadcasted_iota(jnp.int32, (x.shape[0], LANES), 1)
    first_m = (lane >= 64) & (lane < 80)
    first_s = (lane % 64) < 32
    first_d = (lane % 32) < 16
    cm, sm = cm_ref[...], sm_ref[...]
    cs, ss = cs_ref[...], ss_ref[...]
    cd, sd = cd_ref[...], sd_ref[...]

    cq = _rms(u[:, SEG_CQ:SEG_CQ + MLA_Q_RANK], qn_ref[...]).astype(jnp.bfloat16)
    q = jnp.dot(cq, wuq_ref[...], preferred_element_type=jnp.float32)
    ckv = _rms(u[:, SEG_CKV:SEG_CKV + MLA_KV_RANK], kvn_ref[...]).astype(jnp.bfloat16)
    kv = jnp.dot(ckv, wukv_ref[...], preferred_element_type=jnp.float32)
    kr = _rope(u[:, SEG_KR:SEG_KR + LANES], cm, sm, first_m, 16)
    for hd in range(MLA_HEADS):
        sl = slice(hd * LANES, (hd + 1) * LANES)
        qm_ref[:, sl] = _rope(q[:, sl], cm, sm, first_m, 16).astype(jnp.bfloat16)
        km_ref[:, sl] = (kv[:, sl] + kr).astype(jnp.bfloat16)
    vm_ref[...] = kv[:, MLA_HEADS * LANES:].astype(jnp.bfloat16)

    for j in range(SWA_HEADS // 2):
        sl = slice(SEG_SQ + j * LANES, SEG_SQ + (j + 1) * LANES)
        qs_ref[:, j * LANES:(j + 1) * LANES] = _rope(u[:, sl], cs, ss, first_s, 32).astype(jnp.bfloat16)
    ks_ref[...] = _rope(u[:, SEG_SK:SEG_SK + LANES], cs, ss, first_s, 32).astype(jnp.bfloat16)
    vs_ref[...] = u[:, SEG_SV:SEG_SV + LANES].astype(jnp.bfloat16)

    for j in range(2):
        sq = slice(SEG_DQ + j * LANES, SEG_DQ + (j + 1) * LANES)
        sk = slice(SEG_DK + j * LANES, SEG_DK + (j + 1) * LANES)
        qd_ref[:, j * LANES:(j + 1) * LANES] = _rope(u[:, sq], cd, sd, first_d, 16).astype(jnp.bfloat16)
        kd_ref[:, j * LANES:(j + 1) * LANES] = _rope(u[:, sk], cd, sd, first_d, 16).astype(jnp.bfloat16)
    vd_ref[...] = u[:, SEG_DV:SEG_DV + 2 * LANES].astype(jnp.bfloat16)


def _proj_call(x, g, w1, qn, kvn, wuq, wukv, tables):
    T = x.shape[0]
    tm = PROJ_TILE
    n_seq = SEQ // tm
    tok = lambda i: (i, 0)
    const = lambda i: (0, 0)
    tab = lambda i: (i % n_seq, 0)
    widths = (6 * LANES, 6 * LANES, 3 * LANES, 3 * LANES, LANES, LANES,
              2 * LANES, 2 * LANES, 2 * LANES)
    return pl.pallas_call(
        _proj_kernel,
        grid=(T // tm,),
        in_specs=[
            pl.BlockSpec((tm, D_MODEL), tok),
            pl.BlockSpec((1, D_MODEL), const),
            pl.BlockSpec((D_MODEL, PROJ_WIDTH), const),
            pl.BlockSpec((1, MLA_Q_RANK), const),
            pl.BlockSpec((1, MLA_KV_RANK), const),
            pl.BlockSpec((MLA_Q_RANK, 6 * LANES), const),
            pl.BlockSpec((MLA_KV_RANK, 9 * LANES), const),
        ] + [pl.BlockSpec((tm, LANES), tab)] * 6,
        out_specs=[pl.BlockSpec((tm, w), tok) for w in widths],
        out_shape=[jax.ShapeDtypeStruct((T, w), jnp.bfloat16) for w in widths],
        compiler_params=pltpu.CompilerParams(
            dimension_semantics=("arbitrary",), vmem_limit_bytes=VMEM_LIMIT),
        name="proj",
    )(x, g, w1, qn, kvn, wuq, wukv, *tables)


def _fill_values(vx_ref, v_ref):
    lane = lax.broadcasted_iota(jnp.int32, v_ref.shape, 1)
    vx_ref[:, :LANES] = v_ref[...]
    vx_ref[:, LANES:] = jnp.where(lane == 0, 1.0, 0.0).astype(vx_ref.dtype)


def _softmax_pv(q, k, vx):
    s = lax.dot_general(q, k, _NT, preferred_element_type=jnp.float32)
    m = jnp.max(s, axis=-1, keepdims=True)
    p = jnp.exp2(s - m)
    o = jnp.dot(p.astype(jnp.bfloat16), vx, preferred_element_type=jnp.float32)
    return o[:, :LANES] / o[:, LANES:LANES + 1]


def _mla_kernel(q_ref, k_ref, v_ref, o_ref, vx_ref):
    _fill_values(vx_ref, v_ref)
    lane = lax.broadcasted_iota(jnp.int32, (ATTN_SUB, LANES), 1)
    for sb in range(ATTN_Q_TILE // ATTN_SUB):
        rows = slice(sb * ATTN_SUB, (sb + 1) * ATTN_SUB)
        o0 = _softmax_pv(q_ref[rows, :LANES], k_ref[:, :LANES], vx_ref[...])
        o1 = _softmax_pv(q_ref[rows, LANES:], k_ref[:, LANES:], vx_ref[...])
        o_ref[rows, :] = jnp.where(lane < MLA_V_DIM, o0, o1).astype(o_ref.dtype)


def _mla_call(q, k, v, batch):
    T = q.shape[0]
    tq = ATTN_Q_TILE
    nq = SEQ // tq
    return pl.pallas_call(
        _mla_kernel,
        grid=(batch, MLA_HEADS // 2, nq),
        in_specs=[
            pl.BlockSpec((tq, 2 * LANES), lambda b, p, i: (b * nq + i, p)),
            pl.BlockSpec((SEQ, 2 * LANES), lambda b, p, i: (b, p)),
            pl.BlockSpec((SEQ, LANES), lambda b, p, i: (b, p)),
        ],
        out_specs=pl.BlockSpec((tq, LANES), lambda b, p, i: (b * nq + i, p)),
        out_shape=jax.ShapeDtypeStruct((T, MLA_HEADS * MLA_V_DIM), jnp.bfloat16),
        scratch_shapes=[pltpu.VMEM((SEQ, 2 * LANES), jnp.bfloat16)],
        compiler_params=pltpu.CompilerParams(
            dimension_semantics=("arbitrary",) * 3, vmem_limit_bytes=VMEM_LIMIT),
        name="mla_attn",
    )(q, k, v)


def _swa_kernel(sink_ref, q_ref, k_ref, v_ref, o_ref):
    i = pl.program_id(1)
    nb = SEQ // WINDOW
    half = SWA_HEADS // 2
    rows = SWA_HEADS * WINDOW
    lane = lax.broadcasted_iota(jnp.int32, (WINDOW, LANES), 1)
    left = lane < HEAD_DIM
    row = lax.broadcasted_iota(jnp.int32, (rows, 3 * WINDOW), 0)
    col = lax.broadcasted_iota(jnp.int32, (rows, 3 * WINDOW), 1)
    rel = col - row % WINDOW
    band = (rel >= 0) & (rel <= 2 * WINDOW)
    head_col = lax.broadcasted_iota(jnp.int32, (rows, 1), 0) // WINDOW
    sink = jnp.zeros((rows, 1), jnp.float32)
    for hd in range(SWA_HEADS):
        sink = jnp.where(head_col == hd, sink_ref[hd], sink)
    for sub in range(SWA_Q_TILE // WINDOW):
        n = i * (SWA_Q_TILE // WINDOW) + sub
        prev = pl.multiple_of(jnp.maximum(n - 1, 0) * WINDOW, WINDOW)
        cur = pl.multiple_of(n * WINDOW, WINDOW)
        nxt = pl.multiple_of(jnp.minimum(n + 1, nb - 1) * WINDOW, WINDOW)
        kb = jnp.concatenate([k_ref[pl.ds(prev, WINDOW), :], k_ref[pl.ds(cur, WINDOW), :],
                              k_ref[pl.ds(nxt, WINDOW), :]], axis=0)
        vb = jnp.concatenate([v_ref[pl.ds(prev, WINDOW), :], v_ref[pl.ds(cur, WINDOW), :],
                              v_ref[pl.ds(nxt, WINDOW), :]], axis=0)
        lo_col = jnp.where(n > 0, 0, WINDOW)
        hi_col = jnp.where(n < nb - 1, 3 * WINDOW, 2 * WINDOW)
        valid = band & (col >= lo_col) & (col < hi_col)
        parts = [q_ref[sub * WINDOW:(sub + 1) * WINDOW, j * LANES:(j + 1) * LANES] for j in range(half)]
        zero = jnp.zeros_like(parts[0])
        qs = jnp.concatenate([jnp.where(left, p_, zero) for p_ in parts]
                             + [jnp.where(left, zero, p_) for p_ in parts], axis=0)
        s = lax.dot_general(qs, kb, _NT, preferred_element_type=jnp.float32)
        s = jnp.where(valid, s, -jnp.inf)
        m = jnp.maximum(jnp.max(s, axis=-1, keepdims=True), sink)
        p = jnp.exp2(s - m)
        l = jnp.sum(p, axis=-1, keepdims=True) + jnp.exp2(sink - m)
        o = jnp.dot(p.astype(jnp.bfloat16), vb, preferred_element_type=jnp.float32) / l
        for j in range(half):
            o_ref[sub * WINDOW:(sub + 1) * WINDOW, j * LANES:(j + 1) * LANES] = jnp.where(
                left, o[j * WINDOW:(j + 1) * WINDOW], o[(j + half) * WINDOW:(j + half + 1) * WINDOW]
            ).astype(o_ref.dtype)


def _swa_call(sink, q, k, v, batch):
    T = q.shape[0]
    tq = SWA_Q_TILE
    nq = SEQ // tq
    return pl.pallas_call(
        _swa_kernel,
        grid=(batch, nq),
        in_specs=[
            pl.BlockSpec(memory_space=pltpu.SMEM),
            pl.BlockSpec((tq, 3 * LANES), lambda b, i: (b * nq + i, 0)),
            pl.BlockSpec((SEQ, LANES), lambda b, i: (b, 0)),
            pl.BlockSpec((SEQ, LANES), lambda b, i: (b, 0)),
        ],
        out_specs=pl.BlockSpec((tq, 3 * LANES), lambda b, i: (b * nq + i, 0)),
        out_shape=jax.ShapeDtypeStruct((T, SWA_HEADS * HEAD_DIM), jnp.bfloat16),
        compiler_params=pltpu.CompilerParams(
            dimension_semantics=("arbitrary",) * 2, vmem_limit_bytes=VMEM_LIMIT),
        name="swa_attn",
    )(sink, q, k, v)


def _diff_kernel(lam_init, lq1_ref, lk1_ref, lq2_ref, lk2_ref, subln_ref,
                 q_ref, k_ref, v_ref, o_ref, vx_ref):
    lam = (jnp.exp(jnp.sum(lq1_ref[...] * lk1_ref[...], axis=-1, keepdims=True))
           - jnp.exp(jnp.sum(lq2_ref[...] * lk2_ref[...], axis=-1, keepdims=True))
           + lam_init)
    _fill_values(vx_ref, v_ref)
    lane = lax.broadcasted_iota(jnp.int32, (ATTN_SUB, LANES), 1)
    left = lane < DIFF_V_DIM
    for sb in range(DIFF_Q_TILE // ATTN_SUB):
        rows = slice(sb * ATTN_SUB, (sb + 1) * ATTN_SUB)
        q = q_ref[rows, :]
        zero = jnp.zeros_like(q)
        outs = []
        for side in range(2):
            comp = []
            for c in range(2):
                lo = side * DIFF_V_DIM + c * DIFF_QK_DIM
                qc = jnp.where((lane >= lo) & (lane < lo + DIFF_QK_DIM), q, zero)
                comp.append(_softmax_pv(qc, k_ref[...], vx_ref[...]))
            outs.append(comp[0] - lam * comp[1])
        o = jnp.where(left, outs[0], outs[1])
        sq = o * o
        ms_l = jnp.sum(jnp.where(left, sq, 0.0), axis=-1, keepdims=True) * (1.0 / DIFF_V_DIM)
        ms_r = jnp.sum(jnp.where(left, 0.0, sq), axis=-1, keepdims=True) * (1.0 / DIFF_V_DIM)
        r = jnp.where(left, lax.rsqrt(ms_l + NORM_EPS), lax.rsqrt(ms_r + NORM_EPS))
        o_ref[rows, :] = (o * r * subln_ref[...] * (1.0 - lam_init)).astype(o_ref.dtype)


def _diff_call(lam_init, lq1, lk1, lq2, lk2, subln2, q, k, v, batch):
    T = q.shape[0]
    tq = DIFF_Q_TILE
    nq = SEQ // tq
    small = lambda b, p, i: (0, 0)
    return pl.pallas_call(
        functools.partial(_diff_kernel, lam_init),
        grid=(batch, DIFF_HEADS // 2, nq),
        in_specs=[pl.BlockSpec((1, DIFF_QK_DIM), small)] * 4 + [
            pl.BlockSpec((1, LANES), small),
            pl.BlockSpec((tq, LANES), lambda b, p, i: (b * nq + i, p)),
            pl.BlockSpec((SEQ, LANES), lambda b, p, i: (b, p)),
            pl.BlockSpec((SEQ, LANES), lambda b, p, i: (b, p)),
        ],
        out_specs=pl.BlockSpec((tq, LANES), lambda b, p, i: (b * nq + i, p)),
        out_shape=jax.ShapeDtypeStruct((T, DIFF_HEADS * DIFF_V_DIM), jnp.bfloat16),
        scratch_shapes=[pltpu.VMEM((SEQ, 2 * LANES), jnp.bfloat16)],
        compiler_params=pltpu.CompilerParams(
            dimension_semantics=("arbitrary",) * 3, vmem_limit_bytes=VMEM_LIMIT),
        name="diff_attn",
    )(lq1, lk1, lq2, lk2, subln2, q, k, v)


def _outproj_kernel(x_ref, om_ref, os_ref, od_ref, wm_ref, ws_ref, wd_ref, g_ref,
                    wr2_ref, br_ref, x2_ref, h2_ref, lt_ref):
    x2 = (x_ref[...]
          + jnp.dot(om_ref[...], wm_ref[...], preferred_element_type=jnp.float32)
          + jnp.dot(os_ref[...], ws_ref[...], preferred_element_type=jnp.float32)
          + jnp.dot(od_ref[...], wd_ref[...], preferred_element_type=jnp.float32))
    x2_ref[...] = x2
    h2 = _rms(x2, g_ref[...])
    _store_rows(h2_ref, h2)

    hi = h2.astype(jnp.bfloat16)
    lo = (h2 - hi.astype(jnp.float32)).astype(jnp.bfloat16)
    a = jnp.dot(hi, wr2_ref[...], preferred_element_type=jnp.float32)
    b = jnp.dot(lo, wr2_ref[:, :LANES], preferred_element_type=jnp.float32)
    logits = a[:, :LANES] + a[:, LANES:] + b + br_ref[...]
    lt_ref[...] = logits.T


def _outproj_call(x, om, os_, od, wm, ws, wd, g, wr2, br):
    T = x.shape[0]
    tm = PROJ_TILE
    tok = lambda i: (i, 0)
    const = lambda i: (0, 0)
    return pl.pallas_call(
        _outproj_kernel,
        grid=(T // tm,),
        in_specs=[
            pl.BlockSpec((tm, D_MODEL), tok),
            pl.BlockSpec((tm, 3 * LANES), tok),
            pl.BlockSpec((tm, 3 * LANES), tok),
            pl.BlockSpec((tm, 2 * LANES), tok),
            pl.BlockSpec((3 * LANES, D_MODEL), const),
            pl.BlockSpec((3 * LANES, D_MODEL), const),
            pl.BlockSpec((2 * LANES, D_MODEL), const),
            pl.BlockSpec((1, D_MODEL), const),
            pl.BlockSpec((D_MODEL, 2 * LANES), const),
            pl.BlockSpec((1, LANES), const),
        ],
        out_specs=[
            pl.BlockSpec((tm, D_MODEL), tok),
            pl.BlockSpec((tm * ROW_TILES, LANES), tok),
            pl.BlockSpec((LANES, tm), lambda i: (0, i)),
        ],
        out_shape=[
            jax.ShapeDtypeStruct((T, D_MODEL), jnp.float32),
            jax.ShapeDtypeStruct((T * ROW_TILES, LANES), jnp.float32),
            jax.ShapeDtypeStruct((LANES, T), jnp.float32),
        ],
        compiler_params=pltpu.CompilerParams(
            dimension_semantics=("arbitrary",), vmem_limit_bytes=VMEM_LIMIT),
        name="outproj",
    )(x, om, os_, od, wm, ws, wd, g, wr2, br)


def _route_kernel(lt_ref, ri_ref, rf_ref, cnt_ref, carry_ref):
    step = pl.program_id(0)

    @pl.when(step == 0)
    def _():
        carry_ref[...] = jnp.zeros_like(carry_ref)

    tc = lt_ref.shape[1]
    neg = -jnp.inf
    row8 = lax.broadcasted_iota(jnp.int32, (8, tc), 0)
    is_g = row8 < N_GROUPS
    gl = jnp.where(is_g, lt_ref[0:8, :], neg)
    gmax = jnp.max(gl, axis=0, keepdims=True)
    gidx = jnp.min(jnp.where(gl == gmax, row8, 8), axis=0, keepdims=True)
    gsum = jnp.sum(jnp.where(is_g, jnp.exp(gl - gmax), 0.0), axis=0, keepdims=True)
    g_gate = 1.0 / gsum
    gidx8 = jnp.broadcast_to(gidx, (8, tc))
    e_in = lt_ref[EXPERT_LANE0:EXPERT_LANE0 + 8, :]
    for g in range(1, N_GROUPS):
        lo = EXPERT_LANE0 + g * EXPERTS_PER_GROUP
        e_in = jnp.where(gidx8 == g, lt_ref[lo:lo + EXPERTS_PER_GROUP, :], e_in)
    m1 = jnp.max(e_in, axis=0, keepdims=True)
    i1 = jnp.min(jnp.where(e_in == m1, row8, 8), axis=0, keepdims=True)
    e_rest = jnp.where(row8 == i1, neg, e_in)
    m2 = jnp.max(e_rest, axis=0, keepdims=True)
    i2 = jnp.min(jnp.where(e_rest == m2, row8, 8), axis=0, keepdims=True)
    t = jnp.exp(m2 - m1)
    w1 = g_gate / (1.0 + t)
    w2 = g_gate * t / (1.0 + t)
    e1 = gidx * EXPERTS_PER_GROUP + i1
    e2 = gidx * EXPERTS_PER_GROUP + i2

    rowe = lax.broadcasted_iota(jnp.int32, (N_EXPERTS, tc), 0)
    hit1 = rowe == e1
    hit2 = rowe == e2
    onehot = jnp.where(hit1 | hit2, 1.0, 0.0)
    r = lax.broadcasted_iota(jnp.int32, (RANK_CHUNK, RANK_CHUNK), 0)
    c = lax.broadcasted_iota(jnp.int32, (RANK_CHUNK, RANK_CHUNK), 1)
    upper = jnp.where(r < c, 1.0, 0.0).astype(jnp.bfloat16)
    carry = carry_ref[:, 0:1]
    parts = []
    for j in range(tc // RANK_CHUNK):
        oh = onehot[:, j * RANK_CHUNK:(j + 1) * RANK_CHUNK]
        parts.append(jnp.dot(oh.astype(jnp.bfloat16), upper, preferred_element_type=jnp.float32) + carry)
        carry = carry + jnp.sum(oh, axis=1, keepdims=True)
    before = jnp.concatenate(parts, axis=1)
    rank1 = jnp.sum(jnp.where(hit1, before, 0.0), axis=0, keepdims=True).astype(jnp.int32)
    rank2 = jnp.sum(jnp.where(hit2, before, 0.0), axis=0, keepdims=True).astype(jnp.int32)
    carry_ref[...] = jnp.broadcast_to(carry, carry_ref.shape)
    cnt_ref[...] = jnp.broadcast_to(carry, cnt_ref.shape)

    ri_ref[...] = jnp.where(row8 == 0, e1, jnp.where(row8 == 1, e2, jnp.where(
        row8 == 2, rank1, jnp.where(row8 == 3, rank2, 0))))
    rf_ref[...] = jnp.where(row8 == 0, w1, jnp.where(row8 == 1, w2, 0.0))


def _route_call(lt):
    T = lt.shape[1]
    tc = ROUTE_CHUNK
    return pl.pallas_call(
        _route_kernel,
        grid=(T // tc,),
        in_specs=[pl.BlockSpec((LANES, tc), lambda i: (0, i))],
        out_specs=[
            pl.BlockSpec((8, tc), lambda i: (0, i)),
            pl.BlockSpec((8, tc), lambda i: (0, i)),
            pl.BlockSpec((N_EXPERTS, LANES), lambda i: (0, 0)),
        ],
        out_shape=[
            jax.ShapeDtypeStruct((8, T), jnp.int32),
            jax.ShapeDtypeStruct((8, T), jnp.float32),
            jax.ShapeDtypeStruct((N_EXPERTS, LANES), jnp.float32),
        ],
        scratch_shapes=[pltpu.VMEM((N_EXPERTS, LANES), jnp.float32)],
        compiler_params=pltpu.CompilerParams(
            dimension_semantics=("arbitrary",), vmem_limit_bytes=VMEM_LIMIT),
        name="route",
    )(lt)


def _plan_kernel(ri_ref, cnt_ref, pos_ref, te_ref, meta_ref):
    T = ri_ref.shape[1]
    counts = cnt_ref[...].astype(jnp.int32)
    padded = (counts + (MOE_TILE - 1)) // MOE_TILE * MOE_TILE
    r = lax.broadcasted_iota(jnp.int32, (N_EXPERTS, N_EXPERTS), 0)
    c = lax.broadcasted_iota(jnp.int32, (N_EXPERTS, N_EXPERTS), 1)
    lower = jnp.where(c < r, 1.0, 0.0)
    off = jnp.dot(lower, padded.astype(jnp.float32), preferred_element_type=jnp.float32,
                  precision=lax.Precision.HIGHEST).astype(jnp.int32)
    ends = off + padded
    total = jnp.max(ends, axis=0, keepdims=True)

    rowe = lax.broadcasted_iota(jnp.int32, (N_EXPERTS, T), 0)
    off_col = off[:, 0:1]
    row8 = lax.broadcasted_iota(jnp.int32, (8, T), 0)
    pos1 = jnp.sum(jnp.where(rowe == ri_ref[0:1, :], off_col, 0), axis=0, keepdims=True) + ri_ref[2:3, :]
    pos2 = jnp.sum(jnp.where(rowe == ri_ref[1:2, :], off_col, 0), axis=0, keepdims=True) + ri_ref[3:4, :]
    pos_ref[...] = jnp.where(row8 == 0, pos1, jnp.where(row8 == 1, pos2, 0))

    n_lanes = te_ref.shape[1]
    start = lax.broadcasted_iota(jnp.int32, (N_EXPERTS, n_lanes), 1) * MOE_TILE
    start = jnp.minimum(start, total[:, 0:1] - MOE_TILE)
    te = jnp.sum((ends[:, 0:1] <= start).astype(jnp.int32), axis=0, keepdims=True)
    te_ref[...] = jnp.broadcast_to(te, te_ref.shape)
    lane = lax.broadcasted_iota(jnp.int32, (N_EXPERTS, LANES), 1)
    fill = jnp.where(padded > 0, ends - MOE_TILE, -1)
    meta_ref[...] = jnp.where(lane == 0, fill, total // MOE_TILE)


def _plan_call(ri, cnt, max_tiles):
    T = ri.shape[1]
    n_lanes = -(-max_tiles // LANES) * LANES
    full = lambda i: (0, 0)
    return pl.pallas_call(
        _plan_kernel,
        grid=(1,),
        in_specs=[pl.BlockSpec((8, T), full), pl.BlockSpec((N_EXPERTS, LANES), full)],
        out_specs=[pl.BlockSpec((8, T), full), pl.BlockSpec((8, n_lanes), full),
                   pl.BlockSpec((N_EXPERTS, LANES), full)],
        out_shape=[jax.ShapeDtypeStruct((8, T), jnp.int32),
                   jax.ShapeDtypeStruct((8, n_lanes), jnp.int32),
                   jax.ShapeDtypeStruct((N_EXPERTS, LANES), jnp.int32)],
        compiler_params=pltpu.CompilerParams(
            dimension_semantics=("arbitrary",), vmem_limit_bytes=VMEM_LIMIT),
        name="slot_plan",
    )(ri, cnt)


def _row(ref, idx):
    return ref.at[pl.ds(pl.multiple_of(idx * ROW_TILES, ROW_TILES), ROW_TILES), :]


def _push_kernel(pos_ref, fill_ref, nt_ref, h_ref, xs_ref, zero_ref, sem):
    step = pl.program_id(0)
    tm = h_ref.shape[0] // ROW_TILES
    n_tok = pl.num_programs(0) * tm
    fill_rows = MOE_TILE * ROW_TILES
    max_tiles = xs_ref.shape[0] // fill_rows

    def fill_copy(slot):
        start = pl.multiple_of(slot * ROW_TILES, fill_rows)
        return pltpu.make_async_copy(zero_ref, xs_ref.at[pl.ds(start, fill_rows), :], sem)

    @pl.when(step == 0)
    def _():
        zero_ref[...] = jnp.zeros_like(zero_ref)
        for e in range(N_EXPERTS):
            @pl.when(fill_ref[e] >= 0)
            def _():
                fill_copy(fill_ref[e]).start()

        def tail_start(j, carry):
            fill_copy(j * MOE_TILE).start()
            return carry

        def tail_wait(j, carry):
            fill_copy(j * MOE_TILE).wait()
            return carry

        lax.fori_loop(nt_ref[0], max_tiles, tail_start, 0)
        for e in range(N_EXPERTS):
            @pl.when(fill_ref[e] >= 0)
            def _():
                fill_copy(fill_ref[e]).wait()
        lax.fori_loop(nt_ref[0], max_tiles, tail_wait, 0)

    def issue(r, carry):
        t = step * tm + r
        for k in range(2):
            pltpu.make_async_copy(_row(h_ref, r), _row(xs_ref, pos_ref[k * n_tok + t]), sem).start(priority=k)
        return carry

    lax.fori_loop(0, tm, issue, 0, unroll=8)

    def drain(r, carry):
        for k in range(2):
            pltpu.make_async_copy(_row(h_ref, 0), _row(xs_ref, 0), sem).wait()
        return carry

    lax.fori_loop(0, tm, drain, 0, unroll=8)


def _push_call(pos, fill, n_tiles, h2, n_slots):
    tm = MOVE_TILE
    n_tok = h2.shape[0] // ROW_TILES
    return pl.pallas_call(
        _push_kernel,
        grid_spec=pltpu.PrefetchScalarGridSpec(
            num_scalar_prefetch=3,
            grid=(n_tok // tm,),
            in_specs=[pl.BlockSpec((tm * ROW_TILES, LANES), lambda i, pos, fill, nt: (i, 0))],
            out_specs=pl.BlockSpec(memory_space=pl.ANY),
            scratch_shapes=[pltpu.VMEM((MOE_TILE * ROW_TILES, LANES), jnp.float32),
                            pltpu.SemaphoreType.DMA(())],
        ),
        out_shape=jax.ShapeDtypeStruct((n_slots * ROW_TILES, LANES), jnp.float32),
        compiler_params=pltpu.CompilerParams(
            dimension_semantics=("arbitrary",), vmem_limit_bytes=VMEM_LIMIT),
        name="moe_push",
    )(pos, fill, n_tiles, h2)


def _expert_kernel(te_ref, nt_ref, xs_ref, wg_ref, wu_ref, wd_ref, ys_ref,
                   wg_s, wu_s, wd_s):
    j = pl.program_id(0)

    @pl.when(j < nt_ref[0])
    def _():
        changed = (j == 0) | (te_ref[j] != te_ref[jnp.maximum(j - 1, 0)])

        @pl.when(changed)
        def _():
            wg_s[...] = wg_ref[...].astype(jnp.bfloat16)
            wu_s[...] = wu_ref[...].astype(jnp.bfloat16)
            wd_s[...] = wd_ref[...].astype(jnp.bfloat16)

        xb = _load_rows(xs_ref, MOE_TILE).astype(jnp.bfloat16)
        a = jnp.dot(xb, wg_s[...], preferred_element_type=jnp.float32)
        u = jnp.dot(xb, wu_s[...], preferred_element_type=jnp.float32)
        hid = (a * jax.nn.sigmoid(a) * u).astype(jnp.bfloat16)
        _store_rows(ys_ref, jnp.dot(hid, wd_s[...], preferred_element_type=jnp.float32))


def _expert_call(tile_expert, n_tiles, xs, wg, wu, wd, layer):
    n_slots = xs.shape[0] // ROW_TILES
    max_tiles = n_slots // MOE_TILE
    base = layer * N_EXPERTS
    row = lambda j, te, nt: (jnp.minimum(j, nt[0] - 1), 0)
    wsel = lambda j, te, nt: (base + te[j], 0, 0)
    return pl.pallas_call(
        _expert_kernel,
        grid_spec=pltpu.PrefetchScalarGridSpec(
            num_scalar_prefetch=2,
            grid=(max_tiles,),
            in_specs=[
                pl.BlockSpec((MOE_TILE * ROW_TILES, LANES), row),
                pl.BlockSpec((None, D_MODEL, EXPERT_FF), wsel),
                pl.BlockSpec((None, D_MODEL, EXPERT_FF), wsel),
                pl.BlockSpec((None, EXPERT_FF, D_MODEL), wsel),
            ],
            out_specs=pl.BlockSpec((MOE_TILE * ROW_TILES, LANES), row),
            scratch_shapes=[pltpu.VMEM((D_MODEL, EXPERT_FF), jnp.bfloat16),
                            pltpu.VMEM((D_MODEL, EXPERT_FF), jnp.bfloat16),
                            pltpu.VMEM((EXPERT_FF, D_MODEL), jnp.bfloat16)],
        ),
        out_shape=jax.ShapeDtypeStruct((n_slots * ROW_TILES, LANES), jnp.float32),
        input_output_aliases={2: 0},
        compiler_params=pltpu.CompilerParams(
            dimension_semantics=("arbitrary",), vmem_limit_bytes=VMEM_LIMIT),
        name="moe_experts",
    )(tile_expert, n_tiles, xs, wg, wu, wd)


def _combine_kernel(final, pos_ref, x2_ref, rf_ref, g_ref, ys_ref, o_ref, g0, g1, sem):
    step = pl.program_id(0)
    tm = x2_ref.shape[0]
    n_tok = pl.num_programs(0) * tm

    def issue(r, carry):
        t = step * tm + r
        pltpu.make_async_copy(_row(ys_ref, pos_ref[t]), _row(g0, r), sem).start(priority=0)
        pltpu.make_async_copy(_row(ys_ref, pos_ref[n_tok + t]), _row(g1, r), sem).start(priority=1)
        return carry

    lax.fori_loop(0, tm, issue, 0, unroll=8)

    def drain(r, carry):
        for k in range(2):
            pltpu.make_async_copy(_row(ys_ref, 0), _row(g0, 0), sem).wait()
        return carry

    lax.fori_loop(0, tm, drain, 0, unroll=8)

    rf = rf_ref[...]
    y = x2_ref[...] + rf[:, 0:1] * _load_rows(g0, tm) + rf[:, 1:2] * _load_rows(g1, tm)
    if final:
        y = _rms(y, g_ref[...])
    o_ref[...] = y


def _combine_call(pos, x2, rf, g, ys, final):
    T = x2.shape[0]
    tm = MOVE_TILE
    return pl.pallas_call(
        functools.partial(_combine_kernel, final),
        grid_spec=pltpu.PrefetchScalarGridSpec(
            num_scalar_prefetch=1,
            grid=(T // tm,),
            in_specs=[
                pl.BlockSpec((tm, D_MODEL), lambda i, pos: (i, 0)),
                pl.BlockSpec((tm, 2), lambda i, pos: (i, 0)),
                pl.BlockSpec((1, D_MODEL), lambda i, pos: (0, 0)),
                pl.BlockSpec(memory_space=pl.ANY),
            ],
            out_specs=pl.BlockSpec((tm, D_MODEL), lambda i, pos: (i, 0)),
            scratch_shapes=[pltpu.VMEM((tm * ROW_TILES, LANES), jnp.float32),
                            pltpu.VMEM((tm * ROW_TILES, LANES), jnp.float32),
                            pltpu.SemaphoreType.DMA(())],
        ),
        out_shape=jax.ShapeDtypeStruct((T, D_MODEL), jnp.float32),
        compiler_params=pltpu.CompilerParams(
            dimension_semantics=("arbitrary",), vmem_limit_bytes=VMEM_LIMIT),
        name="moe_combine",
    )(pos, x2, rf, g, ys)


def _rope_tables():
    def cos_sin(dim):
        inv = 1.0 / (ROPE_THETA ** (jnp.arange(0, dim, 2, dtype=jnp.float32) / dim))
        ang = jnp.arange(SEQ, dtype=jnp.float32)[:, None] * inv[None, :]
        return jnp.cos(ang), jnp.sin(ang)

    c16, s16 = cos_sin(MLA_ROPE_DIM)
    c32, s32 = cos_sin(HEAD_DIM)
    ones = jnp.ones((SEQ, 64), jnp.float32)
    zeros = jnp.zeros((SEQ, 64), jnp.float32)
    pad = jnp.zeros((SEQ, 32), jnp.float32)
    cm = jnp.concatenate([ones, c16, c16, pad], axis=1)
    sm = jnp.concatenate([zeros, -s16, s16, pad], axis=1)
    cs = jnp.tile(jnp.concatenate([c32, c32], axis=1), (1, 2))
    ss = jnp.tile(jnp.concatenate([-s32, s32], axis=1), (1, 2))
    cd = jnp.tile(jnp.concatenate([c16, c16], axis=1), (1, 4))
    sd = jnp.tile(jnp.concatenate([-s16, s16], axis=1), (1, 4))
    return cm, sm, cs, ss, cd, sd


def _layer_weights(w_in, w_uq, w_ukv, w_out):
    f32 = jnp.float32
    cq = w_in[:, 0:384]
    ckv = w_in[:, 384:640]
    kr = w_in[:, 640:672]
    sq = w_in[:, 672:1056] * (LOG2E * HEAD_DIM ** -0.5)
    sk = w_in[:, 1056:1184]
    sv = w_in[:, 1184:1312]
    dq = w_in[:, 1312:1568] * (LOG2E * DIFF_QK_DIM ** -0.5)
    dk = w_in[:, 1568:1824]
    dv = w_in[:, 1824:2080]
    kr_pad = jnp.concatenate([jnp.zeros((D_MODEL, 64), f32), kr, jnp.zeros((D_MODEL, 32), f32)], axis=1)
    half = SWA_HEADS // 2
    sq_h = sq.reshape(D_MODEL, SWA_HEADS, HEAD_DIM)
    sq_perm = jnp.stack([sq_h[:, :half], sq_h[:, half:]], axis=2).reshape(D_MODEL, SWA_HEADS * HEAD_DIM)
    w1 = jnp.concatenate([cq, ckv, kr_pad, sq_perm, sk, sv, dq, dk, dv], axis=1).astype(jnp.bfloat16)

    scale = LOG2E * (MLA_NOPE_DIM + MLA_ROPE_DIM) ** -0.5
    uq = (w_uq * scale).reshape(MLA_Q_RANK, MLA_HEADS, MLA_NOPE_DIM + MLA_ROPE_DIM)
    uq = jnp.concatenate([uq, jnp.zeros((MLA_Q_RANK, MLA_HEADS, 32), f32)], axis=2)
    wuq = uq.reshape(MLA_Q_RANK, MLA_HEADS * LANES).astype(jnp.bfloat16)
    ukv = w_ukv.reshape(MLA_KV_RANK, MLA_HEADS, MLA_NOPE_DIM + MLA_V_DIM)
    uk = jnp.concatenate([ukv[:, :, :MLA_NOPE_DIM], jnp.zeros((MLA_KV_RANK, MLA_HEADS, 64), f32)], axis=2)
    uv = ukv[:, :, MLA_NOPE_DIM:]
    wukv = jnp.concatenate([uk.reshape(MLA_KV_RANK, MLA_HEADS * LANES),
                            uv.reshape(MLA_KV_RANK, MLA_HEADS * MLA_V_DIM)], axis=1).astype(jnp.bfloat16)

    wm = w_out[0:384].astype(jnp.bfloat16)
    wo_s = w_out[384:768].reshape(SWA_HEADS, HEAD_DIM, D_MODEL)
    ws = jnp.stack([wo_s[:half], wo_s[half:]], axis=1).reshape(SWA_HEADS * HEAD_DIM, D_MODEL).astype(jnp.bfloat16)
    wd = w_out[768:1024].astype(jnp.bfloat16)
    return w1, wuq, wukv, wm, ws, wd


def _router_weights(w_rg, b_rg, w_re, b_re):
    f32 = jnp.float32
    wr = jnp.concatenate([w_rg, jnp.zeros((D_MODEL, EXPERT_LANE0 - N_GROUPS), f32), w_re,
                          jnp.zeros((D_MODEL, LANES - EXPERT_LANE0 - N_EXPERTS), f32)], axis=1)
    br = jnp.concatenate([b_rg, jnp.zeros((EXPERT_LANE0 - N_GROUPS,), f32), b_re,
                          jnp.zeros((LANES - EXPERT_LANE0 - N_EXPERTS,), f32)])[None, :]
    hi = wr.astype(jnp.bfloat16)
    lo = (wr - hi.astype(f32)).astype(jnp.bfloat16)
    return jnp.concatenate([hi, lo], axis=1), br


def kernel(x, attn_norm, w_in, mla_q_norm, mla_kv_norm, mla_w_uq, mla_w_ukv, swa_sink, diff_lq1, diff_lk1, diff_lq2, diff_lk2, diff_subln, w_out, ffn_norm, router_group, router_group_bias, router_expert, router_expert_bias, w_gate, w_up, w_down, final_norm):
    batch, seq, d = x.shape
    assert (seq, d) == (SEQ, D_MODEL)
    depth = w_in.shape[0]
    n_tok = batch * seq
    tables = _rope_tables()
    wg = w_gate.reshape(depth * N_EXPERTS, D_MODEL, EXPERT_FF)
    wu = w_up.reshape(depth * N_EXPERTS, D_MODEL, EXPERT_FF)
    wdn = w_down.reshape(depth * N_EXPERTS, EXPERT_FF, D_MODEL)
    xt = x.reshape(n_tok, D_MODEL)
    for l in range(depth):
        w1, wuq, wukv, wm, ws, wd = _layer_weights(w_in[l], mla_w_uq[l], mla_w_ukv[l], w_out[l])
        qm, km, vm, qs, ks, vs, qd, kd, vd = _proj_call(
            xt, attn_norm[l][None], w1, mla_q_norm[l][None], mla_kv_norm[l][None], wuq, wukv, tables)
        o_mla = _mla_call(qm, km, vm, batch)
        o_swa = _swa_call(swa_sink[l] * LOG2E, qs, ks, vs, batch)
        lam_init = 0.8 - 0.6 * math.exp(-0.3 * l)
        subln2 = jnp.concatenate([diff_subln[l], diff_subln[l]])[None]
        o_diff = _diff_call(lam_init, diff_lq1[l][None], diff_lk1[l][None], diff_lq2[l][None],
                            diff_lk2[l][None], subln2, qd, kd, vd, batch)
        wr2, br = _router_weights(router_group[l], router_group_bias[l],
                                  router_expert[l], router_expert_bias[l])
        x2, h2, lt = _outproj_call(xt, o_mla, o_swa, o_diff, wm, ws, wd, ffn_norm[l][None], wr2, br)
        ri, rf, cnt = _route_call(lt)
        max_tiles = (2 * n_tok) // MOE_TILE + N_EXPERTS
        pos8, te8, meta = _plan_call(ri, cnt, max_tiles)
        pos = pos8[0:2].reshape(2 * n_tok)
        fill, n_tiles = meta[:, 0], meta[0, 1:2]
        xs = _push_call(pos, fill, n_tiles, h2, max_tiles * MOE_TILE)
        ys = _expert_call(te8[0, :max_tiles], n_tiles, xs, wg, wu, wdn, l)
        xt = _combine_call(pos, x2, rf[0:2].T, final_norm[None], ys, l == depth - 1)
    return xt.reshape(batch, seq, d)
```

```python
import functools
import math

import jax
import jax.numpy as jnp
from jax import lax
from jax.experimental import pallas as pl
from jax.experimental.pallas import tpu as pltpu

D_MODEL = 1024
SEQ = 2048
HEAD_DIM = 64
ROPE_THETA = 10000.0
NORM_EPS = 1e-6
MLA_HEADS = 6
MLA_Q_RANK = 384
MLA_KV_RANK = 256
MLA_NOPE_DIM = 64
MLA_ROPE_DIM = 32
MLA_V_DIM = 64
SWA_HEADS = 6
SWA_KV_HEADS = 2
WINDOW = 128
DIFF_HEADS = 4
DIFF_QK_DIM = 32
DIFF_V_DIM = 64
N_GROUPS = 4
EXPERTS_PER_GROUP = 8
N_EXPERTS = N_GROUPS * EXPERTS_PER_GROUP
EXPERT_FF = 256

LANES = 128
ROW_TILES = D_MODEL // LANES
EXPERT_LANE0 = 8
PROJ_TILE = 512
ATTN_Q_TILE = 1024
DIFF_Q_TILE = 512
ATTN_SUB = 256
SWA_Q_TILE = 256
MOE_TILE = 256
MOVE_TILE = 256
ROUTE_CHUNK = 2048
RANK_CHUNK = 512
VMEM_LIMIT = 56 * 1024 * 1024

SEG_CQ = 0
SEG_CKV = SEG_CQ + MLA_Q_RANK
SEG_KR = SEG_CKV + MLA_KV_RANK
SEG_SQ = SEG_KR + LANES
SEG_SK = SEG_SQ + SWA_HEADS * HEAD_DIM
SEG_SV = SEG_SK + LANES
SEG_DQ = SEG_SV + LANES
SEG_DK = SEG_DQ + 2 * LANES
SEG_DV = SEG_DK + 2 * LANES
PROJ_WIDTH = SEG_DV + 2 * LANES

_NT = (((1,), (1,)), ((), ()))
LOG2E = math.log2(math.e)


def _rms(x, g):
    return x * lax.rsqrt(jnp.mean(x * x, axis=-1, keepdims=True) + NORM_EPS) * g


def _load_rows(ref, n):
    return jnp.concatenate([ref[pl.ds(c, n, stride=ROW_TILES), :] for c in range(ROW_TILES)], axis=1)


def _store_rows(ref, val):
    n = val.shape[0]
    for c in range(ROW_TILES):
        ref[pl.ds(c, n, stride=ROW_TILES), :] = val[:, c * LANES:(c + 1) * LANES]


def _rope(x, cos, sin, first_half, shift):
    up = pltpu.roll(x, LANES - shift, 1)
    dn = pltpu.roll(x, shift, 1)
    return x * cos + jnp.where(first_half, up, dn) * sin


def _proj_kernel(x_ref, g_ref, w1_ref, qn_ref, kvn_ref, wuq_ref, wukv_ref,
                 cm_ref, sm_ref, cs_ref, ss_ref, cd_ref, sd_ref,
                 qm_ref, km_ref, vm_ref, qs_ref, ks_ref, vs_ref,
                 qd_ref, kd_ref, vd_ref):
    x = x_ref[...]
    h = _rms(x, g_ref[...]).astype(jnp.bfloat16)
    u = jnp.dot(h, w1_ref[...], preferred_element_type=jnp.float32)

    lane = lax.broadcasted_iota(jnp.int32, (x.shape[0], LANES), 1)
    first_m = (lane >= 64) & (lane < 80)
    first_s = (lane % 64) < 32
    first_d = (lane % 32) < 16
    cm, sm = cm_ref[...], sm_ref[...]
    cs, ss = cs_ref[...], ss_ref[...]
    cd, sd = cd_ref[...], sd_ref[...]

    cq = _rms(u[:, SEG_CQ:SEG_CQ + MLA_Q_RANK], qn_ref[...]).astype(jnp.bfloat16)
    q = jnp.dot(cq, wuq_ref[...], preferred_element_type=jnp.float32)
    ckv = _rms(u[:, SEG_CKV:SEG_CKV + MLA_KV_RANK], kvn_ref[...]).astype(jnp.bfloat16)
    kv = jnp.dot(ckv, wukv_ref[...], preferred_element_type=jnp.float32)
    kr = _rope(u[:, SEG_KR:SEG_KR + LANES], cm, sm, first_m, 16)
    for hd in range(MLA_HEADS):
        sl = slice(hd * LANES, (hd + 1) * LANES)
        qm_ref[:, sl] = _rope(q[:, sl], cm, sm, first_m, 16).astype(jnp.bfloat16)
        km_ref[:, sl] = (kv[:, sl] + kr).astype(jnp.bfloat16)
    vm_ref[...] = kv[:, MLA_HEADS * LANES:].astype(jnp.bfloat16)

    for j in range(SWA_HEADS // 2):
        sl = slice(SEG_SQ + j * LANES, SEG_SQ + (j + 1) * LANES)
        qs_ref[:, j * LANES:(j + 1) * LANES] = _rope(u[:, sl], cs, ss, first_s, 32).astype(jnp.bfloat16)
    ks_ref[...] = _rope(u[:, SEG_SK:SEG_SK + LANES], cs, ss, first_s, 32).astype(jnp.bfloat16)
    vs_ref[...] = u[:, SEG_SV:SEG_SV + LANES].astype(jnp.bfloat16)

    for j in range(2):
        sq = slice(SEG_DQ + j * LANES, SEG_DQ + (j + 1) * LANES)
        sk = slice(SEG_DK + j * LANES, SEG_DK + (j + 1) * LANES)
        qd_ref[:, j * LANES:(j + 1) * LANES] = _rope(u[:, sq], cd, sd, first_d, 16).astype(jnp.bfloat16)
        kd_ref[:, j * LANES:(j + 1) * LANES] = _rope(u[:, sk], cd, sd, first_d, 16).astype(jnp.bfloat16)
    vd_ref[...] = u[:, SEG_DV:SEG_DV + 2 * LANES].astype(jnp.bfloat16)


def _proj_call(x, g, w1, qn, kvn, wuq, wukv, tables):
    T = x.shape[0]
    tm = PROJ_TILE
    n_seq = SEQ // tm
    tok = lambda i: (i, 0)
    const = lambda i: (0, 0)
    tab = lambda i: (i % n_seq, 0)
    widths = (6 * LANES, 6 * LANES, 3 * LANES, 3 * LANES, LANES, LANES,
              2 * LANES, 2 * LANES, 2 * LANES)
    return pl.pallas_call(
        _proj_kernel,
        grid=(T // tm,),
        in_specs=[
            pl.BlockSpec((tm, D_MODEL), tok),
            pl.BlockSpec((1, D_MODEL), const),
            pl.BlockSpec((D_MODEL, PROJ_WIDTH), const),
            pl.BlockSpec((1, MLA_Q_RANK), const),
            pl.BlockSpec((1, MLA_KV_RANK), const),
            pl.BlockSpec((MLA_Q_RANK, 6 * LANES), const),
            pl.BlockSpec((MLA_KV_RANK, 9 * LANES), const),
        ] + [pl.BlockSpec((tm, LANES), tab)] * 6,
        out_specs=[pl.BlockSpec((tm, w), tok) for w in widths],
        out_shape=[jax.ShapeDtypeStruct((T, w), jnp.bfloat16) for w in widths],
        compiler_params=pltpu.CompilerParams(
            dimension_semantics=("arbitrary",), vmem_limit_bytes=VMEM_LIMIT),
        name="proj",
    )(x, g, w1, qn, kvn, wuq, wukv, *tables)


def _fill_values(vx_ref, v_ref):
    lane = lax.broadcasted_iota(jnp.int32, v_ref.shape, 1)
    vx_ref[:, :LANES] = v_ref[...]
    vx_ref[:, LANES:] = jnp.where(lane == 0, 1.0, 0.0).astype(vx_ref.dtype)


def _softmax_pv(q, k, vx):
    s = lax.dot_general(q, k, _NT, preferred_element_type=jnp.float32)
    m = jnp.max(s, axis=-1, keepdims=True)
    p = jnp.exp2(s - m)
    o = jnp.dot(p.astype(jnp.bfloat16), vx, preferred_element_type=jnp.float32)
    return o[:, :LANES] / o[:, LANES:LANES + 1]


def _mla_kernel(q_ref, k_ref, v_ref, o_ref, vx_ref):
    _fill_values(vx_ref, v_ref)
    lane = lax.broadcasted_iota(jnp.int32, (ATTN_SUB, LANES), 1)
    for sb in range(ATTN_Q_TILE // ATTN_SUB):
        rows = slice(sb * ATTN_SUB, (sb + 1) * ATTN_SUB)
        o0 = _softmax_pv(q_ref[rows, :LANES], k_ref[:, :LANES], vx_ref[...])
        o1 = _softmax_pv(q_ref[rows, LANES:], k_ref[:, LANES:], vx_ref[...])
        o_ref[rows, :] = jnp.where(lane < MLA_V_DIM, o0, o1).astype(o_ref.dtype)


def _mla_call(q, k, v, batch):
    T = q.shape[0]
    tq = ATTN_Q_TILE
    nq = SEQ // tq
    return pl.pallas_call(
        _mla_kernel,
        grid=(batch, MLA_HEADS // 2, nq),
        in_specs=[
            pl.BlockSpec((tq, 2 * LANES), lambda b, p, i: (b * nq + i, p)),
            pl.BlockSpec((SEQ, 2 * LANES), lambda b, p, i: (b, p)),
            pl.BlockSpec((SEQ, LANES), lambda b, p, i: (b, p)),
        ],
        out_specs=pl.BlockSpec((tq, LANES), lambda b, p, i: (b * nq + i, p)),
        out_shape=jax.ShapeDtypeStruct((T, MLA_HEADS * MLA_V_DIM), jnp.bfloat16),
        scratch_shapes=[pltpu.VMEM((SEQ, 2 * LANES), jnp.bfloat16)],
        compiler_params=pltpu.CompilerParams(
            dimension_semantics=("arbitrary",) * 3, vmem_limit_bytes=VMEM_LIMIT),
        name="mla_attn",
    )(q, k, v)


def _swa_kernel(sink_ref, q_ref, k_ref, v_ref, o_ref):
    i = pl.program_id(1)
    nb = SEQ // WINDOW
    half = SWA_HEADS // 2
    rows = SWA_HEADS * WINDOW
    lane = lax.broadcasted_iota(jnp.int32, (WINDOW, LANES), 1)
    left = lane < HEAD_DIM
    row = lax.broadcasted_iota(jnp.int32, (rows, 3 * WINDOW), 0)
    col = lax.broadcasted_iota(jnp.int32, (rows, 3 * WINDOW), 1)
    rel = col - row % WINDOW
    band = (rel >= 0) & (rel <= 2 * WINDOW)
    head_col = lax.broadcasted_iota(jnp.int32, (rows, 1), 0) // WINDOW
    sink = jnp.zeros((rows, 1), jnp.float32)
    for hd in range(SWA_HEADS):
        sink = jnp.where(head_col == hd, sink_ref[hd], sink)
    for sub in range(SWA_Q_TILE // WINDOW):
        n = i * (SWA_Q_TILE // WINDOW) + sub
        prev = pl.multiple_of(jnp.maximum(n - 1, 0) * WINDOW, WINDOW)
        cur = pl.multiple_of(n * WINDOW, WINDOW)
        nxt = pl.multiple_of(jnp.minimum(n + 1, nb - 1) * WINDOW, WINDOW)
        kb = jnp.concatenate([k_ref[pl.ds(prev, WINDOW), :], k_ref[pl.ds(cur, WINDOW), :],
                              k_ref[pl.ds(nxt, WINDOW), :]], axis=0)
        vb = jnp.concatenate([v_ref[pl.ds(prev, WINDOW), :], v_ref[pl.ds(cur, WINDOW), :],
                              v_ref[pl.ds(nxt, WINDOW), :]], axis=0)
        lo_col = jnp.where(n > 0, 0, WINDOW)
        hi_col = jnp.where(n < nb - 1, 3 * WINDOW, 2 * WINDOW)
        valid = band & (col >= lo_col) & (col < hi_col)
        parts = [q_ref[sub * WINDOW:(sub + 1) * WINDOW, j * LANES:(j + 1) * LANES] for j in range(half)]
        zero = jnp.zeros_like(parts[0])
        qs = jnp.concatenate([jnp.where(left, p_, zero) for p_ in parts]
                             + [jnp.where(left, zero, p_) for p_ in parts], axis=0)
        s = lax.dot_general(qs, kb, _NT, preferred_element_type=jnp.float32)
        s = jnp.where(valid, s, -jnp.inf)
        m = jnp.maximum(jnp.max(s, axis=-1, keepdims=True), sink)
        p = jnp.exp2(s - m)
        l = jnp.sum(p, axis=-1, keepdims=True) + jnp.exp2(sink - m)
        o = jnp.dot(p.astype(jnp.bfloat16), vb, preferred_element_type=jnp.float32) / l
        for j in range(half):
            o_ref[sub * WINDOW:(sub + 1) * WINDOW, j * LANES:(j + 1) * LANES] = jnp.where(
                left, o[j * WINDOW:(j + 1) * WINDOW], o[(j + half) * WINDOW:(j + half + 1) * WINDOW]
            ).astype(o_ref.dtype)


def _swa_call(sink, q, k, v, batch):
    T = q.shape[0]
    tq = SWA_Q_TILE
    nq = SEQ // tq
    return pl.pallas_call(
        _swa_kernel,
        grid=(batch, nq),
        in_specs=[
            pl.BlockSpec(memory_space=pltpu.SMEM),
            pl.BlockSpec((tq, 3 * LANES), lambda b, i: (b * nq + i, 0)),
            pl.BlockSpec((SEQ, LANES), lambda b, i: (b, 0)),
            pl.BlockSpec((SEQ, LANES), lambda b, i: (b, 0)),
        ],
        out_specs=pl.BlockSpec((tq, 3 * LANES), lambda b, i: (b * nq + i, 0)),
        out_shape=jax.ShapeDtypeStruct((T, SWA_HEADS * HEAD_DIM), jnp.bfloat16),
        compiler_params=pltpu.CompilerParams(
            dimension_semantics=("arbitrary",) * 2, vmem_limit_bytes=VMEM_LIMIT),
        name="swa_attn",
    )(sink, q, k, v)


def _diff_kernel(lam_init, lq1_ref, lk1_ref, lq2_ref, lk2_ref, subln_ref,
                 q_ref, k_ref, v_ref, o_ref, vx_ref):
    lam = (jnp.exp(jnp.sum(lq1_ref[...] * lk1_ref[...], axis=-1, keepdims=True))
           - jnp.exp(jnp.sum(lq2_ref[...] * lk2_ref[...], axis=-1, keepdims=True))
           + lam_init)
    _fill_values(vx_ref, v_ref)
    lane = lax.broadcasted_iota(jnp.int32, (ATTN_SUB, LANES), 1)
    left = lane < DIFF_V_DIM
    for sb in range(DIFF_Q_TILE // ATTN_SUB):
        rows = slice(sb * ATTN_SUB, (sb + 1) * ATTN_SUB)
        q = q_ref[rows, :]
        zero = jnp.zeros_like(q)
        outs = []
        for side in range(2):
            comp = []
            for c in range(2):
                lo = side * DIFF_V_DIM + c * DIFF_QK_DIM
                qc = jnp.where((lane >= lo) & (lane < lo + DIFF_QK_DIM), q, zero)
                comp.append(_softmax_pv(qc, k_ref[...], vx_ref[...]))
            outs.append(comp[0] - lam * comp[1])
        o = jnp.where(left, outs[0], outs[1])
        sq = o * o
        ms_l = jnp.sum(jnp.where(left, sq, 0.0), axis=-1, keepdims=True) * (1.0 / DIFF_V_DIM)
        ms_r = jnp.sum(jnp.where(left, 0.0, sq), axis=-1, keepdims=True) * (1.0 / DIFF_V_DIM)
        r = jnp.where(left, lax.rsqrt(ms_l + NORM_EPS), lax.rsqrt(ms_r + NORM_EPS))
        o_ref[rows, :] = (o * r * subln_ref[...] * (1.0 - lam_init)).astype(o_ref.dtype)


def _diff_call(lam_init, lq1, lk1, lq2, lk2, subln2, q, k, v, batch):
    T = q.shape[0]
    tq = DIFF_Q_TILE
    nq = SEQ // tq
    small = lambda b, p, i: (0, 0)
    return pl.pallas_call(
        functools.partial(_diff_kernel, lam_init),
        grid=(batch, DIFF_HEADS // 2, nq),
        in_specs=[pl.BlockSpec((1, DIFF_QK_DIM), small)] * 4 + [
            pl.BlockSpec((1, LANES), small),
            pl.BlockSpec((tq, LANES), lambda b, p, i: (b * nq + i, p)),
            pl.BlockSpec((SEQ, LANES), lambda b, p, i: (b, p)),
            pl.BlockSpec((SEQ, LANES), lambda b, p, i: (b, p)),
        ],
        out_specs=pl.BlockSpec((tq, LANES), lambda b, p, i: (b * nq + i, p)),
        out_shape=jax.ShapeDtypeStruct((T, DIFF_HEADS * DIFF_V_DIM), jnp.bfloat16),
        scratch_shapes=[pltpu.VMEM((SEQ, 2 * LANES), jnp.bfloat16)],
        compiler_params=pltpu.CompilerParams(
            dimension_semantics=("arbitrary",) * 3, vmem_limit_bytes=VMEM_LIMIT),
        name="diff_attn",
    )(lq1, lk1, lq2, lk2, subln2, q, k, v)


def _outproj_kernel(x_ref, om_ref, os_ref, od_ref, wm_ref, ws_ref, wd_ref, g_ref,
                    wr2_ref, br_ref, x2_ref, h2_ref, lt_ref):
    x2 = (x_ref[...]
          + jnp.dot(om_ref[...], wm_ref[...], preferred_element_type=jnp.float32)
          + jnp.dot(os_ref[...], ws_ref[...], preferred_element_type=jnp.float32)
          + jnp.dot(od_ref[...], wd_ref[...], preferred_element_type=jnp.float32))
    x2_ref[...] = x2
    h2 = _rms(x2, g_ref[...])
    _store_rows(h2_ref, h2)

    hi = h2.astype(jnp.bfloat16)
    lo = (h2 - hi.astype(jnp.float32)).astype(jnp.bfloat16)
    a = jnp.dot(hi, wr2_ref[...], preferred_element_type=jnp.float32)
    b = jnp.dot(lo, wr2_ref[:, :LANES], preferred_element_type=jnp.float32)
    logits = a[:, :LANES] + a[:, LANES:] + b + br_ref[...]
    lt_ref[...] = logits.T


def _outproj_call(x, om, os_, od, wm, ws, wd, g, wr2, br):
    T = x.shape[0]
    tm = PROJ_TILE
    tok = lambda i: (i, 0)
    const = lambda i: (0, 0)
    return pl.pallas_call(
        _outproj_kernel,
        grid=(T // tm,),
        in_specs=[
            pl.BlockSpec((tm, D_MODEL), tok),
            pl.BlockSpec((tm, 3 * LANES), tok),
            pl.BlockSpec((tm, 3 * LANES), tok),
            pl.BlockSpec((tm, 2 * LANES), tok),
            pl.BlockSpec((3 * LANES, D_MODEL), const),
            pl.BlockSpec((3 * LANES, D_MODEL), const),
            pl.BlockSpec((2 * LANES, D_MODEL), const),
            pl.BlockSpec((1, D_MODEL), const),
            pl.BlockSpec((D_MODEL, 2 * LANES), const),
            pl.BlockSpec((1, LANES), const),
        ],
        out_specs=[
            pl.BlockSpec((tm, D_MODEL), tok),
            pl.BlockSpec((tm * ROW_TILES, LANES), tok),
            pl.BlockSpec((LANES, tm), lambda i: (0, i)),
        ],
        out_shape=[
            jax.ShapeDtypeStruct((T, D_MODEL), jnp.float32),
            jax.ShapeDtypeStruct((T * ROW_TILES, LANES), jnp.float32),
            jax.ShapeDtypeStruct((LANES, T), jnp.float32),
        ],
        compiler_params=pltpu.CompilerParams(
            dimension_semantics=("arbitrary",), vmem_limit_bytes=VMEM_LIMIT),
        name="outproj",
    )(x, om, os_, od, wm, ws, wd, g, wr2, br)


def _route_kernel(lt_ref, ri_ref, rf_ref, cnt_ref, carry_ref):
    step = pl.program_id(0)

    @pl.when(step == 0)
    def _():
        carry_ref[...] = jnp.zeros_like(carry_ref)

    tc = lt_ref.shape[1]
    neg = -jnp.inf
    row8 = lax.broadcasted_iota(jnp.int32, (8, tc), 0)
    is_g = row8 < N_GROUPS
    gl = jnp.where(is_g, lt_ref[0:8, :], neg)
    gmax = jnp.max(gl, axis=0, keepdims=True)
    gidx = jnp.min(jnp.where(gl == gmax, row8, 8), axis=0, keepdims=True)
    gsum = jnp.sum(jnp.where(is_g, jnp.exp(gl - gmax), 0.0), axis=0, keepdims=True)
    g_gate = 1.0 / gsum
    gidx8 = jnp.broadcast_to(gidx, (8, tc))
    e_in = lt_ref[EXPERT_LANE0:EXPERT_LANE0 + 8, :]
    for g in range(1, N_GROUPS):
        lo = EXPERT_LANE0 + g * EXPERTS_PER_GROUP
        e_in = jnp.where(gidx8 == g, lt_ref[lo:lo + EXPERTS_PER_GROUP, :], e_in)
    m1 = jnp.max(e_in, axis=0, keepdims=True)
    i1 = jnp.min(jnp.where(e_in == m1, row8, 8), axis=0, keepdims=True)
    e_rest = jnp.where(row8 == i1, neg, e_in)
    m2 = jnp.max(e_rest, axis=0, keepdims=True)
    i2 = jnp.min(jnp.where(e_rest == m2, row8, 8), axis=0, keepdims=True)
    t = jnp.exp(m2 - m1)
    w1 = g_gate / (1.0 + t)
    w2 = g_gate * t / (1.0 + t)
    e1 = gidx * EXPERTS_PER_GROUP + i1
    e2 = gidx * EXPERTS_PER_GROUP + i2

    rowe = lax.broadcasted_iota(jnp.int32, (N_EXPERTS, tc), 0)
    hit1 = rowe == e1
    hit2 = rowe == e2
    onehot = jnp.where(hit1 | hit2, 1.0, 0.0)
    r = lax.broadcasted_iota(jnp.int32, (RANK_CHUNK, RANK_CHUNK), 0)
    c = lax.broadcasted_iota(jnp.int32, (RANK_CHUNK, RANK_CHUNK), 1)
    upper = jnp.where(r < c, 1.0, 0.0).astype(jnp.bfloat16)
    carry = carry_ref[:, 0:1]
    parts = []
    for j in range(tc // RANK_CHUNK):
        oh = onehot[:, j * RANK_CHUNK:(j + 1) * RANK_CHUNK]
        parts.append(jnp.dot(oh.astype(jnp.bfloat16), upper, preferred_element_type=jnp.float32) + carry)
        carry = carry + jnp.sum(oh, axis=1, keepdims=True)
    before = jnp.concatenate(parts, axis=1)
    rank1 = jnp.sum(jnp.where(hit1, before, 0.0), axis=0, keepdims=True).astype(jnp.int32)
    rank2 = jnp.sum(jnp.where(hit2, before, 0.0), axis=0, keepdims=True).astype(jnp.int32)
    carry_ref[...] = jnp.broadcast_to(carry, carry_ref.shape)
    cnt_ref[...] = jnp.broadcast_to(carry, cnt_ref.shape)

    ri_ref[...] = jnp.where(row8 == 0, e1, jnp.where(row8 == 1, e2, jnp.where(
        row8 == 2, rank1, jnp.where(row8 == 3, rank2, 0))))
    rf_ref[...] = jnp.where(row8 == 0, w1, jnp.where(row8 == 1, w2, 0.0))


def _route_call(lt):
    T = lt.shape[1]
    tc = ROUTE_CHUNK
    return pl.pallas_call(
        _route_kernel,
        grid=(T // tc,),
        in_specs=[pl.BlockSpec((LANES, tc), lambda i: (0, i))],
        out_specs=[
            pl.BlockSpec((8, tc), lambda i: (0, i)),
            pl.BlockSpec((8, tc), lambda i: (0, i)),
            pl.BlockSpec((N_EXPERTS, LANES), lambda i: (0, 0)),
        ],
        out_shape=[
            jax.ShapeDtypeStruct((8, T), jnp.int32),
            jax.ShapeDtypeStruct((8, T), jnp.float32),
            jax.ShapeDtypeStruct((N_EXPERTS, LANES), jnp.float32),
        ],
        scratch_shapes=[pltpu.VMEM((N_EXPERTS, LANES), jnp.float32)],
        compiler_params=pltpu.CompilerParams(
            dimension_semantics=("arbitrary",), vmem_limit_bytes=VMEM_LIMIT),
        name="route",
    )(lt)


def _plan_kernel(ri_ref, cnt_ref, pos_ref, te_ref, meta_ref):
    T = ri_ref.shape[1]
    counts = cnt_ref[...].astype(jnp.int32)
    padded = (counts + (MOE_TILE - 1)) // MOE_TILE * MOE_TILE
    r = lax.broadcasted_iota(jnp.int32, (N_EXPERTS, N_EXPERTS), 0)
    c = lax.broadcasted_iota(jnp.int32, (N_EXPERTS, N_EXPERTS), 1)
    lower = jnp.where(c < r, 1.0, 0.0)
    off = jnp.dot(lower, padded.astype(jnp.float32), preferred_element_type=jnp.float32,
                  precision=lax.Precision.HIGHEST).astype(jnp.int32)
    ends = off + padded
    total = jnp.max(ends, axis=0, keepdims=True)

    rowe = lax.broadcasted_iota(jnp.int32, (N_EXPERTS, T), 0)
    off_col = off[:, 0:1]
    row8 = lax.broadcasted_iota(jnp.int32, (8, T), 0)
    pos1 = jnp.sum(jnp.where(rowe == ri_ref[0:1, :], off_col, 0), axis=0, keepdims=True) + ri_ref[2:3, :]
    pos2 = jnp.sum(jnp.where(rowe == ri_ref[1:2, :], off_col, 0), axis=0, keepdims=True) + ri_ref[3:4, :]
    pos_ref[...] = jnp.where(row8 == 0, pos1, jnp.where(row8 == 1, pos2, 0))

    n_lanes = te_ref.shape[1]
    start = lax.broadcasted_iota(jnp.int32, (N_EXPERTS, n_lanes), 1) * MOE_TILE
    start = jnp.minimum(start, total[:, 0:1] - MOE_TILE)
    te = jnp.sum((ends[:, 0:1] <= start).astype(jnp.int32), axis=0, keepdims=True)
    te_ref[...] = jnp.broadcast_to(te, te_ref.shape)
    meta_ref[...] = jnp.broadcast_to(total // MOE_TILE, meta_ref.shape)


def _plan_call(ri, cnt, max_tiles):
    T = ri.shape[1]
    n_lanes = -(-max_tiles // LANES) * LANES
    full = lambda i: (0, 0)
    return pl.pallas_call(
        _plan_kernel,
        grid=(1,),
        in_specs=[pl.BlockSpec((8, T), full), pl.BlockSpec((N_EXPERTS, LANES), full)],
        out_specs=[pl.BlockSpec((8, T), full), pl.BlockSpec((8, n_lanes), full),
                   pl.BlockSpec((N_EXPERTS, LANES), full)],
        out_shape=[jax.ShapeDtypeStruct((8, T), jnp.int32),
                   jax.ShapeDtypeStruct((8, n_lanes), jnp.int32),
                   jax.ShapeDtypeStruct((N_EXPERTS, LANES), jnp.int32)],
        compiler_params=pltpu.CompilerParams(
            dimension_semantics=("arbitrary",), vmem_limit_bytes=VMEM_LIMIT),
        name="slot_plan",
    )(ri, cnt)


def _row(ref, idx):
    return ref.at[pl.ds(pl.multiple_of(idx * ROW_TILES, ROW_TILES), ROW_TILES), :]


def _invert_kernel(pos_ref, src_in, dst_in, src_ref, dst_ref, sem):
    n_tok = pos_ref.shape[0] // 2
    init = [pltpu.make_async_copy(src_in, src_ref, sem.at[0]),
            pltpu.make_async_copy(dst_in, dst_ref, sem.at[1])]
    for cp in init:
        cp.start()
    for cp in init:
        cp.wait()

    def body(t, carry):
        p0 = pos_ref[t]
        p1 = pos_ref[n_tok + t]
        src_ref[p0] = t
        dst_ref[p0] = t
        src_ref[p1] = t
        dst_ref[p1] = n_tok + t
        return carry

    lax.fori_loop(0, n_tok, body, 0, unroll=8)


def _invert_call(pos, n_slots):
    n_tok = pos.shape[0] // 2
    slot = jnp.arange(n_slots, dtype=jnp.int32)
    smem = pl.BlockSpec(memory_space=pltpu.SMEM)
    hbm = pl.BlockSpec(memory_space=pl.ANY)
    return pl.pallas_call(
        _invert_kernel,
        grid_spec=pltpu.PrefetchScalarGridSpec(
            num_scalar_prefetch=1,
            grid=(1,),
            in_specs=[hbm, hbm],
            out_specs=[smem, smem],
            scratch_shapes=[pltpu.SemaphoreType.DMA((2,))],
        ),
        out_shape=[jax.ShapeDtypeStruct((n_slots,), jnp.int32)] * 2,
        compiler_params=pltpu.CompilerParams(dimension_semantics=("arbitrary",)),
        name="moe_invert",
    )(pos, jnp.zeros((n_slots,), jnp.int32), 2 * n_tok + slot % (2 * MOE_TILE))


def _expert_kernel(te_ref, nt_ref, src_ref, dst_ref, h_ref, wg_ref, wu_ref, wd_ref, y_ref,
                   xbuf, ybuf, wg_s, wu_s, wd_s, gsem, ssem):
    j = pl.program_id(0)
    nt = nt_ref[0]
    n_tok = h_ref.shape[0] // ROW_TILES

    def gather(tile, slot, start):
        def body(r, carry):
            if start:
                tok = src_ref[tile * MOE_TILE + r]
                pltpu.make_async_copy(_row(h_ref, tok), _row(xbuf.at[slot], r), gsem.at[slot]).start(priority=0)
            else:
                pltpu.make_async_copy(_row(h_ref, 0), _row(xbuf.at[slot], 0), gsem.at[slot]).wait()
            return carry
        lax.fori_loop(0, MOE_TILE, body, 0, unroll=8)

    def scatter(tile, slot, start):
        def body(r, carry):
            if start:
                dst = dst_ref[tile * MOE_TILE + r]
                pltpu.make_async_copy(_row(ybuf.at[slot], r), _row(y_ref, dst), ssem.at[slot]).start(priority=1)
            else:
                pltpu.make_async_copy(_row(ybuf.at[slot], 0), _row(y_ref, 0), ssem.at[slot]).wait()
            return carry
        lax.fori_loop(0, MOE_TILE, body, 0, unroll=8)

    @pl.when(j == 0)
    def _():
        gather(0, 0, True)
        ybuf[...] = jnp.zeros_like(ybuf)
        rows = MOE_TILE * ROW_TILES
        fills = [pltpu.make_async_copy(
            ybuf.at[s], y_ref.at[pl.ds((2 * n_tok + s * MOE_TILE) * ROW_TILES, rows), :], ssem.at[s])
            for s in range(2)]
        for f in fills:
            f.start()
        for f in fills:
            f.wait()

    @pl.when(j + 1 < nt)
    def _():
        gather(j + 1, (j + 1) % 2, True)

    @pl.when(j < nt)
    def _():
        slot = j % 2
        gather(j, slot, False)

        @pl.when(j >= 2)
        def _():
            scatter(j - 2, slot, False)

        changed = (j == 0) | (te_ref[j] != te_ref[jnp.maximum(j - 1, 0)])

        @pl.when(changed)
        def _():
            wg_s[...] = wg_ref[...].astype(jnp.bfloat16)
            wu_s[...] = wu_ref[...].astype(jnp.bfloat16)
            wd_s[...] = wd_ref[...].astype(jnp.bfloat16)

        xb = _load_rows(xbuf.at[slot], MOE_TILE).astype(jnp.bfloat16)
        a = jnp.dot(xb, wg_s[...], preferred_element_type=jnp.float32)
        u = jnp.dot(xb, wu_s[...], preferred_element_type=jnp.float32)
        hid = (a * jax.nn.sigmoid(a) * u).astype(jnp.bfloat16)
        _store_rows(ybuf.at[slot], jnp.dot(hid, wd_s[...], preferred_element_type=jnp.float32))
        scatter(j, slot, True)

        @pl.when(j == nt - 1)
        def _():
            scatter(j, slot, False)

            @pl.when(j >= 1)
            def _():
                scatter(j - 1, 1 - slot, False)


def _expert_call(tile_expert, n_tiles, src, dst, h2, wg, wu, wd, layer):
    n_tok = h2.shape[0] // ROW_TILES
    max_tiles = tile_expert.shape[0]
    base = layer * N_EXPERTS
    wsel = lambda j, te, nt, src, dst: (base + te[j], 0, 0)
    tile_rows = MOE_TILE * ROW_TILES
    return pl.pallas_call(
        _expert_kernel,
        grid_spec=pltpu.PrefetchScalarGridSpec(
            num_scalar_prefetch=4,
            grid=(max_tiles,),
            in_specs=[
                pl.BlockSpec(memory_space=pl.ANY),
                pl.BlockSpec((None, D_MODEL, EXPERT_FF), wsel),
                pl.BlockSpec((None, D_MODEL, EXPERT_FF), wsel),
                pl.BlockSpec((None, EXPERT_FF, D_MODEL), wsel),
            ],
            out_specs=pl.BlockSpec(memory_space=pl.ANY),
            scratch_shapes=[pltpu.VMEM((2, tile_rows, LANES), jnp.float32),
                            pltpu.VMEM((2, tile_rows, LANES), jnp.float32),
                            pltpu.VMEM((D_MODEL, EXPERT_FF), jnp.bfloat16),
                            pltpu.VMEM((D_MODEL, EXPERT_FF), jnp.bfloat16),
                            pltpu.VMEM((EXPERT_FF, D_MODEL), jnp.bfloat16),
                            pltpu.SemaphoreType.DMA((2,)),
                            pltpu.SemaphoreType.DMA((2,))],
        ),
        out_shape=jax.ShapeDtypeStruct(((2 * n_tok + 2 * MOE_TILE) * ROW_TILES, LANES), jnp.float32),
        compiler_params=pltpu.CompilerParams(
            dimension_semantics=("arbitrary",), vmem_limit_bytes=VMEM_LIMIT),
        name="moe_experts",
    )(tile_expert, n_tiles, src, dst, h2, wg, wu, wd)


def _combine_kernel(final, x2_ref, rf_ref, g_ref, y0_ref, y1_ref, o_ref):
    tm = x2_ref.shape[0]
    rf = rf_ref[...]
    y = x2_ref[...] + rf[:, 0:1] * _load_rows(y0_ref, tm) + rf[:, 1:2] * _load_rows(y1_ref, tm)
    if final:
        y = _rms(y, g_ref[...])
    o_ref[...] = y


def _combine_call(x2, rf, g, y, final):
    T = x2.shape[0]
    tm = MOVE_TILE
    nb = T // tm
    return pl.pallas_call(
        functools.partial(_combine_kernel, final),
        grid=(nb,),
        in_specs=[
            pl.BlockSpec((tm, D_MODEL), lambda i: (i, 0)),
            pl.BlockSpec((tm, 2), lambda i: (i, 0)),
            pl.BlockSpec((1, D_MODEL), lambda i: (0, 0)),
            pl.BlockSpec((tm * ROW_TILES, LANES), lambda i: (i, 0)),
            pl.BlockSpec((tm * ROW_TILES, LANES), lambda i: (nb + i, 0)),
        ],
        out_specs=pl.BlockSpec((tm, D_MODEL), lambda i: (i, 0)),
        out_shape=jax.ShapeDtypeStruct((T, D_MODEL), jnp.float32),
        compiler_params=pltpu.CompilerParams(
            dimension_semantics=("arbitrary",), vmem_limit_bytes=VMEM_LIMIT),
        name="moe_combine",
    )(x2, rf, g, y, y)


def _rope_tables():
    def cos_sin(dim):
        inv = 1.0 / (ROPE_THETA ** (jnp.arange(0, dim, 2, dtype=jnp.float32) / dim))
        ang = jnp.arange(SEQ, dtype=jnp.float32)[:, None] * inv[None, :]
        return jnp.cos(ang), jnp.sin(ang)

    c16, s16 = cos_sin(MLA_ROPE_DIM)
    c32, s32 = cos_sin(HEAD_DIM)
    ones = jnp.ones((SEQ, 64), jnp.float32)
    zeros = jnp.zeros((SEQ, 64), jnp.float32)
    pad = jnp.zeros((SEQ, 32), jnp.float32)
    cm = jnp.concatenate([ones, c16, c16, pad], axis=1)
    sm = jnp.concatenate([zeros, -s16, s16, pad], axis=1)
    cs = jnp.tile(jnp.concatenate([c32, c32], axis=1), (1, 2))
    ss = jnp.tile(jnp.concatenate([-s32, s32], axis=1), (1, 2))
    cd = jnp.tile(jnp.concatenate([c16, c16], axis=1), (1, 4))
    sd = jnp.tile(jnp.concatenate([-s16, s16], axis=1), (1, 4))
    return cm, sm, cs, ss, cd, sd


def _layer_weights(w_in, w_uq, w_ukv, w_out):
    f32 = jnp.float32
    cq = w_in[:, 0:384]
    ckv = w_in[:, 384:640]
    kr = w_in[:, 640:672]
    sq = w_in[:, 672:1056] * (LOG2E * HEAD_DIM ** -0.5)
    sk = w_in[:, 1056:1184]
    sv = w_in[:, 1184:1312]
    dq = w_in[:, 1312:1568] * (LOG2E * DIFF_QK_DIM ** -0.5)
    dk = w_in[:, 1568:1824]
    dv = w_in[:, 1824:2080]
    kr_pad = jnp.concatenate([jnp.zeros((D_MODEL, 64), f32), kr, jnp.zeros((D_MODEL, 32), f32)], axis=1)
    half = SWA_HEADS // 2
    sq_h = sq.reshape(D_MODEL, SWA_HEADS, HEAD_DIM)
    sq_perm = jnp.stack([sq_h[:, :half], sq_h[:, half:]], axis=2).reshape(D_MODEL, SWA_HEADS * HEAD_DIM)
    w1 = jnp.concatenate([cq, ckv, kr_pad, sq_perm, sk, sv, dq, dk, dv], axis=1).astype(jnp.bfloat16)

    scale = LOG2E * (MLA_NOPE_DIM + MLA_ROPE_DIM) ** -0.5
    uq = (w_uq * scale).reshape(MLA_Q_RANK, MLA_HEADS, MLA_NOPE_DIM + MLA_ROPE_DIM)
    uq = jnp.concatenate([uq, jnp.zeros((MLA_Q_RANK, MLA_HEADS, 32), f32)], axis=2)
    wuq = uq.reshape(MLA_Q_RANK, MLA_HEADS * LANES).astype(jnp.bfloat16)
    ukv = w_ukv.reshape(MLA_KV_RANK, MLA_HEADS, MLA_NOPE_DIM + MLA_V_DIM)
    uk = jnp.concatenate([ukv[:, :, :MLA_NOPE_DIM], jnp.zeros((MLA_KV_RANK, MLA_HEADS, 64), f32)], axis=2)
    uv = ukv[:, :, MLA_NOPE_DIM:]
    wukv = jnp.concatenate([uk.reshape(MLA_KV_RANK, MLA_HEADS * LANES),
                            uv.reshape(MLA_KV_RANK, MLA_HEADS * MLA_V_DIM)], axis=1).astype(jnp.bfloat16)

    wm = w_out[0:384].astype(jnp.bfloat16)
    wo_s = w_out[384:768].reshape(SWA_HEADS, HEAD_DIM, D_MODEL)
    ws = jnp.stack([wo_s[:half], wo_s[half:]], axis=1).reshape(SWA_HEADS * HEAD_DIM, D_MODEL).astype(jnp.bfloat16)
    wd = w_out[768:1024].astype(jnp.bfloat16)
    return w1, wuq, wukv, wm, ws, wd


def _router_weights(w_rg, b_rg, w_re, b_re):
    f32 = jnp.float32
    wr = jnp.concatenate([w_rg, jnp.zeros((D_MODEL, EXPERT_LANE0 - N_GROUPS), f32), w_re,
                          jnp.zeros((D_MODEL, LANES - EXPERT_LANE0 - N_EXPERTS), f32)], axis=1)
    br = jnp.concatenate([b_rg, jnp.zeros((EXPERT_LANE0 - N_GROUPS,), f32), b_re,
                          jnp.zeros((LANES - EXPERT_LANE0 - N_EXPERTS,), f32)])[None, :]
    hi = wr.astype(jnp.bfloat16)
    lo = (wr - hi.astype(f32)).astype(jnp.bfloat16)
    return jnp.concatenate([hi, lo], axis=1), br


def kernel(x, attn_norm, w_in, mla_q_norm, mla_kv_norm, mla_w_uq, mla_w_ukv, swa_sink, diff_lq1, diff_lk1, diff_lq2, diff_lk2, diff_subln, w_out, ffn_norm, router_group, router_group_bias, router_expert, router_expert_bias, w_gate, w_up, w_down, final_norm):
    batch, seq, d = x.shape
    assert (seq, d) == (SEQ, D_MODEL)
    depth = w_in.shape[0]
    n_tok = batch * seq
    tables = _rope_tables()
    wg = w_gate.reshape(depth * N_EXPERTS, D_MODEL, EXPERT_FF)
    wu = w_up.reshape(depth * N_EXPERTS, D_MODEL, EXPERT_FF)
    wdn = w_down.reshape(depth * N_EXPERTS, EXPERT_FF, D_MODEL)
    xt = x.reshape(n_tok, D_MODEL)
    for l in range(depth):
        w1, wuq, wukv, wm, ws, wd = _layer_weights(w_in[l], mla_w_uq[l], mla_w_ukv[l], w_out[l])
        qm, km, vm, qs, ks, vs, qd, kd, vd = _proj_call(
            xt, attn_norm[l][None], w1, mla_q_norm[l][None], mla_kv_norm[l][None], wuq, wukv, tables)
        o_mla = _mla_call(qm, km, vm, batch)
        o_swa = _swa_call(swa_sink[l] * LOG2E, qs, ks, vs, batch)
        lam_init = 0.8 - 0.6 * math.exp(-0.3 * l)
        subln2 = jnp.concatenate([diff_subln[l], diff_subln[l]])[None]
        o_diff = _diff_call(lam_init, diff_lq1[l][None], diff_lk1[l][None], diff_lq2[l][None],
                            diff_lk2[l][None], subln2, qd, kd, vd, batch)
        wr2, br = _router_weights(router_group[l], router_group_bias[l],
                                  router_expert[l], router_expert_bias[l])
        x2, h2, lt = _outproj_call(xt, o_mla, o_swa, o_diff, wm, ws, wd, ffn_norm[l][None], wr2, br)
        ri, rf, cnt = _route_call(lt)
        max_tiles = (2 * n_tok) // MOE_TILE + N_EXPERTS
        pos8, te8, meta = _plan_call(ri, cnt, max_tiles)
        src, dst = _invert_call(pos8[0:2].reshape(2 * n_tok), max_tiles * MOE_TILE)
        y = _expert_call(te8[0, :max_tiles], meta[0, 0:1], src, dst, h2, wg, wu, wdn, l)
        xt = _combine_call(x2, rf[0:2].T, final_norm[None], y, l == depth - 1)
    return xt.reshape(batch, seq, d)
```

```python
import functools
import math

import jax
import jax.numpy as jnp
from jax import lax
from jax.experimental import pallas as pl
from jax.experimental.pallas import tpu as pltpu

D_MODEL = 1024
SEQ = 2048
HEAD_DIM = 64
ROPE_THETA = 10000.0
NORM_EPS = 1e-6
MLA_HEADS = 6
MLA_Q_RANK = 384
MLA_KV_RANK = 256
MLA_NOPE_DIM = 64
MLA_ROPE_DIM = 32
MLA_V_DIM = 64
SWA_HEADS = 6
SWA_KV_HEADS = 2
WINDOW = 128
DIFF_HEADS = 4
DIFF_QK_DIM = 32
DIFF_V_DIM = 64
N_GROUPS = 4
EXPERTS_PER_GROUP = 8
N_EXPERTS = N_GROUPS * EXPERTS_PER_GROUP
EXPERT_FF = 256

LANES = 128
ROW_TILES = D_MODEL // LANES
EXPERT_LANE0 = 8
PROJ_TILE = 512
ATTN_Q_TILE = 2048
DIFF_Q_TILE = 1024
ATTN_SUB = 256
SWA_Q_TILE = 256
MOE_TILE = 256
MOVE_TILE = 256
ROUTE_CHUNK = 2048
RANK_CHUNK = 512
VMEM_LIMIT = 56 * 1024 * 1024

SEG_CQ = 0
SEG_CKV = SEG_CQ + MLA_Q_RANK
SEG_KR = SEG_CKV + MLA_KV_RANK
SEG_SQ = SEG_KR + LANES
SEG_SK = SEG_SQ + SWA_HEADS * HEAD_DIM
SEG_SV = SEG_SK + LANES
SEG_DQ = SEG_SV + LANES
SEG_DK = SEG_DQ + 2 * LANES
SEG_DV = SEG_DK + 2 * LANES
PROJ_WIDTH = SEG_DV + 2 * LANES

_NT = (((1,), (1,)), ((), ()))
LOG2E = math.log2(math.e)


def _rms(x, g):
    return x * lax.rsqrt(jnp.mean(x * x, axis=-1, keepdims=True) + NORM_EPS) * g


def _load_rows(ref, n):
    return jnp.concatenate([ref[pl.ds(c, n, stride=ROW_TILES), :] for c in range(ROW_TILES)], axis=1)


def _store_rows(ref, val):
    n = val.shape[0]
    for c in range(ROW_TILES):
        ref[pl.ds(c, n, stride=ROW_TILES), :] = val[:, c * LANES:(c + 1) * LANES]


def _rope(x, cos, sin, first_half, shift):
    up = pltpu.roll(x, LANES - shift, 1)
    dn = pltpu.roll(x, shift, 1)
    return x * cos + jnp.where(first_half, up, dn) * sin


def _proj_kernel(x_ref, *refs):
    _proj_body(x_ref[...], *refs)


def _row(ref, idx):
    return ref.at[pl.ds(pl.multiple_of(idx * ROW_TILES, ROW_TILES), ROW_TILES), :]


def _proj_combine_kernel(pos_ref, x2_ref, rf_ref, ys_ref, *refs):
    *body_refs, x3_ref, g0, g1, sem = refs
    step = pl.program_id(0)
    n_steps = pl.num_programs(0)
    tm = x2_ref.shape[0]
    n_tok = n_steps * tm

    def rows(tile, slot, start):
        def body(r, carry):
            if start:
                t = tile * tm + r
                pltpu.make_async_copy(_row(ys_ref, pos_ref[t]), _row(g0.at[slot], r), sem.at[slot]).start(priority=0)
                pltpu.make_async_copy(_row(ys_ref, pos_ref[n_tok + t]), _row(g1.at[slot], r), sem.at[slot]).start(priority=1)
            else:
                for g in (g0, g1):
                    pltpu.make_async_copy(_row(ys_ref, 0), _row(g.at[slot], 0), sem.at[slot]).wait()
            return carry
        lax.fori_loop(0, tm, body, 0, unroll=8)

    @pl.when(step == 0)
    def _():
        rows(0, 0, True)

    @pl.when(step + 1 < n_steps)
    def _():
        rows(step + 1, (step + 1) % 2, True)

    slot = step % 2
    rows(step, slot, False)
    rf = rf_ref[...]
    x = x2_ref[...] + rf[:, 0:1] * _load_rows(g0.at[slot], tm) + rf[:, 1:2] * _load_rows(g1.at[slot], tm)
    x3_ref[...] = x
    _proj_body(x, *body_refs)


def _proj_body(x, g_ref, w1_ref, qn_ref, kvn_ref, wuq_ref, wukv_ref,
               cm_ref, sm_ref, cs_ref, ss_ref, cd_ref, sd_ref,
               qm_ref, km_ref, vm_ref, qs_ref, ks_ref, vs_ref,
               qd_ref, kd_ref, vd_ref):
    h = _rms(x, g_ref[...]).astype(jnp.bfloat16)
    u = jnp.dot(h, w1_ref[...], preferred_element_type=jnp.float32)

    lane = lax.broadcasted_iota(jnp.int32, (x.shape[0], LANES), 1)
    first_m = (lane >= 64) & (lane < 80)
    first_s = (lane % 64) < 32
    first_d = (lane % 32) < 16
    cm, sm = cm_ref[...], sm_ref[...]
    cs, ss = cs_ref[...], ss_ref[...]
    cd, sd = cd_ref[...], sd_ref[...]

    cq = _rms(u[:, SEG_CQ:SEG_CQ + MLA_Q_RANK], qn_ref[...]).astype(jnp.bfloat16)
    q = jnp.dot(cq, wuq_ref[...], preferred_element_type=jnp.float32)
    ckv = _rms(u[:, SEG_CKV:SEG_CKV + MLA_KV_RANK], kvn_ref[...]).astype(jnp.bfloat16)
    kv = jnp.dot(ckv, wukv_ref[...], preferred_element_type=jnp.float32)
    kr = _rope(u[:, SEG_KR:SEG_KR + LANES], cm, sm, first_m, 16)
    for hd in range(MLA_HEADS):
        sl = slice(hd * LANES, (hd + 1) * LANES)
        qm_ref[:, sl] = _rope(q[:, sl], cm, sm, first_m, 16).astype(jnp.bfloat16)
        km_ref[:, sl] = (kv[:, sl] + kr).astype(jnp.bfloat16)
    vm_ref[...] = kv[:, MLA_HEADS * LANES:].astype(jnp.bfloat16)

    for j in range(SWA_HEADS // 2):
        sl = slice(SEG_SQ + j * LANES, SEG_SQ + (j + 1) * LANES)
        qs_ref[:, j * LANES:(j + 1) * LANES] = _rope(u[:, sl], cs, ss, first_s, 32).astype(jnp.bfloat16)
    ks_ref[...] = _rope(u[:, SEG_SK:SEG_SK + LANES], cs, ss, first_s, 32).astype(jnp.bfloat16)
    vs_ref[...] = u[:, SEG_SV:SEG_SV + LANES].astype(jnp.bfloat16)

    for j in range(2):
        sq = slice(SEG_DQ + j * LANES, SEG_DQ + (j + 1) * LANES)
        sk = slice(SEG_DK + j * LANES, SEG_DK + (j + 1) * LANES)
        qd_ref[:, j * LANES:(j + 1) * LANES] = _rope(u[:, sq], cd, sd, first_d, 16).astype(jnp.bfloat16)
        kd_ref[:, j * LANES:(j + 1) * LANES] = _rope(u[:, sk], cd, sd, first_d, 16).astype(jnp.bfloat16)
    vd_ref[...] = u[:, SEG_DV:SEG_DV + 2 * LANES].astype(jnp.bfloat16)


def _proj_call(x, g, w1, qn, kvn, wuq, wukv, tables, combine=None):
    T = x.shape[0]
    tm = PROJ_TILE
    n_seq = SEQ // tm
    tok = lambda i, *_: (i, 0)
    const = lambda i, *_: (0, 0)
    tab = lambda i, *_: (i % n_seq, 0)
    widths = (6 * LANES, 6 * LANES, 3 * LANES, 3 * LANES, LANES, LANES,
              2 * LANES, 2 * LANES, 2 * LANES)
    weight_specs = [
        pl.BlockSpec((1, D_MODEL), const),
        pl.BlockSpec((D_MODEL, PROJ_WIDTH), const),
        pl.BlockSpec((1, MLA_Q_RANK), const),
        pl.BlockSpec((1, MLA_KV_RANK), const),
        pl.BlockSpec((MLA_Q_RANK, 6 * LANES), const),
        pl.BlockSpec((MLA_KV_RANK, 9 * LANES), const),
    ] + [pl.BlockSpec((tm, LANES), tab)] * 6
    out_specs = [pl.BlockSpec((tm, w), tok) for w in widths]
    out_shape = [jax.ShapeDtypeStruct((T, w), jnp.bfloat16) for w in widths]
    params = pltpu.CompilerParams(dimension_semantics=("arbitrary",), vmem_limit_bytes=VMEM_LIMIT)
    weights = (g, w1, qn, kvn, wuq, wukv, *tables)
    if combine is None:
        return pl.pallas_call(
            _proj_kernel,
            grid=(T // tm,),
            in_specs=[pl.BlockSpec((tm, D_MODEL), tok)] + weight_specs,
            out_specs=out_specs, out_shape=out_shape, compiler_params=params, name="proj",
        )(x, *weights)
    pos, rf, ys = combine
    return pl.pallas_call(
        _proj_combine_kernel,
        grid_spec=pltpu.PrefetchScalarGridSpec(
            num_scalar_prefetch=1,
            grid=(T // tm,),
            in_specs=[pl.BlockSpec((tm, D_MODEL), tok), pl.BlockSpec((tm, 2), tok),
                      pl.BlockSpec(memory_space=pl.ANY)] + weight_specs,
            out_specs=out_specs + [pl.BlockSpec((tm, D_MODEL), tok)],
            scratch_shapes=[pltpu.VMEM((2, tm * ROW_TILES, LANES), jnp.float32),
                            pltpu.VMEM((2, tm * ROW_TILES, LANES), jnp.float32),
                            pltpu.SemaphoreType.DMA((2,))],
        ),
        out_shape=out_shape + [jax.ShapeDtypeStruct((T, D_MODEL), jnp.float32)],
        compiler_params=params, name="proj_combine",
    )(pos, x, rf, ys, *weights)


def _fill_values(vx_ref, v_ref):
    lane = lax.broadcasted_iota(jnp.int32, v_ref.shape, 1)
    vx_ref[:, :LANES] = v_ref[...]
    vx_ref[:, LANES:] = jnp.where(lane == 0, 1.0, 0.0).astype(vx_ref.dtype)


def _softmax_pv(q, k, vx):
    s = lax.dot_general(q, k, _NT, preferred_element_type=jnp.float32)
    m = jnp.max(s, axis=-1, keepdims=True)
    p = jnp.exp2(s - m)
    o = jnp.dot(p.astype(jnp.bfloat16), vx, preferred_element_type=jnp.float32)
    return o[:, :LANES] / o[:, LANES:LANES + 1]


def _mla_kernel(q_ref, k_ref, v_ref, o_ref, vx_ref):
    _fill_values(vx_ref, v_ref)
    lane = lax.broadcasted_iota(jnp.int32, (ATTN_SUB, LANES), 1)
    for sb in range(ATTN_Q_TILE // ATTN_SUB):
        rows = slice(sb * ATTN_SUB, (sb + 1) * ATTN_SUB)
        o0 = _softmax_pv(q_ref[rows, :LANES], k_ref[:, :LANES], vx_ref[...])
        o1 = _softmax_pv(q_ref[rows, LANES:], k_ref[:, LANES:], vx_ref[...])
        o_ref[rows, :] = jnp.where(lane < MLA_V_DIM, o0, o1).astype(o_ref.dtype)


def _mla_call(q, k, v, batch):
    T = q.shape[0]
    tq = ATTN_Q_TILE
    nq = SEQ // tq
    return pl.pallas_call(
        _mla_kernel,
        grid=(batch, MLA_HEADS // 2, nq),
        in_specs=[
            pl.BlockSpec((tq, 2 * LANES), lambda b, p, i: (b * nq + i, p)),
            pl.BlockSpec((SEQ, 2 * LANES), lambda b, p, i: (b, p)),
            pl.BlockSpec((SEQ, LANES), lambda b, p, i: (b, p)),
        ],
        out_specs=pl.BlockSpec((tq, LANES), lambda b, p, i: (b * nq + i, p)),
        out_shape=jax.ShapeDtypeStruct((T, MLA_HEADS * MLA_V_DIM), jnp.bfloat16),
        scratch_shapes=[pltpu.VMEM((SEQ, 2 * LANES), jnp.bfloat16)],
        compiler_params=pltpu.CompilerParams(
            dimension_semantics=("arbitrary",) * 3, vmem_limit_bytes=VMEM_LIMIT),
        name="mla_attn",
    )(q, k, v)


def _swa_kernel(sink_ref, q_ref, k_ref, v_ref, o_ref):
    i = pl.program_id(1)
    nb = SEQ // WINDOW
    half = SWA_HEADS // 2
    rows = SWA_HEADS * WINDOW
    lane = lax.broadcasted_iota(jnp.int32, (WINDOW, LANES), 1)
    left = lane < HEAD_DIM
    row = lax.broadcasted_iota(jnp.int32, (rows, 3 * WINDOW), 0)
    col = lax.broadcasted_iota(jnp.int32, (rows, 3 * WINDOW), 1)
    rel = col - row % WINDOW
    band = (rel >= 0) & (rel <= 2 * WINDOW)
    head_col = lax.broadcasted_iota(jnp.int32, (rows, 1), 0) // WINDOW
    sink = jnp.zeros((rows, 1), jnp.float32)
    for hd in range(SWA_HEADS):
        sink = jnp.where(head_col == hd, sink_ref[hd], sink)
    for sub in range(SWA_Q_TILE // WINDOW):
        n = i * (SWA_Q_TILE // WINDOW) + sub
        prev = pl.multiple_of(jnp.maximum(n - 1, 0) * WINDOW, WINDOW)
        cur = pl.multiple_of(n * WINDOW, WINDOW)
        nxt = pl.multiple_of(jnp.minimum(n + 1, nb - 1) * WINDOW, WINDOW)
        kb = jnp.concatenate([k_ref[pl.ds(prev, WINDOW), :], k_ref[pl.ds(cur, WINDOW), :],
                              k_ref[pl.ds(nxt, WINDOW), :]], axis=0)
        vb = jnp.concatenate([v_ref[pl.ds(prev, WINDOW), :], v_ref[pl.ds(cur, WINDOW), :],
                              v_ref[pl.ds(nxt, WINDOW), :]], axis=0)
        lo_col = jnp.where(n > 0, 0, WINDOW)
        hi_col = jnp.where(n < nb - 1, 3 * WINDOW, 2 * WINDOW)
        valid = band & (col >= lo_col) & (col < hi_col)
        parts = [q_ref[sub * WINDOW:(sub + 1) * WINDOW, j * LANES:(j + 1) * LANES] for j in range(half)]
        zero = jnp.zeros_like(parts[0])
        qs = jnp.concatenate([jnp.where(left, p_, zero) for p_ in parts]
                             + [jnp.where(left, zero, p_) for p_ in parts], axis=0)
        s = lax.dot_general(qs, kb, _NT, preferred_element_type=jnp.float32)
        s = jnp.where(valid, s, -jnp.inf)
        m = jnp.maximum(jnp.max(s, axis=-1, keepdims=True), sink)
        p = jnp.exp2(s - m)
        l = jnp.sum(p, axis=-1, keepdims=True) + jnp.exp2(sink - m)
        o = jnp.dot(p.astype(jnp.bfloat16), vb, preferred_element_type=jnp.float32) / l
        for j in range(half):
            o_ref[sub * WINDOW:(sub + 1) * WINDOW, j * LANES:(j + 1) * LANES] = jnp.where(
                left, o[j * WINDOW:(j + 1) * WINDOW], o[(j + half) * WINDOW:(j + half + 1) * WINDOW]
            ).astype(o_ref.dtype)


def _swa_call(sink, q, k, v, batch):
    T = q.shape[0]
    tq = SWA_Q_TILE
    nq = SEQ // tq
    return pl.pallas_call(
        _swa_kernel,
        grid=(batch, nq),
        in_specs=[
            pl.BlockSpec(memory_space=pltpu.SMEM),
            pl.BlockSpec((tq, 3 * LANES), lambda b, i: (b * nq + i, 0)),
            pl.BlockSpec((SEQ, LANES), lambda b, i: (b, 0)),
            pl.BlockSpec((SEQ, LANES), lambda b, i: (b, 0)),
        ],
        out_specs=pl.BlockSpec((tq, 3 * LANES), lambda b, i: (b * nq + i, 0)),
        out_shape=jax.ShapeDtypeStruct((T, SWA_HEADS * HEAD_DIM), jnp.bfloat16),
        compiler_params=pltpu.CompilerParams(
            dimension_semantics=("arbitrary",) * 2, vmem_limit_bytes=VMEM_LIMIT),
        name="swa_attn",
    )(sink, q, k, v)


def _diff_kernel(lam_init, lq1_ref, lk1_ref, lq2_ref, lk2_ref, subln_ref,
                 q_ref, k_ref, v_ref, o_ref, vx_ref):
    lam = (jnp.exp(jnp.sum(lq1_ref[...] * lk1_ref[...], axis=-1, keepdims=True))
           - jnp.exp(jnp.sum(lq2_ref[...] * lk2_ref[...], axis=-1, keepdims=True))
           + lam_init)
    _fill_values(vx_ref, v_ref)
    lane = lax.broadcasted_iota(jnp.int32, (ATTN_SUB, LANES), 1)
    left = lane < DIFF_V_DIM
    for sb in range(DIFF_Q_TILE // ATTN_SUB):
        rows = slice(sb * ATTN_SUB, (sb + 1) * ATTN_SUB)
        q = q_ref[rows, :]
        zero = jnp.zeros_like(q)
        outs = []
        for side in range(2):
            comp = []
            for c in range(2):
                lo = side * DIFF_V_DIM + c * DIFF_QK_DIM
                qc = jnp.where((lane >= lo) & (lane < lo + DIFF_QK_DIM), q, zero)
                comp.append(_softmax_pv(qc, k_ref[...], vx_ref[...]))
            outs.append(comp[0] - lam * comp[1])
        o = jnp.where(left, outs[0], outs[1])
        sq = o * o
        ms_l = jnp.sum(jnp.where(left, sq, 0.0), axis=-1, keepdims=True) * (1.0 / DIFF_V_DIM)
        ms_r = jnp.sum(jnp.where(left, 0.0, sq), axis=-1, keepdims=True) * (1.0 / DIFF_V_DIM)
        r = jnp.where(left, lax.rsqrt(ms_l + NORM_EPS), lax.rsqrt(ms_r + NORM_EPS))
        o_ref[rows, :] = (o * r * subln_ref[...] * (1.0 - lam_init)).astype(o_ref.dtype)


def _diff_call(lam_init, lq1, lk1, lq2, lk2, subln2, q, k, v, batch):
    T = q.shape[0]
    tq = DIFF_Q_TILE
    nq = SEQ // tq
    small = lambda b, p, i: (0, 0)
    return pl.pallas_call(
        functools.partial(_diff_kernel, lam_init),
        grid=(batch, DIFF_HEADS // 2, nq),
        in_specs=[pl.BlockSpec((1, DIFF_QK_DIM), small)] * 4 + [
            pl.BlockSpec((1, LANES), small),
            pl.BlockSpec((tq, LANES), lambda b, p, i: (b * nq + i, p)),
            pl.BlockSpec((SEQ, LANES), lambda b, p, i: (b, p)),
            pl.BlockSpec((SEQ, LANES), lambda b, p, i: (b, p)),
        ],
        out_specs=pl.BlockSpec((tq, LANES), lambda b, p, i: (b * nq + i, p)),
        out_shape=jax.ShapeDtypeStruct((T, DIFF_HEADS * DIFF_V_DIM), jnp.bfloat16),
        scratch_shapes=[pltpu.VMEM((SEQ, 2 * LANES), jnp.bfloat16)],
        compiler_params=pltpu.CompilerParams(
            dimension_semantics=("arbitrary",) * 3, vmem_limit_bytes=VMEM_LIMIT),
        name="diff_attn",
    )(lq1, lk1, lq2, lk2, subln2, q, k, v)


def _outproj_kernel(x_ref, om_ref, os_ref, od_ref, wm_ref, ws_ref, wd_ref, g_ref,
                    wr2_ref, br_ref, x2_ref, h2_ref, lt_ref):
    x2 = (x_ref[...]
          + jnp.dot(om_ref[...], wm_ref[...], preferred_element_type=jnp.float32)
          + jnp.dot(os_ref[...], ws_ref[...], preferred_element_type=jnp.float32)
          + jnp.dot(od_ref[...], wd_ref[...], preferred_element_type=jnp.float32))
    x2_ref[...] = x2
    h2 = _rms(x2, g_ref[...])
    _store_rows(h2_ref, h2)

    hi = h2.astype(jnp.bfloat16)
    lo = (h2 - hi.astype(jnp.float32)).astype(jnp.bfloat16)
    a = jnp.dot(hi, wr2_ref[...], preferred_element_type=jnp.float32)
    b = jnp.dot(lo, wr2_ref[:, :LANES], preferred_element_type=jnp.float32)
    logits = a[:, :LANES] + a[:, LANES:] + b + br_ref[...]
    lt_ref[...] = logits.T


def _outproj_call(x, om, os_, od, wm, ws, wd, g, wr2, br):
    T = x.shape[0]
    tm = PROJ_TILE
    tok = lambda i: (i, 0)
    const = lambda i: (0, 0)
    return pl.pallas_call(
        _outproj_kernel,
        grid=(T // tm,),
        in_specs=[
            pl.BlockSpec((tm, D_MODEL), tok),
            pl.BlockSpec((tm, 3 * LANES), tok),
            pl.BlockSpec((tm, 3 * LANES), tok),
            pl.BlockSpec((tm, 2 * LANES), tok),
            pl.BlockSpec((3 * LANES, D_MODEL), const),
            pl.BlockSpec((3 * LANES, D_MODEL), const),
            pl.BlockSpec((2 * LANES, D_MODEL), const),
            pl.BlockSpec((1, D_MODEL), const),
            pl.BlockSpec((D_MODEL, 2 * LANES), const),
            pl.BlockSpec((1, LANES), const),
        ],
        out_specs=[
            pl.BlockSpec((tm, D_MODEL), tok),
            pl.BlockSpec((tm * ROW_TILES, LANES), tok),
            pl.BlockSpec((LANES, tm), lambda i: (0, i)),
        ],
        out_shape=[
            jax.ShapeDtypeStruct((T, D_MODEL), jnp.float32),
            jax.ShapeDtypeStruct((T * ROW_TILES, LANES), jnp.float32),
            jax.ShapeDtypeStruct((LANES, T), jnp.float32),
        ],
        compiler_params=pltpu.CompilerParams(
            dimension_semantics=("arbitrary",), vmem_limit_bytes=VMEM_LIMIT),
        name="outproj",
    )(x, om, os_, od, wm, ws, wd, g, wr2, br)


def _route_kernel(lt_ref, ri_ref, rf_ref, cnt_ref, carry_ref):
    step = pl.program_id(0)

    @pl.when(step == 0)
    def _():
        carry_ref[...] = jnp.zeros_like(carry_ref)

    tc = lt_ref.shape[1]
    neg = -jnp.inf
    row8 = lax.broadcasted_iota(jnp.int32, (8, tc), 0)
    is_g = row8 < N_GROUPS
    gl = jnp.where(is_g, lt_ref[0:8, :], neg)
    gmax = jnp.max(gl, axis=0, keepdims=True)
    gidx = jnp.min(jnp.where(gl == gmax, row8, 8), axis=0, keepdims=True)
    gsum = jnp.sum(jnp.where(is_g, jnp.exp(gl - gmax), 0.0), axis=0, keepdims=True)
    g_gate = 1.0 / gsum
    gidx8 = jnp.broadcast_to(gidx, (8, tc))
    e_in = lt_ref[EXPERT_LANE0:EXPERT_LANE0 + 8, :]
    for g in range(1, N_GROUPS):
        lo = EXPERT_LANE0 + g * EXPERTS_PER_GROUP
        e_in = jnp.where(gidx8 == g, lt_ref[lo:lo + EXPERTS_PER_GROUP, :], e_in)
    m1 = jnp.max(e_in, axis=0, keepdims=True)
    i1 = jnp.min(jnp.where(e_in == m1, row8, 8), axis=0, keepdims=True)
    e_rest = jnp.where(row8 == i1, neg, e_in)
    m2 = jnp.max(e_rest, axis=0, keepdims=True)
    i2 = jnp.min(jnp.where(e_rest == m2, row8, 8), axis=0, keepdims=True)
    t = jnp.exp(m2 - m1)
    w1 = g_gate / (1.0 + t)
    w2 = g_gate * t / (1.0 + t)
    e1 = gidx * EXPERTS_PER_GROUP + i1
    e2 = gidx * EXPERTS_PER_GROUP + i2

    rowe = lax.broadcasted_iota(jnp.int32, (N_EXPERTS, tc), 0)
    hit1 = rowe == e1
    hit2 = rowe == e2
    onehot = jnp.where(hit1 | hit2, 1.0, 0.0)
    r = lax.broadcasted_iota(jnp.int32, (RANK_CHUNK, RANK_CHUNK), 0)
    c = lax.broadcasted_iota(jnp.int32, (RANK_CHUNK, RANK_CHUNK), 1)
    upper = jnp.where(r < c, 1.0, 0.0).astype(jnp.bfloat16)
    carry = carry_ref[:, 0:1]
    parts = []
    for j in range(tc // RANK_CHUNK):
        oh = onehot[:, j * RANK_CHUNK:(j + 1) * RANK_CHUNK]
        parts.append(jnp.dot(oh.astype(jnp.bfloat16), upper, preferred_element_type=jnp.float32) + carry)
        carry = carry + jnp.sum(oh, axis=1, keepdims=True)
    before = jnp.concatenate(parts, axis=1)
    rank1 = jnp.sum(jnp.where(hit1, before, 0.0), axis=0, keepdims=True).astype(jnp.int32)
    rank2 = jnp.sum(jnp.where(hit2, before, 0.0), axis=0, keepdims=True).astype(jnp.int32)
    carry_ref[...] = jnp.broadcast_to(carry, carry_ref.shape)
    cnt_ref[...] = jnp.broadcast_to(carry, cnt_ref.shape)

    ri_ref[...] = jnp.where(row8 == 0, e1, jnp.where(row8 == 1, e2, jnp.where(
        row8 == 2, rank1, jnp.where(row8 == 3, rank2, 0))))
    rf_ref[...] = jnp.where(row8 == 0, w1, jnp.where(row8 == 1, w2, 0.0))


def _route_call(lt):
    T = lt.shape[1]
    tc = ROUTE_CHUNK
    return pl.pallas_call(
        _route_kernel,
        grid=(T // tc,),
        in_specs=[pl.BlockSpec((LANES, tc), lambda i: (0, i))],
        out_specs=[
            pl.BlockSpec((8, tc), lambda i: (0, i)),
            pl.BlockSpec((8, tc), lambda i: (0, i)),
            pl.BlockSpec((N_EXPERTS, LANES), lambda i: (0, 0)),
        ],
        out_shape=[
            jax.ShapeDtypeStruct((8, T), jnp.int32),
            jax.ShapeDtypeStruct((8, T), jnp.float32),
            jax.ShapeDtypeStruct((N_EXPERTS, LANES), jnp.float32),
        ],
        scratch_shapes=[pltpu.VMEM((N_EXPERTS, LANES), jnp.float32)],
        compiler_params=pltpu.CompilerParams(
            dimension_semantics=("arbitrary",), vmem_limit_bytes=VMEM_LIMIT),
        name="route",
    )(lt)


def _plan_kernel(ri_ref, cnt_ref, pos_ref, te_ref, meta_ref):
    T = ri_ref.shape[1]
    counts = cnt_ref[...].astype(jnp.int32)
    padded = (counts + (MOE_TILE - 1)) // MOE_TILE * MOE_TILE
    r = lax.broadcasted_iota(jnp.int32, (N_EXPERTS, N_EXPERTS), 0)
    c = lax.broadcasted_iota(jnp.int32, (N_EXPERTS, N_EXPERTS), 1)
    lower = jnp.where(c < r, 1.0, 0.0)
    off = jnp.dot(lower, padded.astype(jnp.float32), preferred_element_type=jnp.float32,
                  precision=lax.Precision.HIGHEST).astype(jnp.int32)
    ends = off + padded
    total = jnp.max(ends, axis=0, keepdims=True)

    rowe = lax.broadcasted_iota(jnp.int32, (N_EXPERTS, T), 0)
    off_col = off[:, 0:1]
    row8 = lax.broadcasted_iota(jnp.int32, (8, T), 0)
    pos1 = jnp.sum(jnp.where(rowe == ri_ref[0:1, :], off_col, 0), axis=0, keepdims=True) + ri_ref[2:3, :]
    pos2 = jnp.sum(jnp.where(rowe == ri_ref[1:2, :], off_col, 0), axis=0, keepdims=True) + ri_ref[3:4, :]
    pos_ref[...] = jnp.where(row8 == 0, pos1, jnp.where(row8 == 1, pos2, 0))

    n_lanes = te_ref.shape[1]
    start = lax.broadcasted_iota(jnp.int32, (N_EXPERTS, n_lanes), 1) * MOE_TILE
    start = jnp.minimum(start, total[:, 0:1] - MOE_TILE)
    te = jnp.sum((ends[:, 0:1] <= start).astype(jnp.int32), axis=0, keepdims=True)
    te_ref[...] = jnp.broadcast_to(te, te_ref.shape)
    lane = lax.broadcasted_iota(jnp.int32, (N_EXPERTS, LANES), 1)
    fill = jnp.where(padded > 0, ends - MOE_TILE, -1)
    meta_ref[...] = jnp.where(lane == 0, fill, total // MOE_TILE)


def _plan_call(ri, cnt, max_tiles):
    T = ri.shape[1]
    n_lanes = -(-max_tiles // LANES) * LANES
    full = lambda i: (0, 0)
    return pl.pallas_call(
        _plan_kernel,
        grid=(1,),
        in_specs=[pl.BlockSpec((8, T), full), pl.BlockSpec((N_EXPERTS, LANES), full)],
        out_specs=[pl.BlockSpec((8, T), full), pl.BlockSpec((8, n_lanes), full),
                   pl.BlockSpec((N_EXPERTS, LANES), full)],
        out_shape=[jax.ShapeDtypeStruct((8, T), jnp.int32),
                   jax.ShapeDtypeStruct((8, n_lanes), jnp.int32),
                   jax.ShapeDtypeStruct((N_EXPERTS, LANES), jnp.int32)],
        compiler_params=pltpu.CompilerParams(
            dimension_semantics=("arbitrary",), vmem_limit_bytes=VMEM_LIMIT),
        name="slot_plan",
    )(ri, cnt)


def _push_kernel(pos_ref, fill_ref, nt_ref, h_ref, xs_ref, zero_ref, sem):
    step = pl.program_id(0)
    tm = h_ref.shape[0] // ROW_TILES
    n_tok = pl.num_programs(0) * tm
    fill_rows = MOE_TILE * ROW_TILES
    max_tiles = xs_ref.shape[0] // fill_rows

    def fill_copy(slot):
        start = pl.multiple_of(slot * ROW_TILES, fill_rows)
        return pltpu.make_async_copy(zero_ref, xs_ref.at[pl.ds(start, fill_rows), :], sem)

    @pl.when(step == 0)
    def _():
        zero_ref[...] = jnp.zeros_like(zero_ref)
        for e in range(N_EXPERTS):
            @pl.when(fill_ref[e] >= 0)
            def _():
                fill_copy(fill_ref[e]).start()

        def tail_start(j, carry):
            fill_copy(j * MOE_TILE).start()
            return carry

        def tail_wait(j, carry):
            fill_copy(j * MOE_TILE).wait()
            return carry

        lax.fori_loop(nt_ref[0], max_tiles, tail_start, 0)
        for e in range(N_EXPERTS):
            @pl.when(fill_ref[e] >= 0)
            def _():
                fill_copy(fill_ref[e]).wait()
        lax.fori_loop(nt_ref[0], max_tiles, tail_wait, 0)

    def issue(r, carry):
        t = step * tm + r
        for k in range(2):
            pltpu.make_async_copy(_row(h_ref, r), _row(xs_ref, pos_ref[k * n_tok + t]), sem).start(priority=k)
        return carry

    lax.fori_loop(0, tm, issue, 0, unroll=8)

    def drain(r, carry):
        for k in range(2):
            pltpu.make_async_copy(_row(h_ref, 0), _row(xs_ref, 0), sem).wait()
        return carry

    lax.fori_loop(0, tm, drain, 0, unroll=8)


def _push_call(pos, fill, n_tiles, h2, n_slots):
    tm = MOVE_TILE
    n_tok = h2.shape[0] // ROW_TILES
    return pl.pallas_call(
        _push_kernel,
        grid_spec=pltpu.PrefetchScalarGridSpec(
            num_scalar_prefetch=3,
            grid=(n_tok // tm,),
            in_specs=[pl.BlockSpec((tm * ROW_TILES, LANES), lambda i, pos, fill, nt: (i, 0))],
            out_specs=pl.BlockSpec(memory_space=pl.ANY),
            scratch_shapes=[pltpu.VMEM((MOE_TILE * ROW_TILES, LANES), jnp.float32),
                            pltpu.SemaphoreType.DMA(())],
        ),
        out_shape=jax.ShapeDtypeStruct((n_slots * ROW_TILES, LANES), jnp.float32),
        compiler_params=pltpu.CompilerParams(
            dimension_semantics=("arbitrary",), vmem_limit_bytes=VMEM_LIMIT),
        name="moe_push",
    )(pos, fill, n_tiles, h2)


def _expert_kernel(te_ref, nt_ref, xs_ref, wg_ref, wu_ref, wd_ref, ys_ref,
                   wg_s, wu_s, wd_s):
    j = pl.program_id(0)

    @pl.when(j < nt_ref[0])
    def _():
        changed = (j == 0) | (te_ref[j] != te_ref[jnp.maximum(j - 1, 0)])

        @pl.when(changed)
        def _():
            wg_s[...] = wg_ref[...].astype(jnp.bfloat16)
            wu_s[...] = wu_ref[...].astype(jnp.bfloat16)
            wd_s[...] = wd_ref[...].astype(jnp.bfloat16)

        xb = _load_rows(xs_ref, MOE_TILE).astype(jnp.bfloat16)
        a = jnp.dot(xb, wg_s[...], preferred_element_type=jnp.float32)
        u = jnp.dot(xb, wu_s[...], preferred_element_type=jnp.float32)
        hid = (a * jax.nn.sigmoid(a) * u).astype(jnp.bfloat16)
        _store_rows(ys_ref, jnp.dot(hid, wd_s[...], preferred_element_type=jnp.float32))


def _expert_call(tile_expert, n_tiles, xs, wg, wu, wd, layer):
    n_slots = xs.shape[0] // ROW_TILES
    max_tiles = n_slots // MOE_TILE
    base = layer * N_EXPERTS
    row = lambda j, te, nt: (jnp.minimum(j, nt[0] - 1), 0)
    wsel = lambda j, te, nt: (base + te[j], 0, 0)
    return pl.pallas_call(
        _expert_kernel,
        grid_spec=pltpu.PrefetchScalarGridSpec(
            num_scalar_prefetch=2,
            grid=(max_tiles,),
            in_specs=[
                pl.BlockSpec((MOE_TILE * ROW_TILES, LANES), row),
                pl.BlockSpec((None, D_MODEL, EXPERT_FF), wsel),
                pl.BlockSpec((None, D_MODEL, EXPERT_FF), wsel),
                pl.BlockSpec((None, EXPERT_FF, D_MODEL), wsel),
            ],
            out_specs=pl.BlockSpec((MOE_TILE * ROW_TILES, LANES), row),
            scratch_shapes=[pltpu.VMEM((D_MODEL, EXPERT_FF), jnp.bfloat16),
                            pltpu.VMEM((D_MODEL, EXPERT_FF), jnp.bfloat16),
                            pltpu.VMEM((EXPERT_FF, D_MODEL), jnp.bfloat16)],
        ),
        out_shape=jax.ShapeDtypeStruct((n_slots * ROW_TILES, LANES), jnp.float32),
        input_output_aliases={2: 0},
        compiler_params=pltpu.CompilerParams(
            dimension_semantics=("arbitrary",), vmem_limit_bytes=VMEM_LIMIT),
        name="moe_experts",
    )(tile_expert, n_tiles, xs, wg, wu, wd)


def _combine_kernel(pos_ref, x2_ref, rf_ref, g_ref, ys_ref, o_ref, g0, g1, sem):
    step = pl.program_id(0)
    tm = x2_ref.shape[0]
    n_tok = pl.num_programs(0) * tm

    def issue(r, carry):
        t = step * tm + r
        pltpu.make_async_copy(_row(ys_ref, pos_ref[t]), _row(g0, r), sem).start(priority=0)
        pltpu.make_async_copy(_row(ys_ref, pos_ref[n_tok + t]), _row(g1, r), sem).start(priority=1)
        return carry

    lax.fori_loop(0, tm, issue, 0, unroll=8)

    def drain(r, carry):
        for k in range(2):
            pltpu.make_async_copy(_row(ys_ref, 0), _row(g0, 0), sem).wait()
        return carry

    lax.fori_loop(0, tm, drain, 0, unroll=8)

    rf = rf_ref[...]
    y = x2_ref[...] + rf[:, 0:1] * _load_rows(g0, tm) + rf[:, 1:2] * _load_rows(g1, tm)
    o_ref[...] = _rms(y, g_ref[...])


def _combine_call(pos, x2, rf, g, ys):
    T = x2.shape[0]
    tm = MOVE_TILE
    return pl.pallas_call(
        _combine_kernel,
        grid_spec=pltpu.PrefetchScalarGridSpec(
            num_scalar_prefetch=1,
            grid=(T // tm,),
            in_specs=[
                pl.BlockSpec((tm, D_MODEL), lambda i, pos: (i, 0)),
                pl.BlockSpec((tm, 2), lambda i, pos: (i, 0)),
                pl.BlockSpec((1, D_MODEL), lambda i, pos: (0, 0)),
                pl.BlockSpec(memory_space=pl.ANY),
            ],
            out_specs=pl.BlockSpec((tm, D_MODEL), lambda i, pos: (i, 0)),
            scratch_shapes=[pltpu.VMEM((tm * ROW_TILES, LANES), jnp.float32),
                            pltpu.VMEM((tm * ROW_TILES, LANES), jnp.float32),
                            pltpu.SemaphoreType.DMA(())],
        ),
        out_shape=jax.ShapeDtypeStruct((T, D_MODEL), jnp.float32),
        compiler_params=pltpu.CompilerParams(
            dimension_semantics=("arbitrary",), vmem_limit_bytes=VMEM_LIMIT),
        name="moe_combine",
    )(pos, x2, rf, g, ys)


def _rope_tables():
    def cos_sin(dim):
        inv = 1.0 / (ROPE_THETA ** (jnp.arange(0, dim, 2, dtype=jnp.float32) / dim))
        ang = jnp.arange(SEQ, dtype=jnp.float32)[:, None] * inv[None, :]
        return jnp.cos(ang), jnp.sin(ang)

    c16, s16 = cos_sin(MLA_ROPE_DIM)
    c32, s32 = cos_sin(HEAD_DIM)
    ones = jnp.ones((SEQ, 64), jnp.float32)
    zeros = jnp.zeros((SEQ, 64), jnp.float32)
    pad = jnp.zeros((SEQ, 32), jnp.float32)
    cm = jnp.concatenate([ones, c16, c16, pad], axis=1)
    sm = jnp.concatenate([zeros, -s16, s16, pad], axis=1)
    cs = jnp.tile(jnp.concatenate([c32, c32], axis=1), (1, 2))
    ss = jnp.tile(jnp.concatenate([-s32, s32], axis=1), (1, 2))
    cd = jnp.tile(jnp.concatenate([c16, c16], axis=1), (1, 4))
    sd = jnp.tile(jnp.concatenate([-s16, s16], axis=1), (1, 4))
    return cm, sm, cs, ss, cd, sd


def _layer_weights(w_in, w_uq, w_ukv, w_out):
    f32 = jnp.float32
    cq = w_in[:, 0:384]
    ckv = w_in[:, 384:640]
    kr = w_in[:, 640:672]
    sq = w_in[:, 672:1056] * (LOG2E * HEAD_DIM ** -0.5)
    sk = w_in[:, 1056:1184]
    sv = w_in[:, 1184:1312]
    dq = w_in[:, 1312:1568] * (LOG2E * DIFF_QK_DIM ** -0.5)
    dk = w_in[:, 1568:1824]
    dv = w_in[:, 1824:2080]
    kr_pad = jnp.concatenate([jnp.zeros((D_MODEL, 64), f32), kr, jnp.zeros((D_MODEL, 32), f32)], axis=1)
    half = SWA_HEADS // 2
    sq_h = sq.reshape(D_MODEL, SWA_HEADS, HEAD_DIM)
    sq_perm = jnp.stack([sq_h[:, :half], sq_h[:, half:]], axis=2).reshape(D_MODEL, SWA_HEADS * HEAD_DIM)
    w1 = jnp.concatenate([cq, ckv, kr_pad, sq_perm, sk, sv, dq, dk, dv], axis=1).astype(jnp.bfloat16)

    scale = LOG2E * (MLA_NOPE_DIM + MLA_ROPE_DIM) ** -0.5
    uq = (w_uq * scale).reshape(MLA_Q_RANK, MLA_HEADS, MLA_NOPE_DIM + MLA_ROPE_DIM)
    uq = jnp.concatenate([uq, jnp.zeros((MLA_Q_RANK, MLA_HEADS, 32), f32)], axis=2)
    wuq = uq.reshape(MLA_Q_RANK, MLA_HEADS * LANES).astype(jnp.bfloat16)
    ukv = w_ukv.reshape(MLA_KV_RANK, MLA_HEADS, MLA_NOPE_DIM + MLA_V_DIM)
    uk = jnp.concatenate([ukv[:, :, :MLA_NOPE_DIM], jnp.zeros((MLA_KV_RANK, MLA_HEADS, 64), f32)], axis=2)
    uv = ukv[:, :, MLA_NOPE_DIM:]
    wukv = jnp.concatenate([uk.reshape(MLA_KV_RANK, MLA_HEADS * LANES),
                            uv.reshape(MLA_KV_RANK, MLA_HEADS * MLA_V_DIM)], axis=1).astype(jnp.bfloat16)

    wm = w_out[0:384].astype(jnp.bfloat16)
    wo_s = w_out[384:768].reshape(SWA_HEADS, HEAD_DIM, D_MODEL)
    ws = jnp.stack([wo_s[:half], wo_s[half:]], axis=1).reshape(SWA_HEADS * HEAD_DIM, D_MODEL).astype(jnp.bfloat16)
    wd = w_out[768:1024].astype(jnp.bfloat16)
    return w1, wuq, wukv, wm, ws, wd


def _router_weights(w_rg, b_rg, w_re, b_re):
    f32 = jnp.float32
    wr = jnp.concatenate([w_rg, jnp.zeros((D_MODEL, EXPERT_LANE0 - N_GROUPS), f32), w_re,
                          jnp.zeros((D_MODEL, LANES - EXPERT_LANE0 - N_EXPERTS), f32)], axis=1)
    br = jnp.concatenate([b_rg, jnp.zeros((EXPERT_LANE0 - N_GROUPS,), f32), b_re,
                          jnp.zeros((LANES - EXPERT_LANE0 - N_EXPERTS,), f32)])[None, :]
    hi = wr.astype(jnp.bfloat16)
    lo = (wr - hi.astype(f32)).astype(jnp.bfloat16)
    return jnp.concatenate([hi, lo], axis=1), br


def kernel(x, attn_norm, w_in, mla_q_norm, mla_kv_norm, mla_w_uq, mla_w_ukv, swa_sink, diff_lq1, diff_lk1, diff_lq2, diff_lk2, diff_subln, w_out, ffn_norm, router_group, router_group_bias, router_expert, router_expert_bias, w_gate, w_up, w_down, final_norm):
    batch, seq, d = x.shape
    assert (seq, d) == (SEQ, D_MODEL)
    depth = w_in.shape[0]
    n_tok = batch * seq
    tables = _rope_tables()
    wg = w_gate.reshape(depth * N_EXPERTS, D_MODEL, EXPERT_FF)
    wu = w_up.reshape(depth * N_EXPERTS, D_MODEL, EXPERT_FF)
    wdn = w_down.reshape(depth * N_EXPERTS, EXPERT_FF, D_MODEL)
    xt = x.reshape(n_tok, D_MODEL)
    pending = None
    for l in range(depth):
        w1, wuq, wukv, wm, ws, wd = _layer_weights(w_in[l], mla_w_uq[l], mla_w_ukv[l], w_out[l])
        outs = _proj_call(xt, attn_norm[l][None], w1, mla_q_norm[l][None], mla_kv_norm[l][None],
                          wuq, wukv, tables, combine=pending)
        qm, km, vm, qs, ks, vs, qd, kd, vd = outs[:9]
        if pending is not None:
            xt = outs[9]
        o_mla = _mla_call(qm, km, vm, batch)
        o_swa = _swa_call(swa_sink[l] * LOG2E, qs, ks, vs, batch)
        lam_init = 0.8 - 0.6 * math.exp(-0.3 * l)
        subln2 = jnp.concatenate([diff_subln[l], diff_subln[l]])[None]
        o_diff = _diff_call(lam_init, diff_lq1[l][None], diff_lk1[l][None], diff_lq2[l][None],
                            diff_lk2[l][None], subln2, qd, kd, vd, batch)
        wr2, br = _router_weights(router_group[l], router_group_bias[l],
                                  router_expert[l], router_expert_bias[l])
        x2, h2, lt = _outproj_call(xt, o_mla, o_swa, o_diff, wm, ws, wd, ffn_norm[l][None], wr2, br)
        ri, rf, cnt = _route_call(lt)
        max_tiles = (2 * n_tok) // MOE_TILE + N_EXPERTS
        pos8, te8, meta = _plan_call(ri, cnt, max_tiles)
        pos = pos8[0:2].reshape(2 * n_tok)
        fill, n_tiles = meta[:, 0], meta[0, 1:2]
        xs = _push_call(pos, fill, n_tiles, h2, max_tiles * MOE_TILE)
        ys = _expert_call(te8[0, :max_tiles], n_tiles, xs, wg, wu, wdn, l)
        xt, pending = x2, (pos, rf[0:2].T, ys)
    pos, rf2, ys = pending
    out = _combine_call(pos, xt, rf2, final_norm[None], ys)
    return out.reshape(batch, seq, d)
```

```python
import functools
import math

import jax
import jax.numpy as jnp
from jax import lax
from jax.experimental import pallas as pl
from jax.experimental.pallas import tpu as pltpu

D_MODEL = 1024
SEQ = 2048
HEAD_DIM = 64
ROPE_THETA = 10000.0
NORM_EPS = 1e-6
MLA_HEADS = 6
MLA_Q_RANK = 384
MLA_KV_RANK = 256
MLA_NOPE_DIM = 64
MLA_ROPE_DIM = 32
MLA_V_DIM = 64
SWA_HEADS = 6
SWA_KV_HEADS = 2
WINDOW = 128
DIFF_HEADS = 4
DIFF_QK_DIM = 32
DIFF_V_DIM = 64
N_GROUPS = 4
EXPERTS_PER_GROUP = 8
N_EXPERTS = N_GROUPS * EXPERTS_PER_GROUP
EXPERT_FF = 256

LANES = 128
ROW_TILES = D_MODEL // LANES
EXPERT_LANE0 = 8
PROJ_TILE = 512
PROJ_SUB = 256
ATTN_Q_TILE = 2048
DIFF_Q_TILE = 1024
ATTN_SUB = 256
SWA_Q_TILE = 512
MOE_TILE = 256
MOVE_TILE = 256
EXPERT_RING = 3
ROUTE_CHUNK = 2048
RANK_CHUNK = 512
VMEM_LIMIT = 56 * 1024 * 1024

SEG_CQ = 0
SEG_CKV = SEG_CQ + MLA_Q_RANK
SEG_KR = SEG_CKV + MLA_KV_RANK
SEG_SQ = SEG_KR + LANES
SEG_SK = SEG_SQ + SWA_HEADS * HEAD_DIM
SEG_SV = SEG_SK + LANES
SEG_DQ = SEG_SV + LANES
SEG_DK = SEG_DQ + 2 * LANES
SEG_DV = SEG_DK + 2 * LANES
PROJ_WIDTH = SEG_DV + 2 * LANES

_NT = (((1,), (1,)), ((), ()))
LOG2E = math.log2(math.e)


def _rms(x, g):
    return x * lax.rsqrt(jnp.mean(x * x, axis=-1, keepdims=True) + NORM_EPS) * g


def _load_rows(ref, n):
    return jnp.concatenate([ref[pl.ds(c, n, stride=ROW_TILES), :] for c in range(ROW_TILES)], axis=1)


def _store_rows(ref, val):
    n = val.shape[0]
    for c in range(ROW_TILES):
        ref[pl.ds(c, n, stride=ROW_TILES), :] = val[:, c * LANES:(c + 1) * LANES]


def _rope(x, cos, sin, first_half, shift):
    up = pltpu.roll(x, LANES - shift, 1)
    dn = pltpu.roll(x, shift, 1)
    return x * cos + jnp.where(first_half, up, dn) * sin


def _proj_kernel(x_ref, *refs):
    _proj_body(x_ref[...], *refs)


def _row(ref, idx):
    return ref.at[pl.ds(pl.multiple_of(idx * ROW_TILES, ROW_TILES), ROW_TILES), :]


def _proj_combine_kernel(pos_ref, x2_ref, rf_ref, ys_ref, *refs):
    *body_refs, x3_ref, g0, g1, sem = refs
    step = pl.program_id(0)
    n_steps = pl.num_programs(0)
    tm = x2_ref.shape[0]
    n_tok = n_steps * tm

    def rows(tile, slot, start):
        def body(r, carry):
            if start:
                t = tile * tm + r
                pltpu.make_async_copy(_row(ys_ref, pos_ref[t]), _row(g0.at[slot], r), sem.at[slot]).start(priority=0)
                pltpu.make_async_copy(_row(ys_ref, pos_ref[n_tok + t]), _row(g1.at[slot], r), sem.at[slot]).start(priority=1)
            else:
                for g in (g0, g1):
                    pltpu.make_async_copy(_row(ys_ref, 0), _row(g.at[slot], 0), sem.at[slot]).wait()
            return carry
        lax.fori_loop(0, tm, body, 0, unroll=8)

    @pl.when(step == 0)
    def _():
        rows(0, 0, True)

    @pl.when(step + 1 < n_steps)
    def _():
        rows(step + 1, (step + 1) % 2, True)

    slot = step % 2
    rows(step, slot, False)
    rf = rf_ref[...]
    x = x2_ref[...] + rf[:, 0:1] * _load_rows(g0.at[slot], tm) + rf[:, 1:2] * _load_rows(g1.at[slot], tm)
    x3_ref[...] = x
    _proj_body(x, *body_refs)


def _proj_body(x, g_ref, w1_ref, qn_ref, kvn_ref, wuq_ref, wukv_ref,
               cm_ref, sm_ref, cs_ref, ss_ref, cd_ref, sd_ref,
               qm_ref, km_ref, vm_ref, qs_ref, ks_ref, vs_ref,
               qd_ref, kd_ref, vd_ref):
    lane = lax.broadcasted_iota(jnp.int32, (PROJ_SUB, LANES), 1)
    first_m = (lane >= 64) & (lane < 80)
    first_s = (lane % 64) < 32
    first_d = (lane % 32) < 16
    for sb in range(x.shape[0] // PROJ_SUB):
        rs = slice(sb * PROJ_SUB, (sb + 1) * PROJ_SUB)
        h = _rms(x[rs], g_ref[...]).astype(jnp.bfloat16)
        u = jnp.dot(h, w1_ref[...], preferred_element_type=jnp.float32)
        cm, sm = cm_ref[rs, :], sm_ref[rs, :]
        cs, ss = cs_ref[rs, :], ss_ref[rs, :]
        cd, sd = cd_ref[rs, :], sd_ref[rs, :]

        cq = _rms(u[:, SEG_CQ:SEG_CQ + MLA_Q_RANK], qn_ref[...]).astype(jnp.bfloat16)
        q = jnp.dot(cq, wuq_ref[...], preferred_element_type=jnp.float32)
        ckv = _rms(u[:, SEG_CKV:SEG_CKV + MLA_KV_RANK], kvn_ref[...]).astype(jnp.bfloat16)
        kv = jnp.dot(ckv, wukv_ref[...], preferred_element_type=jnp.float32)
        kr = _rope(u[:, SEG_KR:SEG_KR + LANES], cm, sm, first_m, 16)
        for hd in range(MLA_HEADS):
            sl = slice(hd * LANES, (hd + 1) * LANES)
            qm_ref[rs, sl] = _rope(q[:, sl], cm, sm, first_m, 16).astype(jnp.bfloat16)
            km_ref[rs, sl] = (kv[:, sl] + kr).astype(jnp.bfloat16)
        vm_ref[rs, :] = kv[:, MLA_HEADS * LANES:].astype(jnp.bfloat16)

        for j in range(SWA_HEADS // 2):
            sl = slice(SEG_SQ + j * LANES, SEG_SQ + (j + 1) * LANES)
            qs_ref[rs, j * LANES:(j + 1) * LANES] = _rope(u[:, sl], cs, ss, first_s, 32).astype(jnp.bfloat16)
        ks_ref[rs, :] = _rope(u[:, SEG_SK:SEG_SK + LANES], cs, ss, first_s, 32).astype(jnp.bfloat16)
        vs_ref[rs, :] = u[:, SEG_SV:SEG_SV + LANES].astype(jnp.bfloat16)

        for j in range(2):
            sq = slice(SEG_DQ + j * LANES, SEG_DQ + (j + 1) * LANES)
            sk = slice(SEG_DK + j * LANES, SEG_DK + (j + 1) * LANES)
            qd_ref[rs, j * LANES:(j + 1) * LANES] = _rope(u[:, sq], cd, sd, first_d, 16).astype(jnp.bfloat16)
            kd_ref[rs, j * LANES:(j + 1) * LANES] = _rope(u[:, sk], cd, sd, first_d, 16).astype(jnp.bfloat16)
        vd_ref[rs, :] = u[:, SEG_DV:SEG_DV + 2 * LANES].astype(jnp.bfloat16)


def _proj_call(x, g, w1, qn, kvn, wuq, wukv, tables, combine=None):
    T = x.shape[0]
    tm = PROJ_TILE
    n_seq = SEQ // tm
    tok = lambda i, *_: (i, 0)
    const = lambda i, *_: (0, 0)
    tab = lambda i, *_: (i % n_seq, 0)
    widths = (6 * LANES, 6 * LANES, 3 * LANES, 3 * LANES, LANES, LANES,
              2 * LANES, 2 * LANES, 2 * LANES)
    weight_specs = [
        pl.BlockSpec((1, D_MODEL), const),
        pl.BlockSpec((D_MODEL, PROJ_WIDTH), const),
        pl.BlockSpec((1, MLA_Q_RANK), const),
        pl.BlockSpec((1, MLA_KV_RANK), const),
        pl.BlockSpec((MLA_Q_RANK, 6 * LANES), const),
        pl.BlockSpec((MLA_KV_RANK, 9 * LANES), const),
    ] + [pl.BlockSpec((tm, LANES), tab)] * 6
    out_specs = [pl.BlockSpec((tm, w), tok) for w in widths]
    out_shape = [jax.ShapeDtypeStruct((T, w), jnp.bfloat16) for w in widths]
    params = pltpu.CompilerParams(dimension_semantics=("arbitrary",), vmem_limit_bytes=VMEM_LIMIT)
    weights = (g, w1, qn, kvn, wuq, wukv, *tables)
    if combine is None:
        return pl.pallas_call(
            _proj_kernel,
            grid=(T // tm,),
            in_specs=[pl.BlockSpec((tm, D_MODEL), tok)] + weight_specs,
            out_specs=out_specs, out_shape=out_shape, compiler_params=params, name="proj",
        )(x, *weights)
    pos, rf, ys = combine
    return pl.pallas_call(
        _proj_combine_kernel,
        grid_spec=pltpu.PrefetchScalarGridSpec(
            num_scalar_prefetch=1,
            grid=(T // tm,),
            in_specs=[pl.BlockSpec((tm, D_MODEL), tok), pl.BlockSpec((tm, 2), tok),
                      pl.BlockSpec(memory_space=pl.ANY)] + weight_specs,
            out_specs=out_specs + [pl.BlockSpec((tm, D_MODEL), tok)],
            scratch_shapes=[pltpu.VMEM((2, tm * ROW_TILES, LANES), jnp.float32),
                            pltpu.VMEM((2, tm * ROW_TILES, LANES), jnp.float32),
                            pltpu.SemaphoreType.DMA((2,))],
        ),
        out_shape=out_shape + [jax.ShapeDtypeStruct((T, D_MODEL), jnp.float32)],
        compiler_params=params, name="proj_combine",
    )(pos, x, rf, ys, *weights)


def _fill_values(vx_ref, v_ref):
    lane = lax.broadcasted_iota(jnp.int32, v_ref.shape, 1)
    vx_ref[:, :LANES] = v_ref[...]
    vx_ref[:, LANES:] = jnp.where(lane == 0, 1.0, 0.0).astype(vx_ref.dtype)


def _softmax_pv(q, k, vx):
    s = lax.dot_general(q, k, _NT, preferred_element_type=jnp.float32)
    m = jnp.max(s, axis=-1, keepdims=True)
    p = jnp.exp2(s - m)
    o = jnp.dot(p.astype(jnp.bfloat16), vx, preferred_element_type=jnp.float32)
    return o[:, :LANES] / o[:, LANES:LANES + 1]


def _mla_kernel(q_ref, k_ref, v_ref, o_ref, vx_ref):
    _fill_values(vx_ref, v_ref)
    lane = lax.broadcasted_iota(jnp.int32, (ATTN_SUB, LANES), 1)
    for sb in range(ATTN_Q_TILE // ATTN_SUB):
        rows = slice(sb * ATTN_SUB, (sb + 1) * ATTN_SUB)
        o0 = _softmax_pv(q_ref[rows, :LANES], k_ref[:, :LANES], vx_ref[...])
        o1 = _softmax_pv(q_ref[rows, LANES:], k_ref[:, LANES:], vx_ref[...])
        o_ref[rows, :] = jnp.where(lane < MLA_V_DIM, o0, o1).astype(o_ref.dtype)


def _mla_call(q, k, v, batch):
    T = q.shape[0]
    tq = ATTN_Q_TILE
    nq = SEQ // tq
    return pl.pallas_call(
        _mla_kernel,
        grid=(batch, MLA_HEADS // 2, nq),
        in_specs=[
            pl.BlockSpec((tq, 2 * LANES), lambda b, p, i: (b * nq + i, p)),
            pl.BlockSpec((SEQ, 2 * LANES), lambda b, p, i: (b, p)),
            pl.BlockSpec((SEQ, LANES), lambda b, p, i: (b, p)),
        ],
        out_specs=pl.BlockSpec((tq, LANES), lambda b, p, i: (b * nq + i, p)),
        out_shape=jax.ShapeDtypeStruct((T, MLA_HEADS * MLA_V_DIM), jnp.bfloat16),
        scratch_shapes=[pltpu.VMEM((SEQ, 2 * LANES), jnp.bfloat16)],
        compiler_params=pltpu.CompilerParams(
            dimension_semantics=("arbitrary",) * 3, vmem_limit_bytes=VMEM_LIMIT),
        name="mla_attn",
    )(q, k, v)


def _swa_kernel(sink_ref, q_ref, k_ref, v_ref, o_ref):
    i = pl.program_id(1)
    nb = SEQ // WINDOW
    half = SWA_HEADS // 2
    rows = SWA_HEADS * WINDOW
    lane = lax.broadcasted_iota(jnp.int32, (WINDOW, LANES), 1)
    left = lane < HEAD_DIM
    row = lax.broadcasted_iota(jnp.int32, (rows, 3 * WINDOW), 0)
    col = lax.broadcasted_iota(jnp.int32, (rows, 3 * WINDOW), 1)
    rel = col - row % WINDOW
    band = (rel >= 0) & (rel <= 2 * WINDOW)
    head_col = lax.broadcasted_iota(jnp.int32, (rows, 1), 0) // WINDOW
    sink = jnp.zeros((rows, 1), jnp.float32)
    for hd in range(SWA_HEADS):
        sink = jnp.where(head_col == hd, sink_ref[hd], sink)
    for sub in range(SWA_Q_TILE // WINDOW):
        n = i * (SWA_Q_TILE // WINDOW) + sub
        prev = pl.multiple_of(jnp.maximum(n - 1, 0) * WINDOW, WINDOW)
        cur = pl.multiple_of(n * WINDOW, WINDOW)
        nxt = pl.multiple_of(jnp.minimum(n + 1, nb - 1) * WINDOW, WINDOW)
        kb = jnp.concatenate([k_ref[pl.ds(prev, WINDOW), :], k_ref[pl.ds(cur, WINDOW), :],
                              k_ref[pl.ds(nxt, WINDOW), :]], axis=0)
        vb = jnp.concatenate([v_ref[pl.ds(prev, WINDOW), :], v_ref[pl.ds(cur, WINDOW), :],
                              v_ref[pl.ds(nxt, WINDOW), :]], axis=0)
        lo_col = jnp.where(n > 0, 0, WINDOW)
        hi_col = jnp.where(n < nb - 1, 3 * WINDOW, 2 * WINDOW)
        valid = band & (col >= lo_col) & (col < hi_col)
        parts = [q_ref[sub * WINDOW:(sub + 1) * WINDOW, j * LANES:(j + 1) * LANES] for j in range(half)]
        zero = jnp.zeros_like(parts[0])
        qs = jnp.concatenate([jnp.where(left, p_, zero) for p_ in parts]
                             + [jnp.where(left, zero, p_) for p_ in parts], axis=0)
        s = lax.dot_general(qs, kb, _NT, preferred_element_type=jnp.float32)
        s = jnp.where(valid, s, -jnp.inf)
        m = jnp.maximum(jnp.max(s, axis=-1, keepdims=True), sink)
        p = jnp.exp2(s - m)
        l = jnp.sum(p, axis=-1, keepdims=True) + jnp.exp2(sink - m)
        o = jnp.dot(p.astype(jnp.bfloat16), vb, preferred_element_type=jnp.float32) / l
        for j in range(half):
            o_ref[sub * WINDOW:(sub + 1) * WINDOW, j * LANES:(j + 1) * LANES] = jnp.where(
                left, o[j * WINDOW:(j + 1) * WINDOW], o[(j + half) * WINDOW:(j + half + 1) * WINDOW]
            ).astype(o_ref.dtype)


def _swa_call(sink, q, k, v, batch):
    T = q.shape[0]
    tq = SWA_Q_TILE
    nq = SEQ // tq
    return pl.pallas_call(
        _swa_kernel,
        grid=(batch, nq),
        in_specs=[
            pl.BlockSpec(memory_space=pltpu.SMEM),
            pl.BlockSpec((tq, 3 * LANES), lambda b, i: (b * nq + i, 0)),
            pl.BlockSpec((SEQ, LANES), lambda b, i: (b, 0)),
            pl.BlockSpec((SEQ, LANES), lambda b, i: (b, 0)),
        ],
        out_specs=pl.BlockSpec((tq, 3 * LANES), lambda b, i: (b * nq + i, 0)),
        out_shape=jax.ShapeDtypeStruct((T, SWA_HEADS * HEAD_DIM), jnp.bfloat16),
        compiler_params=pltpu.CompilerParams(
            dimension_semantics=("arbitrary",) * 2, vmem_limit_bytes=VMEM_LIMIT),
        name="swa_attn",
    )(sink, q, k, v)


def _diff_kernel(lam_init, lq1_ref, lk1_ref, lq2_ref, lk2_ref, subln_ref,
                 q_ref, k_ref, v_ref, o_ref, vx_ref):
    lam = (jnp.exp(jnp.sum(lq1_ref[...] * lk1_ref[...], axis=-1, keepdims=True))
           - jnp.exp(jnp.sum(lq2_ref[...] * lk2_ref[...], axis=-1, keepdims=True))
           + lam_init)
    _fill_values(vx_ref, v_ref)
    lane = lax.broadcasted_iota(jnp.int32, (ATTN_SUB, LANES), 1)
    left = lane < DIFF_V_DIM
    for sb in range(DIFF_Q_TILE // ATTN_SUB):
        rows = slice(sb * ATTN_SUB, (sb + 1) * ATTN_SUB)
        q = q_ref[rows, :]
        zero = jnp.zeros_like(q)
        outs = []
        for side in range(2):
            comp = []
            for c in range(2):
                lo = side * DIFF_V_DIM + c * DIFF_QK_DIM
                qc = jnp.where((lane >= lo) & (lane < lo + DIFF_QK_DIM), q, zero)
                comp.append(_softmax_pv(qc, k_ref[...], vx_ref[...]))
            outs.append(comp[0] - lam * comp[1])
        o = jnp.where(left, outs[0], outs[1])
        sq = o * o
        ms_l = jnp.sum(jnp.where(left, sq, 0.0), axis=-1, keepdims=True) * (1.0 / DIFF_V_DIM)
        ms_r = jnp.sum(jnp.where(left, 0.0, sq), axis=-1, keepdims=True) * (1.0 / DIFF_V_DIM)
        r = jnp.where(left, lax.rsqrt(ms_l + NORM_EPS), lax.rsqrt(ms_r + NORM_EPS))
        o_ref[rows, :] = (o * r * subln_ref[...] * (1.0 - lam_init)).astype(o_ref.dtype)


def _diff_call(lam_init, lq1, lk1, lq2, lk2, subln2, q, k, v, batch):
    T = q.shape[0]
    tq = DIFF_Q_TILE
    nq = SEQ // tq
    small = lambda b, p, i: (0, 0)
    return pl.pallas_call(
        functools.partial(_diff_kernel, lam_init),
        grid=(batch, DIFF_HEADS // 2, nq),
        in_specs=[pl.BlockSpec((1, DIFF_QK_DIM), small)] * 4 + [
            pl.BlockSpec((1, LANES), small),
            pl.BlockSpec((tq, LANES), lambda b, p, i: (b * nq + i, p)),
            pl.BlockSpec((SEQ, LANES), lambda b, p, i: (b, p)),
            pl.BlockSpec((SEQ, LANES), lambda b, p, i: (b, p)),
        ],
        out_specs=pl.BlockSpec((tq, LANES), lambda b, p, i: (b * nq + i, p)),
        out_shape=jax.ShapeDtypeStruct((T, DIFF_HEADS * DIFF_V_DIM), jnp.bfloat16),
        scratch_shapes=[pltpu.VMEM((SEQ, 2 * LANES), jnp.bfloat16)],
        compiler_params=pltpu.CompilerParams(
            dimension_semantics=("arbitrary",) * 3, vmem_limit_bytes=VMEM_LIMIT),
        name="diff_attn",
    )(lq1, lk1, lq2, lk2, subln2, q, k, v)


def _outproj_kernel(x_ref, om_ref, os_ref, od_ref, wm_ref, ws_ref, wd_ref, g_ref,
                    wr2_ref, br_ref, x2_ref, h2_ref, lt_ref):
    x2 = (x_ref[...]
          + jnp.dot(om_ref[...], wm_ref[...], preferred_element_type=jnp.float32)
          + jnp.dot(os_ref[...], ws_ref[...], preferred_element_type=jnp.float32)
          + jnp.dot(od_ref[...], wd_ref[...], preferred_element_type=jnp.float32))
    x2_ref[...] = x2
    h2 = _rms(x2, g_ref[...])
    _store_rows(h2_ref, h2)

    hi = h2.astype(jnp.bfloat16)
    lo = (h2 - hi.astype(jnp.float32)).astype(jnp.bfloat16)
    a = jnp.dot(hi, wr2_ref[...], preferred_element_type=jnp.float32)
    b = jnp.dot(lo, wr2_ref[:, :LANES], preferred_element_type=jnp.float32)
    logits = a[:, :LANES] + a[:, LANES:] + b + br_ref[...]
    lt_ref[...] = logits.T


def _outproj_call(x, om, os_, od, wm, ws, wd, g, wr2, br):
    T = x.shape[0]
    tm = PROJ_TILE
    tok = lambda i: (i, 0)
    const = lambda i: (0, 0)
    return pl.pallas_call(
        _outproj_kernel,
        grid=(T // tm,),
        in_specs=[
            pl.BlockSpec((tm, D_MODEL), tok),
            pl.BlockSpec((tm, 3 * LANES), tok),
            pl.BlockSpec((tm, 3 * LANES), tok),
            pl.BlockSpec((tm, 2 * LANES), tok),
            pl.BlockSpec((3 * LANES, D_MODEL), const),
            pl.BlockSpec((3 * LANES, D_MODEL), const),
            pl.BlockSpec((2 * LANES, D_MODEL), const),
            pl.BlockSpec((1, D_MODEL), const),
            pl.BlockSpec((D_MODEL, 2 * LANES), const),
            pl.BlockSpec((1, LANES), const),
        ],
        out_specs=[
            pl.BlockSpec((tm, D_MODEL), tok),
            pl.BlockSpec((tm * ROW_TILES, LANES), tok),
            pl.BlockSpec((LANES, tm), lambda i: (0, i)),
        ],
        out_shape=[
            jax.ShapeDtypeStruct((T, D_MODEL), jnp.float32),
            jax.ShapeDtypeStruct((T * ROW_TILES, LANES), jnp.float32),
            jax.ShapeDtypeStruct((LANES, T), jnp.float32),
        ],
        compiler_params=pltpu.CompilerParams(
            dimension_semantics=("arbitrary",), vmem_limit_bytes=VMEM_LIMIT),
        name="outproj",
    )(x, om, os_, od, wm, ws, wd, g, wr2, br)


def _route_kernel(lt_ref, ri_ref, rf_ref, cnt_ref, carry_ref):
    step = pl.program_id(0)

    @pl.when(step == 0)
    def _():
        carry_ref[...] = jnp.zeros_like(carry_ref)

    tc = lt_ref.shape[1]
    neg = -jnp.inf
    row8 = lax.broadcasted_iota(jnp.int32, (8, tc), 0)
    is_g = row8 < N_GROUPS
    gl = jnp.where(is_g, lt_ref[0:8, :], neg)
    gmax = jnp.max(gl, axis=0, keepdims=True)
    gidx = jnp.min(jnp.where(gl == gmax, row8, 8), axis=0, keepdims=True)
    gsum = jnp.sum(jnp.where(is_g, jnp.exp(gl - gmax), 0.0), axis=0, keepdims=True)
    g_gate = 1.0 / gsum
    gidx8 = jnp.broadcast_to(gidx, (8, tc))
    e_in = lt_ref[EXPERT_LANE0:EXPERT_LANE0 + 8, :]
    for g in range(1, N_GROUPS):
        lo = EXPERT_LANE0 + g * EXPERTS_PER_GROUP
        e_in = jnp.where(gidx8 == g, lt_ref[lo:lo + EXPERTS_PER_GROUP, :], e_in)
    m1 = jnp.max(e_in, axis=0, keepdims=True)
    i1 = jnp.min(jnp.where(e_in == m1, row8, 8), axis=0, keepdims=True)
    e_rest = jnp.where(row8 == i1, neg, e_in)
    m2 = jnp.max(e_rest, axis=0, keepdims=True)
    i2 = jnp.min(jnp.where(e_rest == m2, row8, 8), axis=0, keepdims=True)
    t = jnp.exp(m2 - m1)
    w1 = g_gate / (1.0 + t)
    w2 = g_gate * t / (1.0 + t)
    e1 = gidx * EXPERTS_PER_GROUP + i1
    e2 = gidx * EXPERTS_PER_GROUP + i2

    rowe = lax.broadcasted_iota(jnp.int32, (N_EXPERTS, tc), 0)
    hit1 = rowe == e1
    hit2 = rowe == e2
    onehot = jnp.where(hit1 | hit2, 1.0, 0.0)
    r = lax.broadcasted_iota(jnp.int32, (RANK_CHUNK, RANK_CHUNK), 0)
    c = lax.broadcasted_iota(jnp.int32, (RANK_CHUNK, RANK_CHUNK), 1)
    upper = jnp.where(r < c, 1.0, 0.0).astype(jnp.bfloat16)
    carry = carry_ref[:, 0:1]
    parts = []
    for j in range(tc // RANK_CHUNK):
        oh = onehot[:, j * RANK_CHUNK:(j + 1) * RANK_CHUNK]
        parts.append(jnp.dot(oh.astype(jnp.bfloat16), upper, preferred_element_type=jnp.float32) + carry)
        carry = carry + jnp.sum(oh, axis=1, keepdims=True)
    before = jnp.concatenate(parts, axis=1)
    rank1 = jnp.sum(jnp.where(hit1, before, 0.0), axis=0, keepdims=True).astype(jnp.int32)
    rank2 = jnp.sum(jnp.where(hit2, before, 0.0), axis=0, keepdims=True).astype(jnp.int32)
    carry_ref[...] = jnp.broadcast_to(carry, carry_ref.shape)
    cnt_ref[...] = jnp.broadcast_to(carry, cnt_ref.shape)

    ri_ref[...] = jnp.where(row8 == 0, e1, jnp.where(row8 == 1, e2, jnp.where(
        row8 == 2, rank1, jnp.where(row8 == 3, rank2, 0))))
    rf_ref[...] = jnp.where(row8 == 0, w1, jnp.where(row8 == 1, w2, 0.0))


def _route_call(lt):
    T = lt.shape[1]
    tc = ROUTE_CHUNK
    return pl.pallas_call(
        _route_kernel,
        grid=(T // tc,),
        in_specs=[pl.BlockSpec((LANES, tc), lambda i: (0, i))],
        out_specs=[
            pl.BlockSpec((8, tc), lambda i: (0, i)),
            pl.BlockSpec((8, tc), lambda i: (0, i)),
            pl.BlockSpec((N_EXPERTS, LANES), lambda i: (0, 0)),
        ],
        out_shape=[
            jax.ShapeDtypeStruct((8, T), jnp.int32),
            jax.ShapeDtypeStruct((8, T), jnp.float32),
            jax.ShapeDtypeStruct((N_EXPERTS, LANES), jnp.float32),
        ],
        scratch_shapes=[pltpu.VMEM((N_EXPERTS, LANES), jnp.float32)],
        compiler_params=pltpu.CompilerParams(
            dimension_semantics=("arbitrary",), vmem_limit_bytes=VMEM_LIMIT),
        name="route",
    )(lt)


def _plan_kernel(ri_ref, cnt_ref, pos_ref, te_ref, meta_ref):
    T = ri_ref.shape[1]
    counts = cnt_ref[...].astype(jnp.int32)
    padded = (counts + (MOE_TILE - 1)) // MOE_TILE * MOE_TILE
    r = lax.broadcasted_iota(jnp.int32, (N_EXPERTS, N_EXPERTS), 0)
    c = lax.broadcasted_iota(jnp.int32, (N_EXPERTS, N_EXPERTS), 1)
    lower = jnp.where(c < r, 1.0, 0.0)
    off = jnp.dot(lower, padded.astype(jnp.float32), preferred_element_type=jnp.float32,
                  precision=lax.Precision.HIGHEST).astype(jnp.int32)
    ends = off + padded
    total = jnp.max(ends, axis=0, keepdims=True)

    rowe = lax.broadcasted_iota(jnp.int32, (N_EXPERTS, T), 0)
    off_col = off[:, 0:1]
    row8 = lax.broadcasted_iota(jnp.int32, (8, T), 0)
    pos1 = jnp.sum(jnp.where(rowe == ri_ref[0:1, :], off_col, 0), axis=0, keepdims=True) + ri_ref[2:3, :]
    pos2 = jnp.sum(jnp.where(rowe == ri_ref[1:2, :], off_col, 0), axis=0, keepdims=True) + ri_ref[3:4, :]
    pos_ref[...] = jnp.where(row8 == 0, pos1, jnp.where(row8 == 1, pos2, 0))

    n_lanes = te_ref.shape[1]
    start = lax.broadcasted_iota(jnp.int32, (N_EXPERTS, n_lanes), 1) * MOE_TILE
    start = jnp.minimum(start, total[:, 0:1] - MOE_TILE)
    te = jnp.sum((ends[:, 0:1] <= start).astype(jnp.int32), axis=0, keepdims=True)
    te_ref[...] = jnp.broadcast_to(te, te_ref.shape)
    lane = lax.broadcasted_iota(jnp.int32, (N_EXPERTS, LANES), 1)
    fill = jnp.where(padded > 0, ends - MOE_TILE, -1)
    meta_ref[...] = jnp.where(lane == 0, fill, total // MOE_TILE)


def _plan_call(ri, cnt, max_tiles):
    T = ri.shape[1]
    n_lanes = -(-max_tiles // LANES) * LANES
    full = lambda i: (0, 0)
    return pl.pallas_call(
        _plan_kernel,
        grid=(1,),
        in_specs=[pl.BlockSpec((8, T), full), pl.BlockSpec((N_EXPERTS, LANES), full)],
        out_specs=[pl.BlockSpec((8, T), full), pl.BlockSpec((8, n_lanes), full),
                   pl.BlockSpec((N_EXPERTS, LANES), full)],
        out_shape=[jax.ShapeDtypeStruct((8, T), jnp.int32),
                   jax.ShapeDtypeStruct((8, n_lanes), jnp.int32),
                   jax.ShapeDtypeStruct((N_EXPERTS, LANES), jnp.int32)],
        compiler_params=pltpu.CompilerParams(
            dimension_semantics=("arbitrary",), vmem_limit_bytes=VMEM_LIMIT),
        name="slot_plan",
    )(ri, cnt)


def _push_kernel(pos_ref, fill_ref, nt_ref, h_ref, xs_ref, zero_ref, sem):
    step = pl.program_id(0)
    tm = h_ref.shape[0] // ROW_TILES
    n_tok = pl.num_programs(0) * tm
    fill_rows = MOE_TILE * ROW_TILES
    max_tiles = xs_ref.shape[0] // fill_rows

    def fill_copy(slot):
        start = pl.multiple_of(slot * ROW_TILES, fill_rows)
        return pltpu.make_async_copy(zero_ref, xs_ref.at[pl.ds(start, fill_rows), :], sem)

    @pl.when(step == 0)
    def _():
        zero_ref[...] = jnp.zeros_like(zero_ref)
        for e in range(N_EXPERTS):
            @pl.when(fill_ref[e] >= 0)
            def _():
                fill_copy(fill_ref[e]).start()

        def tail_start(j, carry):
            fill_copy(j * MOE_TILE).start()
            return carry

        def tail_wait(j, carry):
            fill_copy(j * MOE_TILE).wait()
            return carry

        lax.fori_loop(nt_ref[0], max_tiles, tail_start, 0)
        for e in range(N_EXPERTS):
            @pl.when(fill_ref[e] >= 0)
            def _():
                fill_copy(fill_ref[e]).wait()
        lax.fori_loop(nt_ref[0], max_tiles, tail_wait, 0)

    def issue(r, carry):
        t = step * tm + r
        for k in range(2):
            pltpu.make_async_copy(_row(h_ref, r), _row(xs_ref, pos_ref[k * n_tok + t]), sem).start(priority=k)
        return carry

    lax.fori_loop(0, tm, issue, 0, unroll=8)

    def drain(r, carry):
        for k in range(2):
            pltpu.make_async_copy(_row(h_ref, 0), _row(xs_ref, 0), sem).wait()
        return carry

    lax.fori_loop(0, tm, drain, 0, unroll=8)


def _push_call(pos, fill, n_tiles, h2, n_slots):
    tm = MOVE_TILE
    n_tok = h2.shape[0] // ROW_TILES
    return pl.pallas_call(
        _push_kernel,
        grid_spec=pltpu.PrefetchScalarGridSpec(
            num_scalar_prefetch=3,
            grid=(n_tok // tm,),
            in_specs=[pl.BlockSpec((tm * ROW_TILES, LANES), lambda i, pos, fill, nt: (i, 0))],
            out_specs=pl.BlockSpec(memory_space=pl.ANY),
            scratch_shapes=[pltpu.VMEM((MOE_TILE * ROW_TILES, LANES), jnp.float32),
                            pltpu.SemaphoreType.DMA(())],
        ),
        out_shape=jax.ShapeDtypeStruct((n_slots * ROW_TILES, LANES), jnp.float32),
        compiler_params=pltpu.CompilerParams(
            dimension_semantics=("arbitrary",), vmem_limit_bytes=VMEM_LIMIT),
        name="moe_push",
    )(pos, fill, n_tiles, h2)


def _expert_kernel(te_ref, nt_ref, xs_ref, wg_ref, wu_ref, wd_ref, ys_ref,
                   ring, obuf, wg_s, wu_s, wd_s, sem, osem):
    j = pl.program_id(0)
    nt = nt_ref[0]
    rows = MOE_TILE * ROW_TILES

    def tile_copy(tile):
        slot = tile % EXPERT_RING
        return pltpu.make_async_copy(xs_ref.at[pl.ds(pl.multiple_of(tile * rows, rows), rows), :],
                                     ring.at[slot], sem.at[slot])

    @pl.when(j == 0)
    def _():
        tile_copy(0).start()

        @pl.when(nt > 1)
        def _():
            tile_copy(1).start()

    @pl.when(j + 2 < nt)
    def _():
        tile_copy(j + 2).start()

    def out_copy(tile):
        slot = tile % 2
        return pltpu.make_async_copy(obuf.at[slot],
                                     ys_ref.at[pl.ds(pl.multiple_of(tile * rows, rows), rows), :],
                                     osem.at[slot])

    @pl.when(j < nt)
    def _():
        tile_copy(j).wait()

        @pl.when(j >= 2)
        def _():
            out_copy(j - 2).wait()

        changed = (j == 0) | (te_ref[j] != te_ref[jnp.maximum(j - 1, 0)])

        @pl.when(changed)
        def _():
            wg_s[...] = wg_ref[...].astype(jnp.bfloat16)
            wu_s[...] = wu_ref[...].astype(jnp.bfloat16)
            wd_s[...] = wd_ref[...].astype(jnp.bfloat16)

        xb = _load_rows(ring.at[j % EXPERT_RING], MOE_TILE).astype(jnp.bfloat16)
        a = jnp.dot(xb, wg_s[...], preferred_element_type=jnp.float32)
        u = jnp.dot(xb, wu_s[...], preferred_element_type=jnp.float32)
        hid = (a * jax.nn.sigmoid(a) * u).astype(jnp.bfloat16)
        _store_rows(obuf.at[j % 2], jnp.dot(hid, wd_s[...], preferred_element_type=jnp.float32))
        out_copy(j).start()

        @pl.when(j == nt - 1)
        def _():
            out_copy(j).wait()

            @pl.when(j >= 1)
            def _():
                out_copy(j - 1).wait()


def _expert_call(tile_expert, n_tiles, xs, wg, wu, wd, layer):
    n_slots = xs.shape[0] // ROW_TILES
    max_tiles = n_slots // MOE_TILE
    base = layer * N_EXPERTS
    wsel = lambda j, te, nt: (base + te[j], 0, 0)
    return pl.pallas_call(
        _expert_kernel,
        grid_spec=pltpu.PrefetchScalarGridSpec(
            num_scalar_prefetch=2,
            grid=(max_tiles,),
            in_specs=[
                pl.BlockSpec(memory_space=pl.ANY),
                pl.BlockSpec((None, D_MODEL, EXPERT_FF), wsel),
                pl.BlockSpec((None, D_MODEL, EXPERT_FF), wsel),
                pl.BlockSpec((None, EXPERT_FF, D_MODEL), wsel),
            ],
            out_specs=pl.BlockSpec(memory_space=pl.ANY),
            scratch_shapes=[pltpu.VMEM((EXPERT_RING, MOE_TILE * ROW_TILES, LANES), jnp.float32),
                            pltpu.VMEM((2, MOE_TILE * ROW_TILES, LANES), jnp.float32),
                            pltpu.VMEM((D_MODEL, EXPERT_FF), jnp.bfloat16),
                            pltpu.VMEM((D_MODEL, EXPERT_FF), jnp.bfloat16),
                            pltpu.VMEM((EXPERT_FF, D_MODEL), jnp.bfloat16),
                            pltpu.SemaphoreType.DMA((EXPERT_RING,)),
                            pltpu.SemaphoreType.DMA((2,))],
        ),
        out_shape=jax.ShapeDtypeStruct((n_slots * ROW_TILES, LANES), jnp.float32),
        input_output_aliases={2: 0},
        compiler_params=pltpu.CompilerParams(
            dimension_semantics=("arbitrary",), vmem_limit_bytes=VMEM_LIMIT),
        name="moe_experts",
    )(tile_expert, n_tiles, xs, wg, wu, wd)


def _combine_kernel(pos_ref, x2_ref, rf_ref, g_ref, ys_ref, o_ref, g0, g1, sem):
    step = pl.program_id(0)
    tm = x2_ref.shape[0]
    n_tok = pl.num_programs(0) * tm

    def issue(r, carry):
        t = step * tm + r
        pltpu.make_async_copy(_row(ys_ref, pos_ref[t]), _row(g0, r), sem).start(priority=0)
        pltpu.make_async_copy(_row(ys_ref, pos_ref[n_tok + t]), _row(g1, r), sem).start(priority=1)
        return carry

    lax.fori_loop(0, tm, issue, 0, unroll=8)

    def drain(r, carry):
        for k in range(2):
            pltpu.make_async_copy(_row(ys_ref, 0), _row(g0, 0), sem).wait()
        return carry

    lax.fori_loop(0, tm, drain, 0, unroll=8)

    rf = rf_ref[...]
    y = x2_ref[...] + rf[:, 0:1] * _load_rows(g0, tm) + rf[:, 1:2] * _load_rows(g1, tm)
    o_ref[...] = _rms(y, g_ref[...])


def _combine_call(pos, x2, rf, g, ys):
    T = x2.shape[0]
    tm = MOVE_TILE
    return pl.pallas_call(
        _combine_kernel,
        grid_spec=pltpu.PrefetchScalarGridSpec(
            num_scalar_prefetch=1,
            grid=(T // tm,),
            in_specs=[
                pl.BlockSpec((tm, D_MODEL), lambda i, pos: (i, 0)),
                pl.BlockSpec((tm, 2), lambda i, pos: (i, 0)),
                pl.BlockSpec((1, D_MODEL), lambda i, pos: (0, 0)),
                pl.BlockSpec(memory_space=pl.ANY),
            ],
            out_specs=pl.BlockSpec((tm, D_MODEL), lambda i, pos: (i, 0)),
            scratch_shapes=[pltpu.VMEM((tm * ROW_TILES, LANES), jnp.float32),
                            pltpu.VMEM((tm * ROW_TILES, LANES), jnp.float32),
                            pltpu.SemaphoreType.DMA(())],
        ),
        out_shape=jax.ShapeDtypeStruct((T, D_MODEL), jnp.float32),
        compiler_params=pltpu.CompilerParams(
            dimension_semantics=("arbitrary",), vmem_limit_bytes=VMEM_LIMIT),
        name="moe_combine",
    )(pos, x2, rf, g, ys)


def _rope_tables():
    def cos_sin(dim):
        inv = 1.0 / (ROPE_THETA ** (jnp.arange(0, dim, 2, dtype=jnp.float32) / dim))
        ang = jnp.arange(SEQ, dtype=jnp.float32)[:, None] * inv[None, :]
        return jnp.cos(ang), jnp.sin(ang)

    c16, s16 = cos_sin(MLA_ROPE_DIM)
    c32, s32 = cos_sin(HEAD_DIM)
    ones = jnp.ones((SEQ, 64), jnp.float32)
    zeros = jnp.zeros((SEQ, 64), jnp.float32)
    pad = jnp.zeros((SEQ, 32), jnp.float32)
    cm = jnp.concatenate([ones, c16, c16, pad], axis=1)
    sm = jnp.concatenate([zeros, -s16, s16, pad], axis=1)
    cs = jnp.tile(jnp.concatenate([c32, c32], axis=1), (1, 2))
    ss = jnp.tile(jnp.concatenate([-s32, s32], axis=1), (1, 2))
    cd = jnp.tile(jnp.concatenate([c16, c16], axis=1), (1, 4))
    sd = jnp.tile(jnp.concatenate([-s16, s16], axis=1), (1, 4))
    return cm, sm, cs, ss, cd, sd


def _layer_weights(w_in, w_uq, w_ukv, w_out):
    f32 = jnp.float32
    cq = w_in[:, 0:384]
    ckv = w_in[:, 384:640]
    kr = w_in[:, 640:672]
    sq = w_in[:, 672:1056] * (LOG2E * HEAD_DIM ** -0.5)
    sk = w_in[:, 1056:1184]
    sv = w_in[:, 1184:1312]
    dq = w_in[:, 1312:1568] * (LOG2E * DIFF_QK_DIM ** -0.5)
    dk = w_in[:, 1568:1824]
    dv = w_in[:, 1824:2080]
    kr_pad = jnp.concatenate([jnp.zeros((D_MODEL, 64), f32), kr, jnp.zeros((D_MODEL, 32), f32)], axis=1)
    half = SWA_HEADS // 2
    sq_h = sq.reshape(D_MODEL, SWA_HEADS, HEAD_DIM)
    sq_perm = jnp.stack([sq_h[:, :half], sq_h[:, half:]], axis=2).reshape(D_MODEL, SWA_HEADS * HEAD_DIM)
    w1 = jnp.concatenate([cq, ckv, kr_pad, sq_perm, sk, sv, dq, dk, dv], axis=1).astype(jnp.bfloat16)

    scale = LOG2E * (MLA_NOPE_DIM + MLA_ROPE_DIM) ** -0.5
    uq = (w_uq * scale).reshape(MLA_Q_RANK, MLA_HEADS, MLA_NOPE_DIM + MLA_ROPE_DIM)
    uq = jnp.concatenate([uq, jnp.zeros((MLA_Q_RANK, MLA_HEADS, 32), f32)], axis=2)
    wuq = uq.reshape(MLA_Q_RANK, MLA_HEADS * LANES).astype(jnp.bfloat16)
    ukv = w_ukv.reshape(MLA_KV_RANK, MLA_HEADS, MLA_NOPE_DIM + MLA_V_DIM)
    uk = jnp.concatenate([ukv[:, :, :MLA_NOPE_DIM], jnp.zeros((MLA_KV_RANK, MLA_HEADS, 64), f32)], axis=2)
    uv = ukv[:, :, MLA_NOPE_DIM:]
    wukv = jnp.concatenate([uk.reshape(MLA_KV_RANK, MLA_HEADS * LANES),
                            uv.reshape(MLA_KV_RANK, MLA_HEADS * MLA_V_DIM)], axis=1).astype(jnp.bfloat16)

    wm = w_out[0:384].astype(jnp.bfloat16)
    wo_s = w_out[384:768].reshape(SWA_HEADS, HEAD_DIM, D_MODEL)
    ws = jnp.stack([wo_s[:half], wo_s[half:]], axis=1).reshape(SWA_HEADS * HEAD_DIM, D_MODEL).astype(jnp.bfloat16)
    wd = w_out[768:1024].astype(jnp.bfloat16)
    return w1, wuq, wukv, wm, ws, wd


def _router_weights(w_rg, b_rg, w_re, b_re):
    f32 = jnp.float32
    wr = jnp.concatenate([w_rg, jnp.zeros((D_MODEL, EXPERT_LANE0 - N_GROUPS), f32), w_re,
                          jnp.zeros((D_MODEL, LANES - EXPERT_LANE0 - N_EXPERTS), f32)], axis=1)
    br = jnp.concatenate([b_rg, jnp.zeros((EXPERT_LANE0 - N_GROUPS,), f32), b_re,
                          jnp.zeros((LANES - EXPERT_LANE0 - N_EXPERTS,), f32)])[None, :]
    hi = wr.astype(jnp.bfloat16)
    lo = (wr - hi.astype(f32)).astype(jnp.bfloat16)
    return jnp.concatenate([hi, lo], axis=1), br


def kernel(x, attn_norm, w_in, mla_q_norm, mla_kv_norm, mla_w_uq, mla_w_ukv, swa_sink, diff_lq1, diff_lk1, diff_lq2, diff_lk2, diff_subln, w_out, ffn_norm, router_group, router_group_bias, router_expert, router_expert_bias, w_gate, w_up, w_down, final_norm):
    batch, seq, d = x.shape
    assert (seq, d) == (SEQ, D_MODEL)
    depth = w_in.shape[0]
    n_tok = batch * seq
    tables = _rope_tables()
    wg = w_gate.reshape(depth * N_EXPERTS, D_MODEL, EXPERT_FF)
    wu = w_up.reshape(depth * N_EXPERTS, D_MODEL, EXPERT_FF)
    wdn = w_down.reshape(depth * N_EXPERTS, EXPERT_FF, D_MODEL)
    xt = x.reshape(n_tok, D_MODEL)
    pending = None
    for l in range(depth):
        w1, wuq, wukv, wm, ws, wd = _layer_weights(w_in[l], mla_w_uq[l], mla_w_ukv[l], w_out[l])
        outs = _proj_call(xt, attn_norm[l][None], w1, mla_q_norm[l][None], mla_kv_norm[l][None],
                          wuq, wukv, tables, combine=pending)
        qm, km, vm, qs, ks, vs, qd, kd, vd = outs[:9]
        if pending is not None:
            xt = outs[9]
        o_mla = _mla_call(qm, km, vm, batch)
        o_swa = _swa_call(swa_sink[l] * LOG2E, qs, ks, vs, batch)
        lam_init = 0.8 - 0.6 * math.exp(-0.3 * l)
        subln2 = jnp.concatenate([diff_subln[l], diff_subln[l]])[None]
        o_diff = _diff_call(lam_init, diff_lq1[l][None], diff_lk1[l][None], diff_lq2[l][None],
                            diff_lk2[l][None], subln2, qd, kd, vd, batch)
        wr2, br = _router_weights(router_group[l], router_group_bias[l],
                                  router_expert[l], router_expert_bias[l])
        x2, h2, lt = _outproj_call(xt, o_mla, o_swa, o_diff, wm, ws, wd, ffn_norm[l][None], wr2, br)
        ri, rf, cnt = _route_call(lt)
        max_tiles = (2 * n_tok) // MOE_TILE + N_EXPERTS
        pos8, te8, meta = _plan_call(ri, cnt, max_tiles)
        pos = pos8[0:2].reshape(2 * n_tok)
        fill, n_tiles = meta[:, 0], meta[0, 1:2]
        xs = _push_call(pos, fill, n_tiles, h2, max_tiles * MOE_TILE)
        ys = _expert_call(te8[0, :max_tiles], n_tiles, xs, wg, wu, wdn, l)
        xt, pending = x2, (pos, rf[0:2].T, ys)
    pos, rf2, ys = pending
    out = _combine_call(pos, xt, rf2, final_norm[None], ys)
    return out.reshape(batch, seq, d)
```

```python
import functools
import math

import jax
import jax.numpy as jnp
import numpy as np
from jax import lax
from jax.experimental import pallas as pl
from jax.experimental.pallas import tpu as pltpu

D_MODEL = 1024
SEQ = 2048
HEAD_DIM = 64
ROPE_THETA = 10000.0
NORM_EPS = 1e-6
MLA_HEADS = 6
MLA_Q_RANK = 384
MLA_KV_RANK = 256
MLA_NOPE_DIM = 64
MLA_ROPE_DIM = 32
MLA_V_DIM = 64
SWA_HEADS = 6
SWA_KV_HEADS = 2
WINDOW = 128
DIFF_HEADS = 4
DIFF_QK_DIM = 32
DIFF_V_DIM = 64
N_GROUPS = 4
EXPERTS_PER_GROUP = 8
N_EXPERTS = N_GROUPS * EXPERTS_PER_GROUP
EXPERT_FF = 256

LANES = 128
ROW_TILES = D_MODEL // LANES
EXPERT_LANE0 = 8
PROJ_TILE = 512
PROJ_SUB = 256
ATTN_Q_TILE = 2048
DIFF_Q_TILE = 2048
ATTN_SUB = 256
SWA_Q_TILE = 512
MOE_TILE = 256
MOVE_TILE = 256
EXPERT_RING = 3
ROUTE_CHUNK = 2048
RANK_CHUNK = 512
VMEM_LIMIT = 56 * 1024 * 1024

SEG_CQ = 0
SEG_CKV = SEG_CQ + MLA_Q_RANK
SEG_KR = SEG_CKV + MLA_KV_RANK
SEG_SQ = SEG_KR + LANES
SEG_SK = SEG_SQ + SWA_HEADS * HEAD_DIM
SEG_SV = SEG_SK + LANES
SEG_DQ = SEG_SV + LANES
SEG_DK = SEG_DQ + 2 * LANES
SEG_DV = SEG_DK + 2 * LANES
PROJ_WIDTH = SEG_DV + 2 * LANES

_NT = (((1,), (1,)), ((), ()))
LOG2E = math.log2(math.e)


def _rms(x, g):
    return x * lax.rsqrt(jnp.mean(x * x, axis=-1, keepdims=True) + NORM_EPS) * g


def _load_rows(ref, n):
    return jnp.concatenate([ref[pl.ds(c, n, stride=ROW_TILES), :] for c in range(ROW_TILES)], axis=1)


def _store_rows(ref, val):
    n = val.shape[0]
    for c in range(ROW_TILES):
        ref[pl.ds(c, n, stride=ROW_TILES), :] = val[:, c * LANES:(c + 1) * LANES]


def _rope(x, cos, sin, first_half, shift):
    up = pltpu.roll(x, LANES - shift, 1)
    dn = pltpu.roll(x, shift, 1)
    return x * cos + jnp.where(first_half, up, dn) * sin


def _proj_kernel(x_ref, *refs):
    _proj_body(x_ref[...], *refs)


def _row(ref, idx):
    return ref.at[pl.ds(pl.multiple_of(idx * ROW_TILES, ROW_TILES), ROW_TILES), :]


def _proj_combine_kernel(pos_ref, x2_ref, rf_ref, ys_ref, *refs):
    *body_refs, x3_ref, g0, g1, sem = refs
    step = pl.program_id(0)
    n_steps = pl.num_programs(0)
    tm = x2_ref.shape[0]
    n_tok = n_steps * tm

    def rows(tile, slot, start):
        def body(r, carry):
            if start:
                t = tile * tm + r
                pltpu.make_async_copy(_row(ys_ref, pos_ref[t]), _row(g0.at[slot], r), sem.at[slot]).start(priority=0)
                pltpu.make_async_copy(_row(ys_ref, pos_ref[n_tok + t]), _row(g1.at[slot], r), sem.at[slot]).start(priority=1)
            else:
                for g in (g0, g1):
                    pltpu.make_async_copy(_row(ys_ref, 0), _row(g.at[slot], 0), sem.at[slot]).wait()
            return carry
        lax.fori_loop(0, tm, body, 0, unroll=8)

    @pl.when(step == 0)
    def _():
        rows(0, 0, True)

    @pl.when(step + 1 < n_steps)
    def _():
        rows(step + 1, (step + 1) % 2, True)

    slot = step % 2
    rows(step, slot, False)
    rf = rf_ref[...]
    x = x2_ref[...] + rf[:, 0:1] * _load_rows(g0.at[slot], tm) + rf[:, 1:2] * _load_rows(g1.at[slot], tm)
    x3_ref[...] = x
    _proj_body(x, *body_refs)


def _proj_body(x, g_ref, w1_ref, qn_ref, kvn_ref, wuq_ref, wukv_ref,
               cm_ref, sm_ref, cs_ref, ss_ref, cd_ref, sd_ref,
               qm_ref, km_ref, vm_ref, qs_ref, ks_ref, vs_ref,
               qd_ref, kd_ref, vd_ref):
    lane = lax.broadcasted_iota(jnp.int32, (PROJ_SUB, LANES), 1)
    first_m = (lane >= 64) & (lane < 80)
    first_s = (lane % 64) < 32
    first_d = (lane % 32) < 16
    for sb in range(x.shape[0] // PROJ_SUB):
        rs = slice(sb * PROJ_SUB, (sb + 1) * PROJ_SUB)
        h = _rms(x[rs], g_ref[...]).astype(jnp.bfloat16)
        u = jnp.dot(h, w1_ref[...], preferred_element_type=jnp.float32)
        cm, sm = cm_ref[rs, :], sm_ref[rs, :]
        cs, ss = cs_ref[rs, :], ss_ref[rs, :]
        cd, sd = cd_ref[rs, :], sd_ref[rs, :]

        cq = _rms(u[:, SEG_CQ:SEG_CQ + MLA_Q_RANK], qn_ref[...]).astype(jnp.bfloat16)
        q = jnp.dot(cq, wuq_ref[...], preferred_element_type=jnp.float32)
        ckv = _rms(u[:, SEG_CKV:SEG_CKV + MLA_KV_RANK], kvn_ref[...]).astype(jnp.bfloat16)
        kv = jnp.dot(ckv, wukv_ref[...], preferred_element_type=jnp.float32)
        kr = _rope(u[:, SEG_KR:SEG_KR + LANES], cm, sm, first_m, 16)
        for hd in range(MLA_HEADS):
            sl = slice(hd * LANES, (hd + 1) * LANES)
            qm_ref[rs, sl] = _rope(q[:, sl], cm, sm, first_m, 16).astype(jnp.bfloat16)
            km_ref[rs, sl] = (kv[:, sl] + kr).astype(jnp.bfloat16)
        vm_ref[rs, :] = kv[:, MLA_HEADS * LANES:].astype(jnp.bfloat16)

        for j in range(SWA_HEADS // 2):
            sl = slice(SEG_SQ + j * LANES, SEG_SQ + (j + 1) * LANES)
            qs_ref[rs, j * LANES:(j + 1) * LANES] = _rope(u[:, sl], cs, ss, first_s, 32).astype(jnp.bfloat16)
        ks_ref[rs, :] = _rope(u[:, SEG_SK:SEG_SK + LANES], cs, ss, first_s, 32).astype(jnp.bfloat16)
        vs_ref[rs, :] = u[:, SEG_SV:SEG_SV + LANES].astype(jnp.bfloat16)

        for j in range(2):
            sq = slice(SEG_DQ + j * LANES, SEG_DQ + (j + 1) * LANES)
            sk = slice(SEG_DK + j * LANES, SEG_DK + (j + 1) * LANES)
            qd_ref[rs, j * LANES:(j + 1) * LANES] = _rope(u[:, sq], cd, sd, first_d, 16).astype(jnp.bfloat16)
            kd_ref[rs, j * LANES:(j + 1) * LANES] = _rope(u[:, sk], cd, sd, first_d, 16).astype(jnp.bfloat16)
        vd_ref[rs, :] = u[:, SEG_DV:SEG_DV + 2 * LANES].astype(jnp.bfloat16)


def _proj_call(x, g, w1, qn, kvn, wuq, wukv, tables, combine=None):
    T = x.shape[0]
    tm = PROJ_TILE
    n_seq = SEQ // tm
    tok = lambda i, *_: (i, 0)
    const = lambda i, *_: (0, 0)
    tab = lambda i, *_: (i % n_seq, 0)
    widths = (6 * LANES, 6 * LANES, 3 * LANES, 3 * LANES, LANES, LANES,
              2 * LANES, 2 * LANES, 2 * LANES)
    weight_specs = [
        pl.BlockSpec((1, D_MODEL), const),
        pl.BlockSpec((D_MODEL, PROJ_WIDTH), const),
        pl.BlockSpec((1, MLA_Q_RANK), const),
        pl.BlockSpec((1, MLA_KV_RANK), const),
        pl.BlockSpec((MLA_Q_RANK, 6 * LANES), const),
        pl.BlockSpec((MLA_KV_RANK, 9 * LANES), const),
    ] + [pl.BlockSpec((tm, LANES), tab)] * 6
    out_specs = [pl.BlockSpec((tm, w), tok) for w in widths]
    out_shape = [jax.ShapeDtypeStruct((T, w), jnp.bfloat16) for w in widths]
    params = pltpu.CompilerParams(dimension_semantics=("arbitrary",), vmem_limit_bytes=VMEM_LIMIT)
    weights = (g, w1, qn, kvn, wuq, wukv, *tables)
    if combine is None:
        return pl.pallas_call(
            _proj_kernel,
            grid=(T // tm,),
            in_specs=[pl.BlockSpec((tm, D_MODEL), tok)] + weight_specs,
            out_specs=out_specs, out_shape=out_shape, compiler_params=params, name="proj",
        )(x, *weights)
    pos, rf, ys = combine
    return pl.pallas_call(
        _proj_combine_kernel,
        grid_spec=pltpu.PrefetchScalarGridSpec(
            num_scalar_prefetch=1,
            grid=(T // tm,),
            in_specs=[pl.BlockSpec((tm, D_MODEL), tok), pl.BlockSpec((tm, 2), tok),
                      pl.BlockSpec(memory_space=pl.ANY)] + weight_specs,
            out_specs=out_specs + [pl.BlockSpec((tm, D_MODEL), tok)],
            scratch_shapes=[pltpu.VMEM((2, tm * ROW_TILES, LANES), jnp.float32),
                            pltpu.VMEM((2, tm * ROW_TILES, LANES), jnp.float32),
                            pltpu.SemaphoreType.DMA((2,))],
        ),
        out_shape=out_shape + [jax.ShapeDtypeStruct((T, D_MODEL), jnp.float32)],
        compiler_params=params, name="proj_combine",
    )(pos, x, rf, ys, *weights)


def _fill_values(vx_ref, v_ref):
    lane = lax.broadcasted_iota(jnp.int32, v_ref.shape, 1)
    vx_ref[:, :LANES] = v_ref[...]
    vx_ref[:, LANES:] = jnp.where(lane == 0, 1.0, 0.0).astype(vx_ref.dtype)


def _softmax_pv(q, k, vx):
    s = lax.dot_general(q, k, _NT, preferred_element_type=jnp.float32)
    m = jnp.max(s, axis=-1, keepdims=True)
    p = jnp.exp2(s - m)
    o = jnp.dot(p.astype(jnp.bfloat16), vx, preferred_element_type=jnp.float32)
    return o[:, :LANES] / o[:, LANES:LANES + 1]


def _mla_kernel(q_ref, k_ref, v_ref, o_ref, vx_ref):
    _fill_values(vx_ref, v_ref)
    lane = lax.broadcasted_iota(jnp.int32, (ATTN_SUB, LANES), 1)
    for sb in range(ATTN_Q_TILE // ATTN_SUB):
        rows = slice(sb * ATTN_SUB, (sb + 1) * ATTN_SUB)
        o0 = _softmax_pv(q_ref[rows, :LANES], k_ref[:, :LANES], vx_ref[...])
        o1 = _softmax_pv(q_ref[rows, LANES:], k_ref[:, LANES:], vx_ref[...])
        o_ref[rows, :] = jnp.where(lane < MLA_V_DIM, o0, o1).astype(o_ref.dtype)


def _mla_call(q, k, v, batch):
    T = q.shape[0]
    tq = ATTN_Q_TILE
    nq = SEQ // tq
    return pl.pallas_call(
        _mla_kernel,
        grid=(batch, MLA_HEADS // 2, nq),
        in_specs=[
            pl.BlockSpec((tq, 2 * LANES), lambda b, p, i: (b * nq + i, p)),
            pl.BlockSpec((SEQ, 2 * LANES), lambda b, p, i: (b, p)),
            pl.BlockSpec((SEQ, LANES), lambda b, p, i: (b, p)),
        ],
        out_specs=pl.BlockSpec((tq, LANES), lambda b, p, i: (b * nq + i, p)),
        out_shape=jax.ShapeDtypeStruct((T, MLA_HEADS * MLA_V_DIM), jnp.bfloat16),
        scratch_shapes=[pltpu.VMEM((SEQ, 2 * LANES), jnp.bfloat16)],
        compiler_params=pltpu.CompilerParams(
            dimension_semantics=("arbitrary",) * 3, vmem_limit_bytes=VMEM_LIMIT),
        name="mla_attn",
    )(q, k, v)


def _swa_kernel(sink_ref, q_ref, k_ref, v_ref, o_ref):
    i = pl.program_id(1)
    nb = SEQ // WINDOW
    half = SWA_HEADS // 2
    rows = SWA_HEADS * WINDOW
    lane = lax.broadcasted_iota(jnp.int32, (WINDOW, LANES), 1)
    left = lane < HEAD_DIM
    row = lax.broadcasted_iota(jnp.int32, (rows, 3 * WINDOW), 0)
    col = lax.broadcasted_iota(jnp.int32, (rows, 3 * WINDOW), 1)
    rel = col - row % WINDOW
    band = (rel >= 0) & (rel <= 2 * WINDOW)
    head_col = lax.broadcasted_iota(jnp.int32, (rows, 1), 0) // WINDOW
    sink = jnp.zeros((rows, 1), jnp.float32)
    for hd in range(SWA_HEADS):
        sink = jnp.where(head_col == hd, sink_ref[hd], sink)
    for sub in range(SWA_Q_TILE // WINDOW):
        n = i * (SWA_Q_TILE // WINDOW) + sub
        prev = pl.multiple_of(jnp.maximum(n - 1, 0) * WINDOW, WINDOW)
        cur = pl.multiple_of(n * WINDOW, WINDOW)
        nxt = pl.multiple_of(jnp.minimum(n + 1, nb - 1) * WINDOW, WINDOW)
        kb = jnp.concatenate([k_ref[pl.ds(prev, WINDOW), :], k_ref[pl.ds(cur, WINDOW), :],
                              k_ref[pl.ds(nxt, WINDOW), :]], axis=0)
        vb = jnp.concatenate([v_ref[pl.ds(prev, WINDOW), :], v_ref[pl.ds(cur, WINDOW), :],
                              v_ref[pl.ds(nxt, WINDOW), :]], axis=0)
        lo_col = jnp.where(n > 0, 0, WINDOW)
        hi_col = jnp.where(n < nb - 1, 3 * WINDOW, 2 * WINDOW)
        valid = band & (col >= lo_col) & (col < hi_col)
        parts = [q_ref[sub * WINDOW:(sub + 1) * WINDOW, j * LANES:(j + 1) * LANES] for j in range(half)]
        zero = jnp.zeros_like(parts[0])
        qs = jnp.concatenate([jnp.where(left, p_, zero) for p_ in parts]
                             + [jnp.where(left, zero, p_) for p_ in parts], axis=0)
        s = lax.dot_general(qs, kb, _NT, preferred_element_type=jnp.float32)
        s = jnp.where(valid, s, -jnp.inf)
        m = jnp.maximum(jnp.max(s, axis=-1, keepdims=True), sink)
        p = jnp.exp2(s - m)
        l = jnp.sum(p, axis=-1, keepdims=True) + jnp.exp2(sink - m)
        o = jnp.dot(p.astype(jnp.bfloat16), vb, preferred_element_type=jnp.float32) / l
        for j in range(half):
            o_ref[sub * WINDOW:(sub + 1) * WINDOW, j * LANES:(j + 1) * LANES] = jnp.where(
                left, o[j * WINDOW:(j + 1) * WINDOW], o[(j + half) * WINDOW:(j + half + 1) * WINDOW]
            ).astype(o_ref.dtype)


def _swa_call(sink, q, k, v, batch):
    T = q.shape[0]
    tq = SWA_Q_TILE
    nq = SEQ // tq
    return pl.pallas_call(
        _swa_kernel,
        grid=(batch, nq),
        in_specs=[
            pl.BlockSpec(memory_space=pltpu.SMEM),
            pl.BlockSpec((tq, 3 * LANES), lambda b, i: (b * nq + i, 0)),
            pl.BlockSpec((SEQ, LANES), lambda b, i: (b, 0)),
            pl.BlockSpec((SEQ, LANES), lambda b, i: (b, 0)),
        ],
        out_specs=pl.BlockSpec((tq, 3 * LANES), lambda b, i: (b * nq + i, 0)),
        out_shape=jax.ShapeDtypeStruct((T, SWA_HEADS * HEAD_DIM), jnp.bfloat16),
        compiler_params=pltpu.CompilerParams(
            dimension_semantics=("arbitrary",) * 2, vmem_limit_bytes=VMEM_LIMIT),
        name="swa_attn",
    )(sink, q, k, v)


def _diff_kernel(lam_init, lq1_ref, lk1_ref, lq2_ref, lk2_ref, subln_ref,
                 q_ref, k_ref, v_ref, o_ref, vx_ref):
    lam = (jnp.exp(jnp.sum(lq1_ref[...] * lk1_ref[...], axis=-1, keepdims=True))
           - jnp.exp(jnp.sum(lq2_ref[...] * lk2_ref[...], axis=-1, keepdims=True))
           + lam_init)
    _fill_values(vx_ref, v_ref)
    lane = lax.broadcasted_iota(jnp.int32, (ATTN_SUB, LANES), 1)
    left = lane < DIFF_V_DIM
    for sb in range(DIFF_Q_TILE // ATTN_SUB):
        rows = slice(sb * ATTN_SUB, (sb + 1) * ATTN_SUB)
        q = q_ref[rows, :]
        zero = jnp.zeros_like(q)
        outs = []
        for side in range(2):
            comp = []
            for c in range(2):
                lo = side * DIFF_V_DIM + c * DIFF_QK_DIM
                qc = jnp.where((lane >= lo) & (lane < lo + DIFF_QK_DIM), q, zero)
                comp.append(_softmax_pv(qc, k_ref[...], vx_ref[...]))
            outs.append(comp[0] - lam * comp[1])
        o = jnp.where(left, outs[0], outs[1])
        sq = o * o
        ms_l = jnp.sum(jnp.where(left, sq, 0.0), axis=-1, keepdims=True) * (1.0 / DIFF_V_DIM)
        ms_r = jnp.sum(jnp.where(left, 0.0, sq), axis=-1, keepdims=True) * (1.0 / DIFF_V_DIM)
        r = jnp.where(left, lax.rsqrt(ms_l + NORM_EPS), lax.rsqrt(ms_r + NORM_EPS))
        o_ref[rows, :] = (o * r * subln_ref[...] * (1.0 - lam_init)).astype(o_ref.dtype)


def _diff_call(lam_init, lq1, lk1, lq2, lk2, subln2, q, k, v, batch):
    T = q.shape[0]
    tq = DIFF_Q_TILE
    nq = SEQ // tq
    small = lambda b, p, i: (0, 0)
    return pl.pallas_call(
        functools.partial(_diff_kernel, lam_init),
        grid=(batch, DIFF_HEADS // 2, nq),
        in_specs=[pl.BlockSpec((1, DIFF_QK_DIM), small)] * 4 + [
            pl.BlockSpec((1, LANES), small),
            pl.BlockSpec((tq, LANES), lambda b, p, i: (b * nq + i, p)),
            pl.BlockSpec((SEQ, LANES), lambda b, p, i: (b, p)),
            pl.BlockSpec((SEQ, LANES), lambda b, p, i: (b, p)),
        ],
        out_specs=pl.BlockSpec((tq, LANES), lambda b, p, i: (b * nq + i, p)),
        out_shape=jax.ShapeDtypeStruct((T, DIFF_HEADS * DIFF_V_DIM), jnp.bfloat16),
        scratch_shapes=[pltpu.VMEM((SEQ, 2 * LANES), jnp.bfloat16)],
        compiler_params=pltpu.CompilerParams(
            dimension_semantics=("arbitrary",) * 3, vmem_limit_bytes=VMEM_LIMIT),
        name="diff_attn",
    )(lq1, lk1, lq2, lk2, subln2, q, k, v)


def _outproj_kernel(x_ref, om_ref, os_ref, od_ref, wm_ref, ws_ref, wd_ref, g_ref,
                    wr2_ref, br_ref, x2_ref, lt_ref):
    x2 = (x_ref[...]
          + jnp.dot(om_ref[...], wm_ref[...], preferred_element_type=jnp.float32)
          + jnp.dot(os_ref[...], ws_ref[...], preferred_element_type=jnp.float32)
          + jnp.dot(od_ref[...], wd_ref[...], preferred_element_type=jnp.float32))
    x2_ref[...] = x2
    h2 = _rms(x2, g_ref[...])

    hi = h2.astype(jnp.bfloat16)
    lo = (h2 - hi.astype(jnp.float32)).astype(jnp.bfloat16)
    a = jnp.dot(hi, wr2_ref[...], preferred_element_type=jnp.float32)
    b = jnp.dot(lo, wr2_ref[:, :LANES], preferred_element_type=jnp.float32)
    logits = a[:, :LANES] + a[:, LANES:] + b + br_ref[...]
    lt_ref[...] = logits.T


def _outproj_call(x, om, os_, od, wm, ws, wd, g, wr2, br):
    T = x.shape[0]
    tm = PROJ_TILE
    tok = lambda i: (i, 0)
    const = lambda i: (0, 0)
    return pl.pallas_call(
        _outproj_kernel,
        grid=(T // tm,),
        in_specs=[
            pl.BlockSpec((tm, D_MODEL), tok),
            pl.BlockSpec((tm, 3 * LANES), tok),
            pl.BlockSpec((tm, 3 * LANES), tok),
            pl.BlockSpec((tm, 2 * LANES), tok),
            pl.BlockSpec((3 * LANES, D_MODEL), const),
            pl.BlockSpec((3 * LANES, D_MODEL), const),
            pl.BlockSpec((2 * LANES, D_MODEL), const),
            pl.BlockSpec((1, D_MODEL), const),
            pl.BlockSpec((D_MODEL, 2 * LANES), const),
            pl.BlockSpec((1, LANES), const),
        ],
        out_specs=[
            pl.BlockSpec((tm, D_MODEL), tok),
            pl.BlockSpec((LANES, tm), lambda i: (0, i)),
        ],
        out_shape=[
            jax.ShapeDtypeStruct((T, D_MODEL), jnp.float32),
            jax.ShapeDtypeStruct((LANES, T), jnp.float32),
        ],
        compiler_params=pltpu.CompilerParams(
            dimension_semantics=("arbitrary",), vmem_limit_bytes=VMEM_LIMIT),
        name="outproj",
    )(x, om, os_, od, wm, ws, wd, g, wr2, br)


def _route_kernel(lt_ref, ri_ref, rf_ref, cnt_ref, carry_ref):
    step = pl.program_id(0)

    @pl.when(step == 0)
    def _():
        carry_ref[...] = jnp.zeros_like(carry_ref)

    tc = lt_ref.shape[1]
    neg = -jnp.inf
    row8 = lax.broadcasted_iota(jnp.int32, (8, tc), 0)
    is_g = row8 < N_GROUPS
    gl = jnp.where(is_g, lt_ref[0:8, :], neg)
    gmax = jnp.max(gl, axis=0, keepdims=True)
    gidx = jnp.min(jnp.where(gl == gmax, row8, 8), axis=0, keepdims=True)
    gsum = jnp.sum(jnp.where(is_g, jnp.exp(gl - gmax), 0.0), axis=0, keepdims=True)
    g_gate = 1.0 / gsum
    gidx8 = jnp.broadcast_to(gidx, (8, tc))
    e_in = lt_ref[EXPERT_LANE0:EXPERT_LANE0 + 8, :]
    for g in range(1, N_GROUPS):
        lo = EXPERT_LANE0 + g * EXPERTS_PER_GROUP
        e_in = jnp.where(gidx8 == g, lt_ref[lo:lo + EXPERTS_PER_GROUP, :], e_in)
    m1 = jnp.max(e_in, axis=0, keepdims=True)
    i1 = jnp.min(jnp.where(e_in == m1, row8, 8), axis=0, keepdims=True)
    e_rest = jnp.where(row8 == i1, neg, e_in)
    m2 = jnp.max(e_rest, axis=0, keepdims=True)
    i2 = jnp.min(jnp.where(e_rest == m2, row8, 8), axis=0, keepdims=True)
    t = jnp.exp(m2 - m1)
    w1 = g_gate / (1.0 + t)
    w2 = g_gate * t / (1.0 + t)
    e1 = gidx * EXPERTS_PER_GROUP + i1
    e2 = gidx * EXPERTS_PER_GROUP + i2

    rowe = lax.broadcasted_iota(jnp.int32, (N_EXPERTS, tc), 0)
    hit1 = rowe == e1
    hit2 = rowe == e2
    onehot = jnp.where(hit1 | hit2, 1.0, 0.0)
    r = lax.broadcasted_iota(jnp.int32, (RANK_CHUNK, RANK_CHUNK), 0)
    c = lax.broadcasted_iota(jnp.int32, (RANK_CHUNK, RANK_CHUNK), 1)
    upper = jnp.where(r < c, 1.0, 0.0).astype(jnp.bfloat16)
    carry = carry_ref[:, 0:1]
    parts = []
    for j in range(tc // RANK_CHUNK):
        oh = onehot[:, j * RANK_CHUNK:(j + 1) * RANK_CHUNK]
        parts.append(jnp.dot(oh.astype(jnp.bfloat16), upper, preferred_element_type=jnp.float32) + carry)
        carry = carry + jnp.sum(oh, axis=1, keepdims=True)
    before = jnp.concatenate(parts, axis=1)
    rank1 = jnp.sum(jnp.where(hit1, before, 0.0), axis=0, keepdims=True).astype(jnp.int32)
    rank2 = jnp.sum(jnp.where(hit2, before, 0.0), axis=0, keepdims=True).astype(jnp.int32)
    carry_ref[...] = jnp.broadcast_to(carry, carry_ref.shape)
    cnt_ref[...] = jnp.broadcast_to(carry, cnt_ref.shape)

    ri_ref[...] = jnp.where(row8 == 0, e1, jnp.where(row8 == 1, e2, jnp.where(
        row8 == 2, rank1, jnp.where(row8 == 3, rank2, 0))))
    rf_ref[...] = jnp.where(row8 == 0, w1, jnp.where(row8 == 1, w2, 0.0))


def _route_call(lt):
    T = lt.shape[1]
    tc = ROUTE_CHUNK
    return pl.pallas_call(
        _route_kernel,
        grid=(T // tc,),
        in_specs=[pl.BlockSpec((LANES, tc), lambda i: (0, i))],
        out_specs=[
            pl.BlockSpec((8, tc), lambda i: (0, i)),
            pl.BlockSpec((8, tc), lambda i: (0, i)),
            pl.BlockSpec((N_EXPERTS, LANES), lambda i: (0, 0)),
        ],
        out_shape=[
            jax.ShapeDtypeStruct((8, T), jnp.int32),
            jax.ShapeDtypeStruct((8, T), jnp.float32),
            jax.ShapeDtypeStruct((N_EXPERTS, LANES), jnp.float32),
        ],
        scratch_shapes=[pltpu.VMEM((N_EXPERTS, LANES), jnp.float32)],
        compiler_params=pltpu.CompilerParams(
            dimension_semantics=("arbitrary",), vmem_limit_bytes=VMEM_LIMIT),
        name="route",
    )(lt)


def _plan_kernel(ri_ref, cnt_ref, pos_ref, te_ref, meta_ref):
    T = ri_ref.shape[1]
    counts = cnt_ref[...].astype(jnp.int32)
    padded = (counts + (MOE_TILE - 1)) // MOE_TILE * MOE_TILE
    r = lax.broadcasted_iota(jnp.int32, (N_EXPERTS, N_EXPERTS), 0)
    c = lax.broadcasted_iota(jnp.int32, (N_EXPERTS, N_EXPERTS), 1)
    lower = jnp.where(c < r, 1.0, 0.0)
    off = jnp.dot(lower, padded.astype(jnp.float32), preferred_element_type=jnp.float32,
                  precision=lax.Precision.HIGHEST).astype(jnp.int32)
    ends = off + padded
    total = jnp.max(ends, axis=0, keepdims=True)

    rowe = lax.broadcasted_iota(jnp.int32, (N_EXPERTS, T), 0)
    off_col = off[:, 0:1]
    row8 = lax.broadcasted_iota(jnp.int32, (8, T), 0)
    pos1 = jnp.sum(jnp.where(rowe == ri_ref[0:1, :], off_col, 0), axis=0, keepdims=True) + ri_ref[2:3, :]
    pos2 = jnp.sum(jnp.where(rowe == ri_ref[1:2, :], off_col, 0), axis=0, keepdims=True) + ri_ref[3:4, :]
    pos_ref[...] = jnp.where(row8 == 0, pos1, jnp.where(row8 == 1, pos2, 0))

    n_lanes = te_ref.shape[1]
    start = lax.broadcasted_iota(jnp.int32, (N_EXPERTS, n_lanes), 1) * MOE_TILE
    start = jnp.minimum(start, total[:, 0:1] - MOE_TILE)
    te = jnp.sum((ends[:, 0:1] <= start).astype(jnp.int32), axis=0, keepdims=True)
    te_ref[...] = jnp.broadcast_to(te, te_ref.shape)
    lane = lax.broadcasted_iota(jnp.int32, (N_EXPERTS, LANES), 1)
    fill = jnp.where(padded > 0, ends - MOE_TILE, -1)
    meta_ref[...] = jnp.where(lane == 0, fill, total // MOE_TILE)


def _plan_call(ri, cnt, max_tiles):
    T = ri.shape[1]
    n_lanes = -(-max_tiles // LANES) * LANES
    full = lambda i: (0, 0)
    return pl.pallas_call(
        _plan_kernel,
        grid=(1,),
        in_specs=[pl.BlockSpec((8, T), full), pl.BlockSpec((N_EXPERTS, LANES), full)],
        out_specs=[pl.BlockSpec((8, T), full), pl.BlockSpec((8, n_lanes), full),
                   pl.BlockSpec((N_EXPERTS, LANES), full)],
        out_shape=[jax.ShapeDtypeStruct((8, T), jnp.int32),
                   jax.ShapeDtypeStruct((8, n_lanes), jnp.int32),
                   jax.ShapeDtypeStruct((N_EXPERTS, LANES), jnp.int32)],
        compiler_params=pltpu.CompilerParams(
            dimension_semantics=("arbitrary",), vmem_limit_bytes=VMEM_LIMIT),
        name="slot_plan",
    )(ri, cnt)


def _push_kernel(pos_ref, fill_ref, nt_ref, x2_ref, g_ref, xs_ref, h_ref, zero_ref, sem):
    step = pl.program_id(0)
    tm = x2_ref.shape[0]
    n_tok = pl.num_programs(0) * tm
    _store_rows(h_ref, _rms(x2_ref[...], g_ref[...]))
    fill_rows = MOE_TILE * ROW_TILES
    max_tiles = xs_ref.shape[0] // fill_rows

    def fill_copy(slot):
        start = pl.multiple_of(slot * ROW_TILES, fill_rows)
        return pltpu.make_async_copy(zero_ref, xs_ref.at[pl.ds(start, fill_rows), :], sem)

    @pl.when(step == 0)
    def _():
        zero_ref[...] = jnp.zeros_like(zero_ref)
        for e in range(N_EXPERTS):
            @pl.when(fill_ref[e] >= 0)
            def _():
                fill_copy(fill_ref[e]).start()

        def tail_start(j, carry):
            fill_copy(j * MOE_TILE).start()
            return carry

        def tail_wait(j, carry):
            fill_copy(j * MOE_TILE).wait()
            return carry

        lax.fori_loop(nt_ref[0], max_tiles, tail_start, 0)
        for e in range(N_EXPERTS):
            @pl.when(fill_ref[e] >= 0)
            def _():
                fill_copy(fill_ref[e]).wait()
        lax.fori_loop(nt_ref[0], max_tiles, tail_wait, 0)

    def issue(r, carry):
        t = step * tm + r
        for k in range(2):
            pltpu.make_async_copy(_row(h_ref, r), _row(xs_ref, pos_ref[k * n_tok + t]), sem).start(priority=k)
        return carry

    lax.fori_loop(0, tm, issue, 0, unroll=8)

    def drain(r, carry):
        for k in range(2):
            pltpu.make_async_copy(_row(h_ref, 0), _row(xs_ref, 0), sem).wait()
        return carry

    lax.fori_loop(0, tm, drain, 0, unroll=8)


def _push_call(pos, fill, n_tiles, x2, g, n_slots):
    tm = MOVE_TILE
    n_tok = x2.shape[0]
    return pl.pallas_call(
        _push_kernel,
        grid_spec=pltpu.PrefetchScalarGridSpec(
            num_scalar_prefetch=3,
            grid=(n_tok // tm,),
            in_specs=[pl.BlockSpec((tm, D_MODEL), lambda i, pos, fill, nt: (i, 0)),
                      pl.BlockSpec((1, D_MODEL), lambda i, pos, fill, nt: (0, 0))],
            out_specs=pl.BlockSpec(memory_space=pl.ANY),
            scratch_shapes=[pltpu.VMEM((tm * ROW_TILES, LANES), jnp.float32),
                            pltpu.VMEM((MOE_TILE * ROW_TILES, LANES), jnp.float32),
                            pltpu.SemaphoreType.DMA(())],
        ),
        out_shape=jax.ShapeDtypeStruct((n_slots * ROW_TILES, LANES), jnp.float32),
        compiler_params=pltpu.CompilerParams(
            dimension_semantics=("arbitrary",), vmem_limit_bytes=VMEM_LIMIT),
        name="moe_push",
    )(pos, fill, n_tiles, x2, g)


def _expert_kernel(te_ref, nt_ref, xs_ref, wg_ref, wu_ref, wd_ref, ys_ref,
                   ring, obuf, wg_s, wu_s, wd_s, sem, osem):
    j = pl.program_id(0)
    nt = nt_ref[0]
    rows = MOE_TILE * ROW_TILES

    def tile_copy(tile):
        slot = tile % EXPERT_RING
        return pltpu.make_async_copy(xs_ref.at[pl.ds(pl.multiple_of(tile * rows, rows), rows), :],
                                     ring.at[slot], sem.at[slot])

    @pl.when(j == 0)
    def _():
        tile_copy(0).start()

        @pl.when(nt > 1)
        def _():
            tile_copy(1).start()

    @pl.when(j + 2 < nt)
    def _():
        tile_copy(j + 2).start()

    def out_copy(tile):
        slot = tile % 2
        return pltpu.make_async_copy(obuf.at[slot],
                                     ys_ref.at[pl.ds(pl.multiple_of(tile * rows, rows), rows), :],
                                     osem.at[slot])

    @pl.when(j < nt)
    def _():
        tile_copy(j).wait()

        @pl.when(j >= 2)
        def _():
            out_copy(j - 2).wait()

        changed = (j == 0) | (te_ref[j] != te_ref[jnp.maximum(j - 1, 0)])

        @pl.when(changed)
        def _():
            wg_s[...] = wg_ref[...].astype(jnp.bfloat16)
            wu_s[...] = wu_ref[...].astype(jnp.bfloat16)
            wd_s[...] = wd_ref[...].astype(jnp.bfloat16)

        xb = _load_rows(ring.at[j % EXPERT_RING], MOE_TILE).astype(jnp.bfloat16)
        a = jnp.dot(xb, wg_s[...], preferred_element_type=jnp.float32)
        u = jnp.dot(xb, wu_s[...], preferred_element_type=jnp.float32)
        hid = (a * jax.nn.sigmoid(a) * u).astype(jnp.bfloat16)
        _store_rows(obuf.at[j % 2], jnp.dot(hid, wd_s[...], preferred_element_type=jnp.float32))
        out_copy(j).start()

        @pl.when(j == nt - 1)
        def _():
            out_copy(j).wait()

            @pl.when(j >= 1)
            def _():
                out_copy(j - 1).wait()


def _expert_call(tile_expert, n_tiles, xs, wg, wu, wd, layer):
    n_slots = xs.shape[0] // ROW_TILES
    max_tiles = n_slots // MOE_TILE
    base = layer * N_EXPERTS
    wsel = lambda j, te, nt: (base + te[j], 0, 0)
    return pl.pallas_call(
        _expert_kernel,
        grid_spec=pltpu.PrefetchScalarGridSpec(
            num_scalar_prefetch=2,
            grid=(max_tiles,),
            in_specs=[
                pl.BlockSpec(memory_space=pl.ANY),
                pl.BlockSpec((None, D_MODEL, EXPERT_FF), wsel),
                pl.BlockSpec((None, D_MODEL, EXPERT_FF), wsel),
                pl.BlockSpec((None, EXPERT_FF, D_MODEL), wsel),
            ],
            out_specs=pl.BlockSpec(memory_space=pl.ANY),
            scratch_shapes=[pltpu.VMEM((EXPERT_RING, MOE_TILE * ROW_TILES, LANES), jnp.float32),
                            pltpu.VMEM((2, MOE_TILE * ROW_TILES, LANES), jnp.float32),
                            pltpu.VMEM((D_MODEL, EXPERT_FF), jnp.bfloat16),
                            pltpu.VMEM((D_MODEL, EXPERT_FF), jnp.bfloat16),
                            pltpu.VMEM((EXPERT_FF, D_MODEL), jnp.bfloat16),
                            pltpu.SemaphoreType.DMA((EXPERT_RING,)),
                            pltpu.SemaphoreType.DMA((2,))],
        ),
        out_shape=jax.ShapeDtypeStruct((n_slots * ROW_TILES, LANES), jnp.float32),
        input_output_aliases={2: 0},
        compiler_params=pltpu.CompilerParams(
            dimension_semantics=("arbitrary",), vmem_limit_bytes=VMEM_LIMIT),
        name="moe_experts",
    )(tile_expert, n_tiles, xs, wg, wu, wd)


def _combine_kernel(pos_ref, x2_ref, rf_ref, g_ref, ys_ref, o_ref, g0, g1, sem):
    step = pl.program_id(0)
    tm = x2_ref.shape[0]
    n_tok = pl.num_programs(0) * tm

    def issue(r, carry):
        t = step * tm + r
        pltpu.make_async_copy(_row(ys_ref, pos_ref[t]), _row(g0, r), sem).start(priority=0)
        pltpu.make_async_copy(_row(ys_ref, pos_ref[n_tok + t]), _row(g1, r), sem).start(priority=1)
        return carry

    lax.fori_loop(0, tm, issue, 0, unroll=8)

    def drain(r, carry):
        for k in range(2):
            pltpu.make_async_copy(_row(ys_ref, 0), _row(g0, 0), sem).wait()
        return carry

    lax.fori_loop(0, tm, drain, 0, unroll=8)

    rf = rf_ref[...]
    y = x2_ref[...] + rf[:, 0:1] * _load_rows(g0, tm) + rf[:, 1:2] * _load_rows(g1, tm)
    o_ref[...] = _rms(y, g_ref[...])


def _combine_call(pos, x2, rf, g, ys):
    T = x2.shape[0]
    tm = MOVE_TILE
    return pl.pallas_call(
        _combine_kernel,
        grid_spec=pltpu.PrefetchScalarGridSpec(
            num_scalar_prefetch=1,
            grid=(T // tm,),
            in_specs=[
                pl.BlockSpec((tm, D_MODEL), lambda i, pos: (i, 0)),
                pl.BlockSpec((tm, 2), lambda i, pos: (i, 0)),
                pl.BlockSpec((1, D_MODEL), lambda i, pos: (0, 0)),
                pl.BlockSpec(memory_space=pl.ANY),
            ],
            out_specs=pl.BlockSpec((tm, D_MODEL), lambda i, pos: (i, 0)),
            scratch_shapes=[pltpu.VMEM((tm * ROW_TILES, LANES), jnp.float32),
                            pltpu.VMEM((tm * ROW_TILES, LANES), jnp.float32),
                            pltpu.SemaphoreType.DMA(())],
        ),
        out_shape=jax.ShapeDtypeStruct((T, D_MODEL), jnp.float32),
        compiler_params=pltpu.CompilerParams(
            dimension_semantics=("arbitrary",), vmem_limit_bytes=VMEM_LIMIT),
        name="moe_combine",
    )(pos, x2, rf, g, ys)


def _rope_tables():
    def cos_sin(dim):
        inv = 1.0 / (ROPE_THETA ** (np.arange(0, dim, 2, dtype=np.float64) / dim))
        ang = np.arange(SEQ, dtype=np.float64)[:, None] * inv[None, :]
        return np.cos(ang), np.sin(ang)

    c16, s16 = cos_sin(MLA_ROPE_DIM)
    c32, s32 = cos_sin(HEAD_DIM)
    ones = np.ones((SEQ, 64))
    zeros = np.zeros((SEQ, 64))
    pad = np.zeros((SEQ, 32))
    cm = np.concatenate([ones, c16, c16, pad], axis=1)
    sm = np.concatenate([zeros, -s16, s16, pad], axis=1)
    cs = np.tile(np.concatenate([c32, c32], axis=1), (1, 2))
    ss = np.tile(np.concatenate([-s32, s32], axis=1), (1, 2))
    cd = np.tile(np.concatenate([c16, c16], axis=1), (1, 4))
    sd = np.tile(np.concatenate([-s16, s16], axis=1), (1, 4))
    return tuple(jnp.asarray(t, jnp.float32) for t in (cm, sm, cs, ss, cd, sd))


def _layer_weights(w_in, w_uq, w_ukv, w_out):
    f32 = jnp.float32
    cq = w_in[:, 0:384]
    ckv = w_in[:, 384:640]
    kr = w_in[:, 640:672]
    sq = w_in[:, 672:1056] * (LOG2E * HEAD_DIM ** -0.5)
    sk = w_in[:, 1056:1184]
    sv = w_in[:, 1184:1312]
    dq = w_in[:, 1312:1568] * (LOG2E * DIFF_QK_DIM ** -0.5)
    dk = w_in[:, 1568:1824]
    dv = w_in[:, 1824:2080]
    kr_pad = jnp.concatenate([jnp.zeros((D_MODEL, 64), f32), kr, jnp.zeros((D_MODEL, 32), f32)], axis=1)
    half = SWA_HEADS // 2
    sq_h = sq.reshape(D_MODEL, SWA_HEADS, HEAD_DIM)
    sq_perm = jnp.stack([sq_h[:, :half], sq_h[:, half:]], axis=2).reshape(D_MODEL, SWA_HEADS * HEAD_DIM)
    w1 = jnp.concatenate([cq, ckv, kr_pad, sq_perm, sk, sv, dq, dk, dv], axis=1).astype(jnp.bfloat16)

    scale = LOG2E * (MLA_NOPE_DIM + MLA_ROPE_DIM) ** -0.5
    uq = (w_uq * scale).reshape(MLA_Q_RANK, MLA_HEADS, MLA_NOPE_DIM + MLA_ROPE_DIM)
    uq = jnp.concatenate([uq, jnp.zeros((MLA_Q_RANK, MLA_HEADS, 32), f32)], axis=2)
    wuq = uq.reshape(MLA_Q_RANK, MLA_HEADS * LANES).astype(jnp.bfloat16)
    ukv = w_ukv.reshape(MLA_KV_RANK, MLA_HEADS, MLA_NOPE_DIM + MLA_V_DIM)
    uk = jnp.concatenate([ukv[:, :, :MLA_NOPE_DIM], jnp.zeros((MLA_KV_RANK, MLA_HEADS, 64), f32)], axis=2)
    uv = ukv[:, :, MLA_NOPE_DIM:]
    wukv = jnp.concatenate([uk.reshape(MLA_KV_RANK, MLA_HEADS * LANES),
                            uv.reshape(MLA_KV_RANK, MLA_HEADS * MLA_V_DIM)], axis=1).astype(jnp.bfloat16)

    wm = w_out[0:384].astype(jnp.bfloat16)
    wo_s = w_out[384:768].reshape(SWA_HEADS, HEAD_DIM, D_MODEL)
    ws = jnp.stack([wo_s[:half], wo_s[half:]], axis=1).reshape(SWA_HEADS * HEAD_DIM, D_MODEL).astype(jnp.bfloat16)
    wd = w_out[768:1024].astype(jnp.bfloat16)
    return w1, wuq, wukv, wm, ws, wd


def _router_weights(w_rg, b_rg, w_re, b_re):
    f32 = jnp.float32
    wr = jnp.concatenate([w_rg, jnp.zeros((D_MODEL, EXPERT_LANE0 - N_GROUPS), f32), w_re,
                          jnp.zeros((D_MODEL, LANES - EXPERT_LANE0 - N_EXPERTS), f32)], axis=1)
    br = jnp.concatenate([b_rg, jnp.zeros((EXPERT_LANE0 - N_GROUPS,), f32), b_re,
                          jnp.zeros((LANES - EXPERT_LANE0 - N_EXPERTS,), f32)])[None, :]
    hi = wr.astype(jnp.bfloat16)
    lo = (wr - hi.astype(f32)).astype(jnp.bfloat16)
    return jnp.concatenate([hi, lo], axis=1), br


def kernel(x, attn_norm, w_in, mla_q_norm, mla_kv_norm, mla_w_uq, mla_w_ukv, swa_sink, diff_lq1, diff_lk1, diff_lq2, diff_lk2, diff_subln, w_out, ffn_norm, router_group, router_group_bias, router_expert, router_expert_bias, w_gate, w_up, w_down, final_norm):
    batch, seq, d = x.shape
    assert (seq, d) == (SEQ, D_MODEL)
    depth = w_in.shape[0]
    n_tok = batch * seq
    tables = _rope_tables()
    wg = w_gate.reshape(depth * N_EXPERTS, D_MODEL, EXPERT_FF)
    wu = w_up.reshape(depth * N_EXPERTS, D_MODEL, EXPERT_FF)
    wdn = w_down.reshape(depth * N_EXPERTS, EXPERT_FF, D_MODEL)
    xt = x.reshape(n_tok, D_MODEL)
    pending = None
    for l in range(depth):
        w1, wuq, wukv, wm, ws, wd = _layer_weights(w_in[l], mla_w_uq[l], mla_w_ukv[l], w_out[l])
        outs = _proj_call(xt, attn_norm[l][None], w1, mla_q_norm[l][None], mla_kv_norm[l][None],
                          wuq, wukv, tables, combine=pending)
        qm, km, vm, qs, ks, vs, qd, kd, vd = outs[:9]
        if pending is not None:
            xt = outs[9]
        o_mla = _mla_call(qm, km, vm, batch)
        o_swa = _swa_call(swa_sink[l] * LOG2E, qs, ks, vs, batch)
        lam_init = 0.8 - 0.6 * math.exp(-0.3 * l)
        subln2 = jnp.concatenate([diff_subln[l], diff_subln[l]])[None]
        o_diff = _diff_call(lam_init, diff_lq1[l][None], diff_lk1[l][None], diff_lq2[l][None],
                            diff_lk2[l][None], subln2, qd, kd, vd, batch)
        wr2, br = _router_weights(router_group[l], router_group_bias[l],
                                  router_expert[l], router_expert_bias[l])
        x2, lt = _outproj_call(xt, o_mla, o_swa, o_diff, wm, ws, wd, ffn_norm[l][None], wr2, br)
        ri, rf, cnt = _route_call(lt)
        max_tiles = (2 * n_tok) // MOE_TILE + N_EXPERTS
        pos8, te8, meta = _plan_call(ri, cnt, max_tiles)
        pos = pos8[0:2].reshape(2 * n_tok)
        fill, n_tiles = meta[:, 0], meta[0, 1:2]
        xs = _push_call(pos, fill, n_tiles, x2, ffn_norm[l][None], max_tiles * MOE_TILE)
        ys = _expert_call(te8[0, :max_tiles], n_tiles, xs, wg, wu, wdn, l)
        xt, pending = x2, (pos, rf[0:2].T, ys)
    pos, rf2, ys = pending
    out = _combine_call(pos, xt, rf2, final_norm[None], ys)
    return out.reshape(batch, seq, d)
```

```python
import functools
import math

import jax
import jax.numpy as jnp
import numpy as np
from jax import lax
from jax.experimental import pallas as pl
from jax.experimental.pallas import tpu as pltpu

D_MODEL = 1024
SEQ = 2048
HEAD_DIM = 64
ROPE_THETA = 10000.0
NORM_EPS = 1e-6
MLA_HEADS = 6
MLA_Q_RANK = 384
MLA_KV_RANK = 256
MLA_NOPE_DIM = 64
MLA_ROPE_DIM = 32
MLA_V_DIM = 64
SWA_HEADS = 6
SWA_KV_HEADS = 2
WINDOW = 128
DIFF_HEADS = 4
DIFF_QK_DIM = 32
DIFF_V_DIM = 64
N_GROUPS = 4
EXPERTS_PER_GROUP = 8
N_EXPERTS = N_GROUPS * EXPERTS_PER_GROUP
EXPERT_FF = 256

LANES = 128
ROW_TILES = D_MODEL // LANES
EXPERT_LANE0 = 8
PROJ_TILE = 512
PROJ_SUB = 256
ATTN_Q_TILE = 2048
DIFF_Q_TILE = 2048
ATTN_SUB = 256
SWA_Q_TILE = 512
MOE_TILE = 256
MOVE_TILE = 512
EXPERT_RING = 3
ROUTE_CHUNK = 2048
RANK_CHUNK = 512
VMEM_LIMIT = 56 * 1024 * 1024

SEG_CQ = 0
SEG_CKV = SEG_CQ + MLA_Q_RANK
SEG_KR = SEG_CKV + MLA_KV_RANK
SEG_SQ = SEG_KR + LANES
SEG_SK = SEG_SQ + SWA_HEADS * HEAD_DIM
SEG_SV = SEG_SK + LANES
SEG_DQ = SEG_SV + LANES
SEG_DK = SEG_DQ + 2 * LANES
SEG_DV = SEG_DK + 2 * LANES
PROJ_WIDTH = SEG_DV + 2 * LANES

_NT = (((1,), (1,)), ((), ()))
LOG2E = math.log2(math.e)


def _rms(x, g):
    return x * lax.rsqrt(jnp.mean(x * x, axis=-1, keepdims=True) + NORM_EPS) * g


def _load_rows(ref, n):
    return jnp.concatenate([ref[pl.ds(c, n, stride=ROW_TILES), :] for c in range(ROW_TILES)], axis=1)


def _store_rows(ref, val):
    n = val.shape[0]
    for c in range(ROW_TILES):
        ref[pl.ds(c, n, stride=ROW_TILES), :] = val[:, c * LANES:(c + 1) * LANES]


def _rope(x, cos, sin, first_half, shift):
    up = pltpu.roll(x, LANES - shift, 1)
    dn = pltpu.roll(x, shift, 1)
    return x * cos + jnp.where(first_half, up, dn) * sin


def _proj_kernel(x_ref, *refs):
    _proj_body(x_ref[...], *refs)


def _row(ref, idx):
    return ref.at[pl.ds(pl.multiple_of(idx * ROW_TILES, ROW_TILES), ROW_TILES), :]


def _proj_combine_kernel(pos_ref, x2_ref, rf_ref, ys_ref, *refs):
    *body_refs, x3_ref, g0, g1, sem = refs
    step = pl.program_id(0)
    n_steps = pl.num_programs(0)
    tm = x2_ref.shape[0]
    n_tok = n_steps * tm

    def rows(tile, slot, start):
        def body(r, carry):
            if start:
                t = tile * tm + r
                pltpu.make_async_copy(_row(ys_ref, pos_ref[t]), _row(g0.at[slot], r), sem.at[slot]).start(priority=0)
                pltpu.make_async_copy(_row(ys_ref, pos_ref[n_tok + t]), _row(g1.at[slot], r), sem.at[slot]).start(priority=1)
            else:
                for g in (g0, g1):
                    pltpu.make_async_copy(_row(ys_ref, 0), _row(g.at[slot], 0), sem.at[slot]).wait()
            return carry
        lax.fori_loop(0, tm, body, 0, unroll=8)

    @pl.when(step == 0)
    def _():
        rows(0, 0, True)

    @pl.when(step + 1 < n_steps)
    def _():
        rows(step + 1, (step + 1) % 2, True)

    slot = step % 2
    rows(step, slot, False)
    rf = rf_ref[...]
    x = x2_ref[...] + rf[:, 0:1] * _load_rows(g0.at[slot], tm) + rf[:, 1:2] * _load_rows(g1.at[slot], tm)
    x3_ref[...] = x
    _proj_body(x, *body_refs)


def _proj_body(x, g_ref, w1_ref, qn_ref, kvn_ref, wuq_ref, wukv_ref,
               cm_ref, sm_ref, cs_ref, ss_ref, cd_ref, sd_ref,
               qm_ref, km_ref, vm_ref, qs_ref, ks_ref, vs_ref,
               qd_ref, kd_ref, vd_ref):
    lane = lax.broadcasted_iota(jnp.int32, (PROJ_SUB, LANES), 1)
    first_m = (lane >= 64) & (lane < 80)
    first_s = (lane % 64) < 32
    first_d = (lane % 32) < 16
    for sb in range(x.shape[0] // PROJ_SUB):
        rs = slice(sb * PROJ_SUB, (sb + 1) * PROJ_SUB)
        h = _rms(x[rs], g_ref[...]).astype(jnp.bfloat16)
        u = jnp.dot(h, w1_ref[...], preferred_element_type=jnp.float32)
        cm, sm = cm_ref[rs, :], sm_ref[rs, :]
        cs, ss = cs_ref[rs, :], ss_ref[rs, :]
        cd, sd = cd_ref[rs, :], sd_ref[rs, :]

        cq = _rms(u[:, SEG_CQ:SEG_CQ + MLA_Q_RANK], qn_ref[...]).astype(jnp.bfloat16)
        q = jnp.dot(cq, wuq_ref[...], preferred_element_type=jnp.float32)
        ckv = _rms(u[:, SEG_CKV:SEG_CKV + MLA_KV_RANK], kvn_ref[...]).astype(jnp.bfloat16)
        kv = jnp.dot(ckv, wukv_ref[...], preferred_element_type=jnp.float32)
        kr = _rope(u[:, SEG_KR:SEG_KR + LANES], cm, sm, first_m, 16)
        for hd in range(MLA_HEADS):
            sl = slice(hd * LANES, (hd + 1) * LANES)
            qm_ref[rs, sl] = _rope(q[:, sl], cm, sm, first_m, 16).astype(jnp.bfloat16)
            km_ref[rs, sl] = (kv[:, sl] + kr).astype(jnp.bfloat16)
        vm_ref[rs, :] = kv[:, MLA_HEADS * LANES:].astype(jnp.bfloat16)

        for j in range(SWA_HEADS // 2):
            sl = slice(SEG_SQ + j * LANES, SEG_SQ + (j + 1) * LANES)
            qs_ref[rs, j * LANES:(j + 1) * LANES] = _rope(u[:, sl], cs, ss, first_s, 32).astype(jnp.bfloat16)
        ks_ref[rs, :] = _rope(u[:, SEG_SK:SEG_SK + LANES], cs, ss, first_s, 32).astype(jnp.bfloat16)
        vs_ref[rs, :] = u[:, SEG_SV:SEG_SV + LANES].astype(jnp.bfloat16)

        for j in range(2):
            sq = slice(SEG_DQ + j * LANES, SEG_DQ + (j + 1) * LANES)
            sk = slice(SEG_DK + j * LANES, SEG_DK + (j + 1) * LANES)
            qd_ref[rs, j * LANES:(j + 1) * LANES] = _rope(u[:, sq], cd, sd, first_d, 16).astype(jnp.bfloat16)
            kd_ref[rs, j * LANES:(j + 1) * LANES] = _rope(u[:, sk], cd, sd, first_d, 16).astype(jnp.bfloat16)
        vd_ref[rs, :] = u[:, SEG_DV:SEG_DV + 2 * LANES].astype(jnp.bfloat16)


def _proj_call(x, g, w1, qn, kvn, wuq, wukv, tables, combine=None):
    T = x.shape[0]
    tm = PROJ_TILE
    n_seq = SEQ // tm
    tok = lambda i, *_: (i, 0)
    const = lambda i, *_: (0, 0)
    tab = lambda i, *_: (i % n_seq, 0)
    widths = (6 * LANES, 6 * LANES, 3 * LANES, 3 * LANES, LANES, LANES,
              2 * LANES, 2 * LANES, 2 * LANES)
    weight_specs = [
        pl.BlockSpec((1, D_MODEL), const),
        pl.BlockSpec((D_MODEL, PROJ_WIDTH), const),
        pl.BlockSpec((1, MLA_Q_RANK), const),
        pl.BlockSpec((1, MLA_KV_RANK), const),
        pl.BlockSpec((MLA_Q_RANK, 6 * LANES), const),
        pl.BlockSpec((MLA_KV_RANK, 9 * LANES), const),
    ] + [pl.BlockSpec((tm, LANES), tab)] * 6
    out_specs = [pl.BlockSpec((tm, w), tok) for w in widths]
    out_shape = [jax.ShapeDtypeStruct((T, w), jnp.bfloat16) for w in widths]
    params = pltpu.CompilerParams(dimension_semantics=("arbitrary",), vmem_limit_bytes=VMEM_LIMIT)
    weights = (g, w1, qn, kvn, wuq, wukv, *tables)
    if combine is None:
        return pl.pallas_call(
            _proj_kernel,
            grid=(T // tm,),
            in_specs=[pl.BlockSpec((tm, D_MODEL), tok)] + weight_specs,
            out_specs=out_specs, out_shape=out_shape, compiler_params=params, name="proj",
        )(x, *weights)
    pos, rf, ys = combine
    return pl.pallas_call(
        _proj_combine_kernel,
        grid_spec=pltpu.PrefetchScalarGridSpec(
            num_scalar_prefetch=1,
            grid=(T // tm,),
            in_specs=[pl.BlockSpec((tm, D_MODEL), tok), pl.BlockSpec((tm, 2), tok),
                      pl.BlockSpec(memory_space=pl.ANY)] + weight_specs,
            out_specs=out_specs + [pl.BlockSpec((tm, D_MODEL), tok)],
            scratch_shapes=[pltpu.VMEM((2, tm * ROW_TILES, LANES), jnp.float32),
                            pltpu.VMEM((2, tm * ROW_TILES, LANES), jnp.float32),
                            pltpu.SemaphoreType.DMA((2,))],
        ),
        out_shape=out_shape + [jax.ShapeDtypeStruct((T, D_MODEL), jnp.float32)],
        compiler_params=params, name="proj_combine",
    )(pos, x, rf, ys, *weights)


def _fill_values(vx_ref, v_ref):
    lane = lax.broadcasted_iota(jnp.int32, v_ref.shape, 1)
    vx_ref[:, :LANES] = v_ref[...]
    vx_ref[:, LANES:] = jnp.where(lane == 0, 1.0, 0.0).astype(vx_ref.dtype)


def _softmax_pv(q, k, vx):
    s = lax.dot_general(q, k, _NT, preferred_element_type=jnp.float32)
    m = jnp.max(s, axis=-1, keepdims=True)
    p = jnp.exp2(s - m)
    o = jnp.dot(p.astype(jnp.bfloat16), vx, preferred_element_type=jnp.float32)
    return o[:, :LANES] / o[:, LANES:LANES + 1]


def _mla_kernel(q_ref, k_ref, v_ref, o_ref, vx_ref):
    _fill_values(vx_ref, v_ref)
    lane = lax.broadcasted_iota(jnp.int32, (ATTN_SUB, LANES), 1)
    for sb in range(ATTN_Q_TILE // ATTN_SUB):
        rows = slice(sb * ATTN_SUB, (sb + 1) * ATTN_SUB)
        o0 = _softmax_pv(q_ref[rows, :LANES], k_ref[:, :LANES], vx_ref[...])
        o1 = _softmax_pv(q_ref[rows, LANES:], k_ref[:, LANES:], vx_ref[...])
        o_ref[rows, :] = jnp.where(lane < MLA_V_DIM, o0, o1).astype(o_ref.dtype)


def _mla_call(q, k, v, batch):
    T = q.shape[0]
    tq = ATTN_Q_TILE
    nq = SEQ // tq
    return pl.pallas_call(
        _mla_kernel,
        grid=(batch, MLA_HEADS // 2, nq),
        in_specs=[
            pl.BlockSpec((tq, 2 * LANES), lambda b, p, i: (b * nq + i, p)),
            pl.BlockSpec((SEQ, 2 * LANES), lambda b, p, i: (b, p)),
            pl.BlockSpec((SEQ, LANES), lambda b, p, i: (b, p)),
        ],
        out_specs=pl.BlockSpec((tq, LANES), lambda b, p, i: (b * nq + i, p)),
        out_shape=jax.ShapeDtypeStruct((T, MLA_HEADS * MLA_V_DIM), jnp.bfloat16),
        scratch_shapes=[pltpu.VMEM((SEQ, 2 * LANES), jnp.bfloat16)],
        compiler_params=pltpu.CompilerParams(
            dimension_semantics=("arbitrary",) * 3, vmem_limit_bytes=VMEM_LIMIT),
        name="mla_attn",
    )(q, k, v)


def _swa_kernel(sink_ref, q_ref, k_ref, v_ref, o_ref):
    i = pl.program_id(1)
    nb = SEQ // WINDOW
    half = SWA_HEADS // 2
    rows = SWA_HEADS * WINDOW
    lane = lax.broadcasted_iota(jnp.int32, (WINDOW, LANES), 1)
    left = lane < HEAD_DIM
    row = lax.broadcasted_iota(jnp.int32, (rows, 3 * WINDOW), 0)
    col = lax.broadcasted_iota(jnp.int32, (rows, 3 * WINDOW), 1)
    rel = col - row % WINDOW
    band = (rel >= 0) & (rel <= 2 * WINDOW)
    head_col = lax.broadcasted_iota(jnp.int32, (rows, 1), 0) // WINDOW
    sink = jnp.zeros((rows, 1), jnp.float32)
    for hd in range(SWA_HEADS):
        sink = jnp.where(head_col == hd, sink_ref[hd], sink)
    for sub in range(SWA_Q_TILE // WINDOW):
        n = i * (SWA_Q_TILE // WINDOW) + sub
        prev = pl.multiple_of(jnp.maximum(n - 1, 0) * WINDOW, WINDOW)
        cur = pl.multiple_of(n * WINDOW, WINDOW)
        nxt = pl.multiple_of(jnp.minimum(n + 1, nb - 1) * WINDOW, WINDOW)
        kb = jnp.concatenate([k_ref[pl.ds(prev, WINDOW), :], k_ref[pl.ds(cur, WINDOW), :],
                              k_ref[pl.ds(nxt, WINDOW), :]], axis=0)
        vb = jnp.concatenate([v_ref[pl.ds(prev, WINDOW), :], v_ref[pl.ds(cur, WINDOW), :],
                              v_ref[pl.ds(nxt, WINDOW), :]], axis=0)
        lo_col = jnp.where(n > 0, 0, WINDOW)
        hi_col = jnp.where(n < nb - 1, 3 * WINDOW, 2 * WINDOW)
        valid = band & (col >= lo_col) & (col < hi_col)
        parts = [q_ref[sub * WINDOW:(sub + 1) * WINDOW, j * LANES:(j + 1) * LANES] for j in range(half)]
        zero = jnp.zeros_like(parts[0])
        qs = jnp.concatenate([jnp.where(left, p_, zero) for p_ in parts]
                             + [jnp.where(left, zero, p_) for p_ in parts], axis=0)
        s = lax.dot_general(qs, kb, _NT, preferred_element_type=jnp.float32)
        s = jnp.where(valid, s, -jnp.inf)
        m = jnp.maximum(jnp.max(s, axis=-1, keepdims=True), sink)
        p = jnp.exp2(s - m)
        l = jnp.sum(p, axis=-1, keepdims=True) + jnp.exp2(sink - m)
        o = jnp.dot(p.astype(jnp.bfloat16), vb, preferred_element_type=jnp.float32) / l
        for j in range(half):
            o_ref[sub * WINDOW:(sub + 1) * WINDOW, j * LANES:(j + 1) * LANES] = jnp.where(
                left, o[j * WINDOW:(j + 1) * WINDOW], o[(j + half) * WINDOW:(j + half + 1) * WINDOW]
            ).astype(o_ref.dtype)


def _swa_call(sink, q, k, v, batch):
    T = q.shape[0]
    tq = SWA_Q_TILE
    nq = SEQ // tq
    return pl.pallas_call(
        _swa_kernel,
        grid=(batch, nq),
        in_specs=[
            pl.BlockSpec(memory_space=pltpu.SMEM),
            pl.BlockSpec((tq, 3 * LANES), lambda b, i: (b * nq + i, 0)),
            pl.BlockSpec((SEQ, LANES), lambda b, i: (b, 0)),
            pl.BlockSpec((SEQ, LANES), lambda b, i: (b, 0)),
        ],
        out_specs=pl.BlockSpec((tq, 3 * LANES), lambda b, i: (b * nq + i, 0)),
        out_shape=jax.ShapeDtypeStruct((T, SWA_HEADS * HEAD_DIM), jnp.bfloat16),
        compiler_params=pltpu.CompilerParams(
            dimension_semantics=("arbitrary",) * 2, vmem_limit_bytes=VMEM_LIMIT),
        name="swa_attn",
    )(sink, q, k, v)


def _diff_kernel(lam_init, lq1_ref, lk1_ref, lq2_ref, lk2_ref, subln_ref,
                 q_ref, k_ref, v_ref, o_ref, vx_ref):
    lam = (jnp.exp(jnp.sum(lq1_ref[...] * lk1_ref[...], axis=-1, keepdims=True))
           - jnp.exp(jnp.sum(lq2_ref[...] * lk2_ref[...], axis=-1, keepdims=True))
           + lam_init)
    _fill_values(vx_ref, v_ref)
    lane = lax.broadcasted_iota(jnp.int32, (ATTN_SUB, LANES), 1)
    left = lane < DIFF_V_DIM
    for sb in range(DIFF_Q_TILE // ATTN_SUB):
        rows = slice(sb * ATTN_SUB, (sb + 1) * ATTN_SUB)
        q = q_ref[rows, :]
        zero = jnp.zeros_like(q)
        outs = []
        for side in range(2):
            comp = []
            for c in range(2):
                lo = side * DIFF_V_DIM + c * DIFF_QK_DIM
                qc = jnp.where((lane >= lo) & (lane < lo + DIFF_QK_DIM), q, zero)
                comp.append(_softmax_pv(qc, k_ref[...], vx_ref[...]))
            outs.append(comp[0] - lam * comp[1])
        o = jnp.where(left, outs[0], outs[1])
        sq = o * o
        ms_l = jnp.sum(jnp.where(left, sq, 0.0), axis=-1, keepdims=True) * (1.0 / DIFF_V_DIM)
        ms_r = jnp.sum(jnp.where(left, 0.0, sq), axis=-1, keepdims=True) * (1.0 / DIFF_V_DIM)
        r = jnp.where(left, lax.rsqrt(ms_l + NORM_EPS), lax.rsqrt(ms_r + NORM_EPS))
        o_ref[rows, :] = (o * r * subln_ref[...] * (1.0 - lam_init)).astype(o_ref.dtype)


def _diff_call(lam_init, lq1, lk1, lq2, lk2, subln2, q, k, v, batch):
    T = q.shape[0]
    tq = DIFF_Q_TILE
    nq = SEQ // tq
    small = lambda b, p, i: (0, 0)
    return pl.pallas_call(
        functools.partial(_diff_kernel, lam_init),
        grid=(batch, DIFF_HEADS // 2, nq),
        in_specs=[pl.BlockSpec((1, DIFF_QK_DIM), small)] * 4 + [
            pl.BlockSpec((1, LANES), small),
            pl.BlockSpec((tq, LANES), lambda b, p, i: (b * nq + i, p)),
            pl.BlockSpec((SEQ, LANES), lambda b, p, i: (b, p)),
            pl.BlockSpec((SEQ, LANES), lambda b, p, i: (b, p)),
        ],
        out_specs=pl.BlockSpec((tq, LANES), lambda b, p, i: (b * nq + i, p)),
        out_shape=jax.ShapeDtypeStruct((T, DIFF_HEADS * DIFF_V_DIM), jnp.bfloat16),
        scratch_shapes=[pltpu.VMEM((SEQ, 2 * LANES), jnp.bfloat16)],
        compiler_params=pltpu.CompilerParams(
            dimension_semantics=("arbitrary",) * 3, vmem_limit_bytes=VMEM_LIMIT),
        name="diff_attn",
    )(lq1, lk1, lq2, lk2, subln2, q, k, v)


def _outproj_kernel(x_ref, om_ref, os_ref, od_ref, wm_ref, ws_ref, wd_ref, g_ref,
                    wr2_ref, br_ref, x2_ref, lt_ref):
    x2 = (x_ref[...]
          + jnp.dot(om_ref[...], wm_ref[...], preferred_element_type=jnp.float32)
          + jnp.dot(os_ref[...], ws_ref[...], preferred_element_type=jnp.float32)
          + jnp.dot(od_ref[...], wd_ref[...], preferred_element_type=jnp.float32))
    x2_ref[...] = x2
    h2 = _rms(x2, g_ref[...])

    hi = h2.astype(jnp.bfloat16)
    lo = (h2 - hi.astype(jnp.float32)).astype(jnp.bfloat16)
    a = jnp.dot(hi, wr2_ref[...], preferred_element_type=jnp.float32)
    b = jnp.dot(lo, wr2_ref[:, :LANES], preferred_element_type=jnp.float32)
    logits = a[:, :LANES] + a[:, LANES:] + b + br_ref[...]
    lt_ref[...] = logits.T


def _outproj_call(x, om, os_, od, wm, ws, wd, g, wr2, br):
    T = x.shape[0]
    tm = PROJ_TILE
    tok = lambda i: (i, 0)
    const = lambda i: (0, 0)
    return pl.pallas_call(
        _outproj_kernel,
        grid=(T // tm,),
        in_specs=[
            pl.BlockSpec((tm, D_MODEL), tok),
            pl.BlockSpec((tm, 3 * LANES), tok),
            pl.BlockSpec((tm, 3 * LANES), tok),
            pl.BlockSpec((tm, 2 * LANES), tok),
            pl.BlockSpec((3 * LANES, D_MODEL), const),
            pl.BlockSpec((3 * LANES, D_MODEL), const),
            pl.BlockSpec((2 * LANES, D_MODEL), const),
            pl.BlockSpec((1, D_MODEL), const),
            pl.BlockSpec((D_MODEL, 2 * LANES), const),
            pl.BlockSpec((1, LANES), const),
        ],
        out_specs=[
            pl.BlockSpec((tm, D_MODEL), tok),
            pl.BlockSpec((LANES, tm), lambda i: (0, i)),
        ],
        out_shape=[
            jax.ShapeDtypeStruct((T, D_MODEL), jnp.float32),
            jax.ShapeDtypeStruct((LANES, T), jnp.float32),
        ],
        compiler_params=pltpu.CompilerParams(
            dimension_semantics=("arbitrary",), vmem_limit_bytes=VMEM_LIMIT),
        name="outproj",
    )(x, om, os_, od, wm, ws, wd, g, wr2, br)


def _route_kernel(lt_ref, ri_ref, rf_ref, cnt_ref, carry_ref):
    step = pl.program_id(0)

    @pl.when(step == 0)
    def _():
        carry_ref[...] = jnp.zeros_like(carry_ref)

    tc = lt_ref.shape[1]
    neg = -jnp.inf
    row8 = lax.broadcasted_iota(jnp.int32, (8, tc), 0)
    is_g = row8 < N_GROUPS
    gl = jnp.where(is_g, lt_ref[0:8, :], neg)
    gmax = jnp.max(gl, axis=0, keepdims=True)
    gidx = jnp.min(jnp.where(gl == gmax, row8, 8), axis=0, keepdims=True)
    gsum = jnp.sum(jnp.where(is_g, jnp.exp(gl - gmax), 0.0), axis=0, keepdims=True)
    g_gate = 1.0 / gsum
    gidx8 = jnp.broadcast_to(gidx, (8, tc))
    e_in = lt_ref[EXPERT_LANE0:EXPERT_LANE0 + 8, :]
    for g in range(1, N_GROUPS):
        lo = EXPERT_LANE0 + g * EXPERTS_PER_GROUP
        e_in = jnp.where(gidx8 == g, lt_ref[lo:lo + EXPERTS_PER_GROUP, :], e_in)
    m1 = jnp.max(e_in, axis=0, keepdims=True)
    i1 = jnp.min(jnp.where(e_in == m1, row8, 8), axis=0, keepdims=True)
    e_rest = jnp.where(row8 == i1, neg, e_in)
    m2 = jnp.max(e_rest, axis=0, keepdims=True)
    i2 = jnp.min(jnp.where(e_rest == m2, row8, 8), axis=0, keepdims=True)
    t = jnp.exp(m2 - m1)
    w1 = g_gate / (1.0 + t)
    w2 = g_gate * t / (1.0 + t)
    e1 = gidx * EXPERTS_PER_GROUP + i1
    e2 = gidx * EXPERTS_PER_GROUP + i2

    rowe = lax.broadcasted_iota(jnp.int32, (N_EXPERTS, tc), 0)
    hit1 = rowe == e1
    hit2 = rowe == e2
    onehot = jnp.where(hit1 | hit2, 1.0, 0.0)
    r = lax.broadcasted_iota(jnp.int32, (RANK_CHUNK, RANK_CHUNK), 0)
    c = lax.broadcasted_iota(jnp.int32, (RANK_CHUNK, RANK_CHUNK), 1)
    upper = jnp.where(r < c, 1.0, 0.0).astype(jnp.bfloat16)
    carry = carry_ref[:, 0:1]
    parts = []
    for j in range(tc // RANK_CHUNK):
        oh = onehot[:, j * RANK_CHUNK:(j + 1) * RANK_CHUNK]
        parts.append(jnp.dot(oh.astype(jnp.bfloat16), upper, preferred_element_type=jnp.float32) + carry)
        carry = carry + jnp.sum(oh, axis=1, keepdims=True)
    before = jnp.concatenate(parts, axis=1)
    rank1 = jnp.sum(jnp.where(hit1, before, 0.0), axis=0, keepdims=True).astype(jnp.int32)
    rank2 = jnp.sum(jnp.where(hit2, before, 0.0), axis=0, keepdims=True).astype(jnp.int32)
    carry_ref[...] = jnp.broadcast_to(carry, carry_ref.shape)
    cnt_ref[...] = jnp.broadcast_to(carry, cnt_ref.shape)

    ri_ref[...] = jnp.where(row8 == 0, e1, jnp.where(row8 == 1, e2, jnp.where(
        row8 == 2, rank1, jnp.where(row8 == 3, rank2, 0))))
    rf_ref[...] = jnp.where(row8 == 0, w1, jnp.where(row8 == 1, w2, 0.0))


def _route_call(lt):
    T = lt.shape[1]
    tc = ROUTE_CHUNK
    return pl.pallas_call(
        _route_kernel,
        grid=(T // tc,),
        in_specs=[pl.BlockSpec((LANES, tc), lambda i: (0, i))],
        out_specs=[
            pl.BlockSpec((8, tc), lambda i: (0, i)),
            pl.BlockSpec((8, tc), lambda i: (0, i)),
            pl.BlockSpec((N_EXPERTS, LANES), lambda i: (0, 0)),
        ],
        out_shape=[
            jax.ShapeDtypeStruct((8, T), jnp.int32),
            jax.ShapeDtypeStruct((8, T), jnp.float32),
            jax.ShapeDtypeStruct((N_EXPERTS, LANES), jnp.float32),
        ],
        scratch_shapes=[pltpu.VMEM((N_EXPERTS, LANES), jnp.float32)],
        compiler_params=pltpu.CompilerParams(
            dimension_semantics=("arbitrary",), vmem_limit_bytes=VMEM_LIMIT),
        name="route",
    )(lt)


def _plan_kernel(ri_ref, cnt_ref, pos_ref, te_ref, meta_ref):
    T = ri_ref.shape[1]
    counts = cnt_ref[...].astype(jnp.int32)
    padded = (counts + (MOE_TILE - 1)) // MOE_TILE * MOE_TILE
    r = lax.broadcasted_iota(jnp.int32, (N_EXPERTS, N_EXPERTS), 0)
    c = lax.broadcasted_iota(jnp.int32, (N_EXPERTS, N_EXPERTS), 1)
    lower = jnp.where(c < r, 1.0, 0.0)
    off = jnp.dot(lower, padded.astype(jnp.float32), preferred_element_type=jnp.float32,
                  precision=lax.Precision.HIGHEST).astype(jnp.int32)
    ends = off + padded
    total = jnp.max(ends, axis=0, keepdims=True)

    rowe = lax.broadcasted_iota(jnp.int32, (N_EXPERTS, T), 0)
    off_col = off[:, 0:1]
    row8 = lax.broadcasted_iota(jnp.int32, (8, T), 0)
    pos1 = jnp.sum(jnp.where(rowe == ri_ref[0:1, :], off_col, 0), axis=0, keepdims=True) + ri_ref[2:3, :]
    pos2 = jnp.sum(jnp.where(rowe == ri_ref[1:2, :], off_col, 0), axis=0, keepdims=True) + ri_ref[3:4, :]
    pos_ref[...] = jnp.where(row8 == 0, pos1, jnp.where(row8 == 1, pos2, 0))

    n_lanes = te_ref.shape[1]
    start = lax.broadcasted_iota(jnp.int32, (N_EXPERTS, n_lanes), 1) * MOE_TILE
    start = jnp.minimum(start, total[:, 0:1] - MOE_TILE)
    te = jnp.sum((ends[:, 0:1] <= start).astype(jnp.int32), axis=0, keepdims=True)
    te_ref[...] = jnp.broadcast_to(te, te_ref.shape)
    lane = lax.broadcasted_iota(jnp.int32, (N_EXPERTS, LANES), 1)
    fill = jnp.where(padded > 0, ends - MOE_TILE, -1)
    meta_ref[...] = jnp.where(lane == 0, fill, total // MOE_TILE)


def _plan_call(ri, cnt, max_tiles):
    T = ri.shape[1]
    n_lanes = -(-max_tiles // LANES) * LANES
    full = lambda i: (0, 0)
    return pl.pallas_call(
        _plan_kernel,
        grid=(1,),
        in_specs=[pl.BlockSpec((8, T), full), pl.BlockSpec((N_EXPERTS, LANES), full)],
        out_specs=[pl.BlockSpec((8, T), full), pl.BlockSpec((8, n_lanes), full),
                   pl.BlockSpec((N_EXPERTS, LANES), full)],
        out_shape=[jax.ShapeDtypeStruct((8, T), jnp.int32),
                   jax.ShapeDtypeStruct((8, n_lanes), jnp.int32),
                   jax.ShapeDtypeStruct((N_EXPERTS, LANES), jnp.int32)],
        compiler_params=pltpu.CompilerParams(
            dimension_semantics=("arbitrary",), vmem_limit_bytes=VMEM_LIMIT),
        name="slot_plan",
    )(ri, cnt)


def _push_kernel(pos_ref, fill_ref, nt_ref, x2_ref, g_ref, xs_ref, h_ref, zero_ref, sem):
    step = pl.program_id(0)
    tm = x2_ref.shape[0]
    n_tok = pl.num_programs(0) * tm
    _store_rows(h_ref, _rms(x2_ref[...], g_ref[...]))
    fill_rows = MOE_TILE * ROW_TILES
    max_tiles = xs_ref.shape[0] // fill_rows

    def fill_copy(slot):
        start = pl.multiple_of(slot * ROW_TILES, fill_rows)
        return pltpu.make_async_copy(zero_ref, xs_ref.at[pl.ds(start, fill_rows), :], sem)

    @pl.when(step == 0)
    def _():
        zero_ref[...] = jnp.zeros_like(zero_ref)
        for e in range(N_EXPERTS):
            @pl.when(fill_ref[e] >= 0)
            def _():
                fill_copy(fill_ref[e]).start()

        def tail_start(j, carry):
            fill_copy(j * MOE_TILE).start()
            return carry

        def tail_wait(j, carry):
            fill_copy(j * MOE_TILE).wait()
            return carry

        lax.fori_loop(nt_ref[0], max_tiles, tail_start, 0)
        for e in range(N_EXPERTS):
            @pl.when(fill_ref[e] >= 0)
            def _():
                fill_copy(fill_ref[e]).wait()
        lax.fori_loop(nt_ref[0], max_tiles, tail_wait, 0)

    def issue(r, carry):
        t = step * tm + r
        for k in range(2):
            pltpu.make_async_copy(_row(h_ref, r), _row(xs_ref, pos_ref[k * n_tok + t]), sem).start(priority=k)
        return carry

    lax.fori_loop(0, tm, issue, 0, unroll=8)

    def drain(r, carry):
        for k in range(2):
            pltpu.make_async_copy(_row(h_ref, 0), _row(xs_ref, 0), sem).wait()
        return carry

    lax.fori_loop(0, tm, drain, 0, unroll=8)


def _push_call(pos, fill, n_tiles, x2, g, n_slots):
    tm = MOVE_TILE
    n_tok = x2.shape[0]
    return pl.pallas_call(
        _push_kernel,
        grid_spec=pltpu.PrefetchScalarGridSpec(
            num_scalar_prefetch=3,
            grid=(n_tok // tm,),
            in_specs=[pl.BlockSpec((tm, D_MODEL), lambda i, pos, fill, nt: (i, 0)),
                      pl.BlockSpec((1, D_MODEL), lambda i, pos, fill, nt: (0, 0))],
            out_specs=pl.BlockSpec(memory_space=pl.ANY),
            scratch_shapes=[pltpu.VMEM((tm * ROW_TILES, LANES), jnp.float32),
                            pltpu.VMEM((MOE_TILE * ROW_TILES, LANES), jnp.float32),
                            pltpu.SemaphoreType.DMA(())],
        ),
        out_shape=jax.ShapeDtypeStruct((n_slots * ROW_TILES, LANES), jnp.float32),
        compiler_params=pltpu.CompilerParams(
            dimension_semantics=("arbitrary",), vmem_limit_bytes=VMEM_LIMIT),
        name="moe_push",
    )(pos, fill, n_tiles, x2, g)


def _expert_kernel(te_ref, nt_ref, xs_ref, wg_ref, wu_ref, wd_ref, ys_ref,
                   ring, obuf, wg_s, wu_s, wd_s, sem, osem):
    j = pl.program_id(0)
    nt = nt_ref[0]
    rows = MOE_TILE * ROW_TILES

    def tile_copy(tile):
        slot = tile % EXPERT_RING
        return pltpu.make_async_copy(xs_ref.at[pl.ds(pl.multiple_of(tile * rows, rows), rows), :],
                                     ring.at[slot], sem.at[slot])

    @pl.when(j == 0)
    def _():
        tile_copy(0).start()

        @pl.when(nt > 1)
        def _():
            tile_copy(1).start()

    @pl.when(j + 2 < nt)
    def _():
        tile_copy(j + 2).start()

    def out_copy(tile):
        slot = tile % 2
        return pltpu.make_async_copy(obuf.at[slot],
                                     ys_ref.at[pl.ds(pl.multiple_of(tile * rows, rows), rows), :],
                                     osem.at[slot])

    @pl.when(j < nt)
    def _():
        tile_copy(j).wait()

        @pl.when(j >= 2)
        def _():
            out_copy(j - 2).wait()

        changed = (j == 0) | (te_ref[j] != te_ref[jnp.maximum(j - 1, 0)])

        @pl.when(changed)
        def _():
            wg_s[...] = wg_ref[...].astype(jnp.bfloat16)
            wu_s[...] = wu_ref[...].astype(jnp.bfloat16)
            wd_s[...] = wd_ref[...].astype(jnp.bfloat16)

        xb = _load_rows(ring.at[j % EXPERT_RING], MOE_TILE).astype(jnp.bfloat16)
        a = jnp.dot(xb, wg_s[...], preferred_element_type=jnp.float32)
        u = jnp.dot(xb, wu_s[...], preferred_element_type=jnp.float32)
        hid = (a * jax.nn.sigmoid(a) * u).astype(jnp.bfloat16)
        _store_rows(obuf.at[j % 2], jnp.dot(hid, wd_s[...], preferred_element_type=jnp.float32))
        out_copy(j).start()

        @pl.when(j == nt - 1)
        def _():
            out_copy(j).wait()

            @pl.when(j >= 1)
            def _():
                out_copy(j - 1).wait()


def _expert_call(tile_expert, n_tiles, xs, wg, wu, wd, layer):
    n_slots = xs.shape[0] // ROW_TILES
    max_tiles = n_slots // MOE_TILE
    base = layer * N_EXPERTS
    wsel = lambda j, te, nt: (base + te[j], 0, 0)
    return pl.pallas_call(
        _expert_kernel,
        grid_spec=pltpu.PrefetchScalarGridSpec(
            num_scalar_prefetch=2,
            grid=(max_tiles,),
            in_specs=[
                pl.BlockSpec(memory_space=pl.ANY),
                pl.BlockSpec((None, D_MODEL, EXPERT_FF), wsel),
                pl.BlockSpec((None, D_MODEL, EXPERT_FF), wsel),
                pl.BlockSpec((None, EXPERT_FF, D_MODEL), wsel),
            ],
            out_specs=pl.BlockSpec(memory_space=pl.ANY),
            scratch_shapes=[pltpu.VMEM((EXPERT_RING, MOE_TILE * ROW_TILES, LANES), jnp.float32),
                            pltpu.VMEM((2, MOE_TILE * ROW_TILES, LANES), jnp.float32),
                            pltpu.VMEM((D_MODEL, EXPERT_FF), jnp.bfloat16),
                            pltpu.VMEM((D_MODEL, EXPERT_FF), jnp.bfloat16),
                            pltpu.VMEM((EXPERT_FF, D_MODEL), jnp.bfloat16),
                            pltpu.SemaphoreType.DMA((EXPERT_RING,)),
                            pltpu.SemaphoreType.DMA((2,))],
        ),
        out_shape=jax.ShapeDtypeStruct((n_slots * ROW_TILES, LANES), jnp.float32),
        input_output_aliases={2: 0},
        compiler_params=pltpu.CompilerParams(
            dimension_semantics=("arbitrary",), vmem_limit_bytes=VMEM_LIMIT),
        name="moe_experts",
    )(tile_expert, n_tiles, xs, wg, wu, wd)


def _combine_kernel(pos_ref, x2_ref, rf_ref, g_ref, ys_ref, o_ref, g0, g1, sem):
    step = pl.program_id(0)
    tm = x2_ref.shape[0]
    n_tok = pl.num_programs(0) * tm

    def issue(r, carry):
        t = step * tm + r
        pltpu.make_async_copy(_row(ys_ref, pos_ref[t]), _row(g0, r), sem).start(priority=0)
        pltpu.make_async_copy(_row(ys_ref, pos_ref[n_tok + t]), _row(g1, r), sem).start(priority=1)
        return carry

    lax.fori_loop(0, tm, issue, 0, unroll=8)

    def drain(r, carry):
        for k in range(2):
            pltpu.make_async_copy(_row(ys_ref, 0), _row(g0, 0), sem).wait()
        return carry

    lax.fori_loop(0, tm, drain, 0, unroll=8)

    rf = rf_ref[...]
    y = x2_ref[...] + rf[:, 0:1] * _load_rows(g0, tm) + rf[:, 1:2] * _load_rows(g1, tm)
    o_ref[...] = _rms(y, g_ref[...])


def _combine_call(pos, x2, rf, g, ys):
    T = x2.shape[0]
    tm = MOVE_TILE
    return pl.pallas_call(
        _combine_kernel,
        grid_spec=pltpu.PrefetchScalarGridSpec(
            num_scalar_prefetch=1,
            grid=(T // tm,),
            in_specs=[
                pl.BlockSpec((tm, D_MODEL), lambda i, pos: (i, 0)),
                pl.BlockSpec((tm, 2), lambda i, pos: (i, 0)),
                pl.BlockSpec((1, D_MODEL), lambda i, pos: (0, 0)),
                pl.BlockSpec(memory_space=pl.ANY),
            ],
            out_specs=pl.BlockSpec((tm, D_MODEL), lambda i, pos: (i, 0)),
            scratch_shapes=[pltpu.VMEM((tm * ROW_TILES, LANES), jnp.float32),
                            pltpu.VMEM((tm * ROW_TILES, LANES), jnp.float32),
                            pltpu.SemaphoreType.DMA(())],
        ),
        out_shape=jax.ShapeDtypeStruct((T, D_MODEL), jnp.float32),
        compiler_params=pltpu.CompilerParams(
            dimension_semantics=("arbitrary",), vmem_limit_bytes=VMEM_LIMIT),
        name="moe_combine",
    )(pos, x2, rf, g, ys)


def _rope_tables():
    def cos_sin(dim):
        inv = 1.0 / (ROPE_THETA ** (np.arange(0, dim, 2, dtype=np.float64) / dim))
        ang = np.arange(SEQ, dtype=np.float64)[:, None] * inv[None, :]
        return np.cos(ang), np.sin(ang)

    c16, s16 = cos_sin(MLA_ROPE_DIM)
    c32, s32 = cos_sin(HEAD_DIM)
    ones = np.ones((SEQ, 64))
    zeros = np.zeros((SEQ, 64))
    pad = np.zeros((SEQ, 32))
    cm = np.concatenate([ones, c16, c16, pad], axis=1)
    sm = np.concatenate([zeros, -s16, s16, pad], axis=1)
    cs = np.tile(np.concatenate([c32, c32], axis=1), (1, 2))
    ss = np.tile(np.concatenate([-s32, s32], axis=1), (1, 2))
    cd = np.tile(np.concatenate([c16, c16], axis=1), (1, 4))
    sd = np.tile(np.concatenate([-s16, s16], axis=1), (1, 4))
    return tuple(jnp.asarray(t, jnp.float32) for t in (cm, sm, cs, ss, cd, sd))


def _layer_weights(w_in, w_uq, w_ukv, w_out):
    f32 = jnp.float32
    cq = w_in[:, 0:384]
    ckv = w_in[:, 384:640]
    kr = w_in[:, 640:672]
    sq = w_in[:, 672:1056] * (LOG2E * HEAD_DIM ** -0.5)
    sk = w_in[:, 1056:1184]
    sv = w_in[:, 1184:1312]
    dq = w_in[:, 1312:1568] * (LOG2E * DIFF_QK_DIM ** -0.5)
    dk = w_in[:, 1568:1824]
    dv = w_in[:, 1824:2080]
    kr_pad = jnp.concatenate([jnp.zeros((D_MODEL, 64), f32), kr, jnp.zeros((D_MODEL, 32), f32)], axis=1)
    half = SWA_HEADS // 2
    sq_h = sq.reshape(D_MODEL, SWA_HEADS, HEAD_DIM)
    sq_perm = jnp.stack([sq_h[:, :half], sq_h[:, half:]], axis=2).reshape(D_MODEL, SWA_HEADS * HEAD_DIM)
    w1 = jnp.concatenate([cq, ckv, kr_pad, sq_perm, sk, sv, dq, dk, dv], axis=1).astype(jnp.bfloat16)

    scale = LOG2E * (MLA_NOPE_DIM + MLA_ROPE_DIM) ** -0.5
    uq = (w_uq * scale).reshape(MLA_Q_RANK, MLA_HEADS, MLA_NOPE_DIM + MLA_ROPE_DIM)
    uq = jnp.concatenate([uq, jnp.zeros((MLA_Q_RANK, MLA_HEADS, 32), f32)], axis=2)
    wuq = uq.reshape(MLA_Q_RANK, MLA_HEADS * LANES).astype(jnp.bfloat16)
    ukv = w_ukv.reshape(MLA_KV_RANK, MLA_HEADS, MLA_NOPE_DIM + MLA_V_DIM)
    uk = jnp.concatenate([ukv[:, :, :MLA_NOPE_DIM], jnp.zeros((MLA_KV_RANK, MLA_HEADS, 64), f32)], axis=2)
    uv = ukv[:, :, MLA_NOPE_DIM:]
    wukv = jnp.concatenate([uk.reshape(MLA_KV_RANK, MLA_HEADS * LANES),
                            uv.reshape(MLA_KV_RANK, MLA_HEADS * MLA_V_DIM)], axis=1).astype(jnp.bfloat16)

    wm = w_out[0:384].astype(jnp.bfloat16)
    wo_s = w_out[384:768].reshape(SWA_HEADS, HEAD_DIM, D_MODEL)
    ws = jnp.stack([wo_s[:half], wo_s[half:]], axis=1).reshape(SWA_HEADS * HEAD_DIM, D_MODEL).astype(jnp.bfloat16)
    wd = w_out[768:1024].astype(jnp.bfloat16)
    return w1, wuq, wukv, wm, ws, wd


def _router_weights(w_rg, b_rg, w_re, b_re):
    f32 = jnp.float32
    wr = jnp.concatenate([w_rg, jnp.zeros((D_MODEL, EXPERT_LANE0 - N_GROUPS), f32), w_re,
                          jnp.zeros((D_MODEL, LANES - EXPERT_LANE0 - N_EXPERTS), f32)], axis=1)
    br = jnp.concatenate([b_rg, jnp.zeros((EXPERT_LANE0 - N_GROUPS,), f32), b_re,
                          jnp.zeros((LANES - EXPERT_LANE0 - N_EXPERTS,), f32)])[None, :]
    hi = wr.astype(jnp.bfloat16)
    lo = (wr - hi.astype(f32)).astype(jnp.bfloat16)
    return jnp.concatenate([hi, lo], axis=1), br


def kernel(x, attn_norm, w_in, mla_q_norm, mla_kv_norm, mla_w_uq, mla_w_ukv, swa_sink, diff_lq1, diff_lk1, diff_lq2, diff_lk2, diff_subln, w_out, ffn_norm, router_group, router_group_bias, router_expert, router_expert_bias, w_gate, w_up, w_down, final_norm):
    batch, seq, d = x.shape
    assert (seq, d) == (SEQ, D_MODEL)
    depth = w_in.shape[0]
    n_tok = batch * seq
    tables = _rope_tables()
    wg = w_gate.reshape(depth * N_EXPERTS, D_MODEL, EXPERT_FF)
    wu = w_up.reshape(depth * N_EXPERTS, D_MODEL, EXPERT_FF)
    wdn = w_down.reshape(depth * N_EXPERTS, EXPERT_FF, D_MODEL)
    xt = x.reshape(n_tok, D_MODEL)
    pending = None
    for l in range(depth):
        w1, wuq, wukv, wm, ws, wd = _layer_weights(w_in[l], mla_w_uq[l], mla_w_ukv[l], w_out[l])
        outs = _proj_call(xt, attn_norm[l][None], w1, mla_q_norm[l][None], mla_kv_norm[l][None],
                          wuq, wukv, tables, combine=pending)
        qm, km, vm, qs, ks, vs, qd, kd, vd = outs[:9]
        if pending is not None:
            xt = outs[9]
        o_mla = _mla_call(qm, km, vm, batch)
        o_swa = _swa_call(swa_sink[l] * LOG2E, qs, ks, vs, batch)
        lam_init = 0.8 - 0.6 * math.exp(-0.3 * l)
        subln2 = jnp.concatenate([diff_subln[l], diff_subln[l]])[None]
        o_diff = _diff_call(lam_init, diff_lq1[l][None], diff_lk1[l][None], diff_lq2[l][None],
                            diff_lk2[l][None], subln2, qd, kd, vd, batch)
        wr2, br = _router_weights(router_group[l], router_group_bias[l],
                                  router_expert[l], router_expert_bias[l])
        x2, lt = _outproj_call(xt, o_mla, o_swa, o_diff, wm, ws, wd, ffn_norm[l][None], wr2, br)
        ri, rf, cnt = _route_call(lt)
        max_tiles = (2 * n_tok) // MOE_TILE + N_EXPERTS
        pos8, te8, meta = _plan_call(ri, cnt, max_tiles)
        pos = pos8[0:2].reshape(2 * n_tok)
        fill, n_tiles = meta[:, 0], meta[0, 1:2]
        xs = _push_call(pos, fill, n_tiles, x2, ffn_norm[l][None], max_tiles * MOE_TILE)
        ys = _expert_call(te8[0, :max_tiles], n_tiles, xs, wg, wu, wdn, l)
        xt, pending = x2, (pos, rf[0:2].T, ys)
    pos, rf2, ys = pending
    out = _combine_call(pos, xt, rf2, final_norm[None], ys)
    return out.reshape(batch, seq, d)
```

```python
import functools
import math

import jax
import jax.numpy as jnp
import numpy as np
from jax import lax
from jax.experimental import pallas as pl
from jax.experimental.pallas import tpu as pltpu

D_MODEL = 1024
SEQ = 2048
HEAD_DIM = 64
ROPE_THETA = 10000.0
NORM_EPS = 1e-6
MLA_HEADS = 6
MLA_Q_RANK = 384
MLA_KV_RANK = 256
MLA_NOPE_DIM = 64
MLA_ROPE_DIM = 32
MLA_V_DIM = 64
SWA_HEADS = 6
WINDOW = 128
DIFF_HEADS = 4
DIFF_QK_DIM = 32
DIFF_V_DIM = 64
N_GROUPS = 4
EXPERTS_PER_GROUP = 8
N_EXPERTS = N_GROUPS * EXPERTS_PER_GROUP
EXPERT_FF = 256

LANES = 128
ROW_TILES = D_MODEL // LANES
EXPERT_LANE0 = 8
PROJ_TILE = 512
PROJ_SUB = 256
ATTN_Q_TILE = 2048
DIFF_Q_TILE = 2048
ATTN_SUB = 256
SWA_Q_TILE = 512
MOE_TILE = 256
MOVE_TILE = 1024
EXPERT_RING = 3
ROUTE_CHUNK = 2048
RANK_CHUNK = 512
VMEM_LIMIT = 56 * 1024 * 1024

SEG_CQ = 0
SEG_CKV = SEG_CQ + MLA_Q_RANK
SEG_KR = SEG_CKV + MLA_KV_RANK
SEG_SQ = SEG_KR + LANES
SEG_SK = SEG_SQ + SWA_HEADS * HEAD_DIM
SEG_SV = SEG_SK + LANES
SEG_DQ = SEG_SV + LANES
SEG_DK = SEG_DQ + 2 * LANES
SEG_DV = SEG_DK + 2 * LANES
PROJ_WIDTH = SEG_DV + 2 * LANES

_NT = (((1,), (1,)), ((), ()))
LOG2E = math.log2(math.e)


def _rms(x, g):
    return x * lax.rsqrt(jnp.mean(x * x, axis=-1, keepdims=True) + NORM_EPS) * g


def _load_rows(ref, n):
    return jnp.concatenate([ref[pl.ds(c, n, stride=ROW_TILES), :] for c in range(ROW_TILES)], axis=1)


def _store_rows(ref, val):
    n = val.shape[0]
    for c in range(ROW_TILES):
        ref[pl.ds(c, n, stride=ROW_TILES), :] = val[:, c * LANES:(c + 1) * LANES]


def _rope(x, cos, sin, first_half, shift):
    up = pltpu.roll(x, LANES - shift, 1)
    dn = pltpu.roll(x, shift, 1)
    return x * cos + jnp.where(first_half, up, dn) * sin


def _proj_kernel(x_ref, *refs):
    _proj_body(x_ref[...], *refs)


def _row(ref, idx):
    return ref.at[pl.ds(pl.multiple_of(idx * ROW_TILES, ROW_TILES), ROW_TILES), :]


def _proj_combine_kernel(pos_ref, x2_ref, rf_ref, ys_ref, *refs):
    *body_refs, x3_ref, g0, g1, sem = refs
    step = pl.program_id(0)
    n_steps = pl.num_programs(0)
    tm = x2_ref.shape[0]
    n_tok = n_steps * tm

    def rows(tile, slot, start):
        def body(r, carry):
            if start:
                t = tile * tm + r
                pltpu.make_async_copy(_row(ys_ref, pos_ref[t]), _row(g0.at[slot], r), sem.at[slot]).start(priority=0)
                pltpu.make_async_copy(_row(ys_ref, pos_ref[n_tok + t]), _row(g1.at[slot], r), sem.at[slot]).start(priority=1)
            else:
                for g in (g0, g1):
                    pltpu.make_async_copy(_row(ys_ref, 0), _row(g.at[slot], 0), sem.at[slot]).wait()
            return carry
        lax.fori_loop(0, tm, body, 0, unroll=8)

    @pl.when(step == 0)
    def _():
        rows(0, 0, True)

    @pl.when(step + 1 < n_steps)
    def _():
        rows(step + 1, (step + 1) % 2, True)

    slot = step % 2
    rows(step, slot, False)
    rf = rf_ref[...]
    x = x2_ref[...] + rf[:, 0:1] * _load_rows(g0.at[slot], tm) + rf[:, 1:2] * _load_rows(g1.at[slot], tm)
    x3_ref[...] = x
    _proj_body(x, *body_refs)


def _proj_body(x, g_ref, w1_ref, qn_ref, kvn_ref, wuq_ref, wukv_ref,
               cm_ref, sm_ref, cs_ref, ss_ref, cd_ref, sd_ref,
               qm_ref, km_ref, vm_ref, qs_ref, ks_ref, vs_ref,
               qd_ref, kd_ref, vd_ref):
    lane = lax.broadcasted_iota(jnp.int32, (PROJ_SUB, LANES), 1)
    first_m = (lane >= 64) & (lane < 80)
    first_s = (lane % 64) < 32
    first_d = (lane % 32) < 16
    for sb in range(x.shape[0] // PROJ_SUB):
        rs = slice(sb * PROJ_SUB, (sb + 1) * PROJ_SUB)
        h = _rms(x[rs], g_ref[...]).astype(jnp.bfloat16)
        u = jnp.dot(h, w1_ref[...], preferred_element_type=jnp.float32)
        cm, sm = cm_ref[rs, :], sm_ref[rs, :]
        cs, ss = cs_ref[rs, :], ss_ref[rs, :]
        cd, sd = cd_ref[rs, :], sd_ref[rs, :]

        cq = _rms(u[:, SEG_CQ:SEG_CQ + MLA_Q_RANK], qn_ref[...]).astype(jnp.bfloat16)
        q = jnp.dot(cq, wuq_ref[...], preferred_element_type=jnp.float32)
        ckv = _rms(u[:, SEG_CKV:SEG_CKV + MLA_KV_RANK], kvn_ref[...]).astype(jnp.bfloat16)
        kv = jnp.dot(ckv, wukv_ref[...], preferred_element_type=jnp.float32)
        kr = _rope(u[:, SEG_KR:SEG_KR + LANES], cm, sm, first_m, 16)
        for hd in range(MLA_HEADS):
            sl = slice(hd * LANES, (hd + 1) * LANES)
            qm_ref[rs, sl] = _rope(q[:, sl], cm, sm, first_m, 16).astype(jnp.bfloat16)
            km_ref[rs, sl] = (kv[:, sl] + kr).astype(jnp.bfloat16)
        vm_ref[rs, :] = kv[:, MLA_HEADS * LANES:].astype(jnp.bfloat16)

        for j in range(SWA_HEADS // 2):
            sl = slice(SEG_SQ + j * LANES, SEG_SQ + (j + 1) * LANES)
            qs_ref[rs, j * LANES:(j + 1) * LANES] = _rope(u[:, sl], cs, ss, first_s, 32).astype(jnp.bfloat16)
        ks_ref[rs, :] = _rope(u[:, SEG_SK:SEG_SK + LANES], cs, ss, first_s, 32).astype(jnp.bfloat16)
        vs_ref[rs, :] = u[:, SEG_SV:SEG_SV + LANES].astype(jnp.bfloat16)

        for j in range(2):
            sq = slice(SEG_DQ + j * LANES, SEG_DQ + (j + 1) * LANES)
            sk = slice(SEG_DK + j * LANES, SEG_DK + (j + 1) * LANES)
            qd_ref[rs, j * LANES:(j + 1) * LANES] = _rope(u[:, sq], cd, sd, first_d, 16).astype(jnp.bfloat16)
            kd_ref[rs, j * LANES:(j + 1) * LANES] = _rope(u[:, sk], cd, sd, first_d, 16).astype(jnp.bfloat16)
        vd_ref[rs, :] = u[:, SEG_DV:SEG_DV + 2 * LANES].astype(jnp.bfloat16)


def _proj_call(x, g, w1, qn, kvn, wuq, wukv, tables, combine=None):
    T = x.shape[0]
    tm = PROJ_TILE
    n_seq = SEQ // tm
    tok = lambda i, *_: (i, 0)
    const = lambda i, *_: (0, 0)
    tab = lambda i, *_: (i % n_seq, 0)
    widths = (6 * LANES, 6 * LANES, 3 * LANES, 3 * LANES, LANES, LANES,
              2 * LANES, 2 * LANES, 2 * LANES)
    weight_specs = [
        pl.BlockSpec((1, D_MODEL), const),
        pl.BlockSpec((D_MODEL, PROJ_WIDTH), const),
        pl.BlockSpec((1, MLA_Q_RANK), const),
        pl.BlockSpec((1, MLA_KV_RANK), const),
        pl.BlockSpec((MLA_Q_RANK, 6 * LANES), const),
        pl.BlockSpec((MLA_KV_RANK, 9 * LANES), const),
    ] + [pl.BlockSpec((tm, LANES), tab)] * 6
    out_specs = [pl.BlockSpec((tm, w), tok) for w in widths]
    out_shape = [jax.ShapeDtypeStruct((T, w), jnp.bfloat16) for w in widths]
    params = pltpu.CompilerParams(dimension_semantics=("arbitrary",), vmem_limit_bytes=VMEM_LIMIT)
    weights = (g, w1, qn, kvn, wuq, wukv, *tables)
    if combine is None:
        return pl.pallas_call(
            _proj_kernel,
            grid=(T // tm,),
            in_specs=[pl.BlockSpec((tm, D_MODEL), tok)] + weight_specs,
            out_specs=out_specs, out_shape=out_shape, compiler_params=params, name="proj",
        )(x, *weights)
    pos, rf, ys = combine
    return pl.pallas_call(
        _proj_combine_kernel,
        grid_spec=pltpu.PrefetchScalarGridSpec(
            num_scalar_prefetch=1,
            grid=(T // tm,),
            in_specs=[pl.BlockSpec((tm, D_MODEL), tok), pl.BlockSpec((tm, 2), tok),
                      pl.BlockSpec(memory_space=pl.ANY)] + weight_specs,
            out_specs=out_specs + [pl.BlockSpec((tm, D_MODEL), tok)],
            scratch_shapes=[pltpu.VMEM((2, tm * ROW_TILES, LANES), jnp.float32),
                            pltpu.VMEM((2, tm * ROW_TILES, LANES), jnp.float32),
                            pltpu.SemaphoreType.DMA((2,))],
        ),
        out_shape=out_shape + [jax.ShapeDtypeStruct((T, D_MODEL), jnp.float32)],
        compiler_params=params, name="proj_combine",
    )(pos, x, rf, ys, *weights)


def _fill_values(vx_ref, v_ref):
    lane = lax.broadcasted_iota(jnp.int32, v_ref.shape, 1)
    vx_ref[:, :LANES] = v_ref[...]
    vx_ref[:, LANES:] = jnp.where(lane == 0, 1.0, 0.0).astype(vx_ref.dtype)


def _softmax_pv(q, k, vx):
    s = lax.dot_general(q, k, _NT, preferred_element_type=jnp.float32)
    m = jnp.max(s, axis=-1, keepdims=True)
    p = jnp.exp2(s - m)
    o = jnp.dot(p.astype(jnp.bfloat16), vx, preferred_element_type=jnp.float32)
    return o[:, :LANES] / o[:, LANES:LANES + 1]


def _mla_kernel(q_ref, k_ref, v_ref, o_ref, vx_ref):
    _fill_values(vx_ref, v_ref)
    lane = lax.broadcasted_iota(jnp.int32, (ATTN_SUB, LANES), 1)
    for sb in range(ATTN_Q_TILE // ATTN_SUB):
        rows = slice(sb * ATTN_SUB, (sb + 1) * ATTN_SUB)
        o0 = _softmax_pv(q_ref[rows, :LANES], k_ref[:, :LANES], vx_ref[...])
        o1 = _softmax_pv(q_ref[rows, LANES:], k_ref[:, LANES:], vx_ref[...])
        o_ref[rows, :] = jnp.where(lane < MLA_V_DIM, o0, o1).astype(o_ref.dtype)


def _mla_call(q, k, v, batch):
    T = q.shape[0]
    tq = ATTN_Q_TILE
    nq = SEQ // tq
    return pl.pallas_call(
        _mla_kernel,
        grid=(batch, MLA_HEADS // 2, nq),
        in_specs=[
            pl.BlockSpec((tq, 2 * LANES), lambda b, p, i: (b * nq + i, p)),
            pl.BlockSpec((SEQ, 2 * LANES), lambda b, p, i: (b, p)),
            pl.BlockSpec((SEQ, LANES), lambda b, p, i: (b, p)),
        ],
        out_specs=pl.BlockSpec((tq, LANES), lambda b, p, i: (b * nq + i, p)),
        out_shape=jax.ShapeDtypeStruct((T, MLA_HEADS * MLA_V_DIM), jnp.bfloat16),
        scratch_shapes=[pltpu.VMEM((SEQ, 2 * LANES), jnp.bfloat16)],
        compiler_params=pltpu.CompilerParams(
            dimension_semantics=("arbitrary",) * 3, vmem_limit_bytes=VMEM_LIMIT),
        name="mla_attn",
    )(q, k, v)


def _swa_kernel(sink_ref, q_ref, k_ref, v_ref, o_ref):
    i = pl.program_id(1)
    nb = SEQ // WINDOW
    half = SWA_HEADS // 2
    rows = SWA_HEADS * WINDOW
    lane = lax.broadcasted_iota(jnp.int32, (WINDOW, LANES), 1)
    left = lane < HEAD_DIM
    row = lax.broadcasted_iota(jnp.int32, (rows, 3 * WINDOW), 0)
    col = lax.broadcasted_iota(jnp.int32, (rows, 3 * WINDOW), 1)
    rel = col - row % WINDOW
    band = (rel >= 0) & (rel <= 2 * WINDOW)
    head_col = lax.broadcasted_iota(jnp.int32, (rows, 1), 0) // WINDOW
    sink = jnp.zeros((rows, 1), jnp.float32)
    for hd in range(SWA_HEADS):
        sink = jnp.where(head_col == hd, sink_ref[hd], sink)
    for sub in range(SWA_Q_TILE // WINDOW):
        n = i * (SWA_Q_TILE // WINDOW) + sub
        prev = pl.multiple_of(jnp.maximum(n - 1, 0) * WINDOW, WINDOW)
        cur = pl.multiple_of(n * WINDOW, WINDOW)
        nxt = pl.multiple_of(jnp.minimum(n + 1, nb - 1) * WINDOW, WINDOW)
        kb = jnp.concatenate([k_ref[pl.ds(prev, WINDOW), :], k_ref[pl.ds(cur, WINDOW), :],
                              k_ref[pl.ds(nxt, WINDOW), :]], axis=0)
        vb = jnp.concatenate([v_ref[pl.ds(prev, WINDOW), :], v_ref[pl.ds(cur, WINDOW), :],
                              v_ref[pl.ds(nxt, WINDOW), :]], axis=0)
        lo_col = jnp.where(n > 0, 0, WINDOW)
        hi_col = jnp.where(n < nb - 1, 3 * WINDOW, 2 * WINDOW)
        valid = band & (col >= lo_col) & (col < hi_col)
        parts = [q_ref[sub * WINDOW:(sub + 1) * WINDOW, j * LANES:(j + 1) * LANES] for j in range(half)]
        zero = jnp.zeros_like(parts[0])
        qs = jnp.concatenate([jnp.where(left, p_, zero) for p_ in parts]
                             + [jnp.where(left, zero, p_) for p_ in parts], axis=0)
        s = lax.dot_general(qs, kb, _NT, preferred_element_type=jnp.float32)
        s = jnp.where(valid, s, -jnp.inf)
        m = jnp.maximum(jnp.max(s, axis=-1, keepdims=True), sink)
        p = jnp.exp2(s - m)
        l = jnp.sum(p, axis=-1, keepdims=True) + jnp.exp2(sink - m)
        o = jnp.dot(p.astype(jnp.bfloat16), vb, preferred_element_type=jnp.float32) / l
        for j in range(half):
            o_ref[sub * WINDOW:(sub + 1) * WINDOW, j * LANES:(j + 1) * LANES] = jnp.where(
                left, o[j * WINDOW:(j + 1) * WINDOW], o[(j + half) * WINDOW:(j + half + 1) * WINDOW]
            ).astype(o_ref.dtype)


def _swa_call(sink, q, k, v, batch):
    T = q.shape[0]
    tq = SWA_Q_TILE
    nq = SEQ // tq
    return pl.pallas_call(
        _swa_kernel,
        grid=(batch, nq),
        in_specs=[
            pl.BlockSpec(memory_space=pltpu.SMEM),
            pl.BlockSpec((tq, 3 * LANES), lambda b, i: (b * nq + i, 0)),
            pl.BlockSpec((SEQ, LANES), lambda b, i: (b, 0)),
            pl.BlockSpec((SEQ, LANES), lambda b, i: (b, 0)),
        ],
        out_specs=pl.BlockSpec((tq, 3 * LANES), lambda b, i: (b * nq + i, 0)),
        out_shape=jax.ShapeDtypeStruct((T, SWA_HEADS * HEAD_DIM), jnp.bfloat16),
        compiler_params=pltpu.CompilerParams(
            dimension_semantics=("arbitrary",) * 2, vmem_limit_bytes=VMEM_LIMIT),
        name="swa_attn",
    )(sink, q, k, v)


def _diff_kernel(lam_init, lq1_ref, lk1_ref, lq2_ref, lk2_ref, subln_ref,
                 q_ref, k_ref, v_ref, o_ref, vx_ref):
    lam = (jnp.exp(jnp.sum(lq1_ref[...] * lk1_ref[...], axis=-1, keepdims=True))
           - jnp.exp(jnp.sum(lq2_ref[...] * lk2_ref[...], axis=-1, keepdims=True))
           + lam_init)
    _fill_values(vx_ref, v_ref)
    lane = lax.broadcasted_iota(jnp.int32, (ATTN_SUB, LANES), 1)
    left = lane < DIFF_V_DIM
    for sb in range(DIFF_Q_TILE // ATTN_SUB):
        rows = slice(sb * ATTN_SUB, (sb + 1) * ATTN_SUB)
        q = q_ref[rows, :]
        zero = jnp.zeros_like(q)
        outs = []
        for side in range(2):
            comp = []
            for c in range(2):
                lo = side * DIFF_V_DIM + c * DIFF_QK_DIM
                qc = jnp.where((lane >= lo) & (lane < lo + DIFF_QK_DIM), q, zero)
                comp.append(_softmax_pv(qc, k_ref[...], vx_ref[...]))
            outs.append(comp[0] - lam * comp[1])
        o = jnp.where(left, outs[0], outs[1])
        sq = o * o
        ms_l = jnp.sum(jnp.where(left, sq, 0.0), axis=-1, keepdims=True) * (1.0 / DIFF_V_DIM)
        ms_r = jnp.sum(jnp.where(left, 0.0, sq), axis=-1, keepdims=True) * (1.0 / DIFF_V_DIM)
        r = jnp.where(left, lax.rsqrt(ms_l + NORM_EPS), lax.rsqrt(ms_r + NORM_EPS))
        o_ref[rows, :] = (o * r * subln_ref[...] * (1.0 - lam_init)).astype(o_ref.dtype)


def _diff_call(lam_init, lq1, lk1, lq2, lk2, subln2, q, k, v, batch):
    T = q.shape[0]
    tq = DIFF_Q_TILE
    nq = SEQ // tq
    small = lambda b, p, i: (0, 0)
    return pl.pallas_call(
        functools.partial(_diff_kernel, lam_init),
        grid=(batch, DIFF_HEADS // 2, nq),
        in_specs=[pl.BlockSpec((1, DIFF_QK_DIM), small)] * 4 + [
            pl.BlockSpec((1, LANES), small),
            pl.BlockSpec((tq, LANES), lambda b, p, i: (b * nq + i, p)),
            pl.BlockSpec((SEQ, LANES), lambda b, p, i: (b, p)),
            pl.BlockSpec((SEQ, LANES), lambda b, p, i: (b, p)),
        ],
        out_specs=pl.BlockSpec((tq, LANES), lambda b, p, i: (b * nq + i, p)),
        out_shape=jax.ShapeDtypeStruct((T, DIFF_HEADS * DIFF_V_DIM), jnp.bfloat16),
        scratch_shapes=[pltpu.VMEM((SEQ, 2 * LANES), jnp.bfloat16)],
        compiler_params=pltpu.CompilerParams(
            dimension_semantics=("arbitrary",) * 3, vmem_limit_bytes=VMEM_LIMIT),
        name="diff_attn",
    )(lq1, lk1, lq2, lk2, subln2, q, k, v)


def _outproj_kernel(x_ref, om_ref, os_ref, od_ref, wm_ref, ws_ref, wd_ref, g_ref,
                    wr2_ref, br_ref, x2_ref, lt_ref):
    x2 = (x_ref[...]
          + jnp.dot(om_ref[...], wm_ref[...], preferred_element_type=jnp.float32)
          + jnp.dot(os_ref[...], ws_ref[...], preferred_element_type=jnp.float32)
          + jnp.dot(od_ref[...], wd_ref[...], preferred_element_type=jnp.float32))
    x2_ref[...] = x2
    h2 = _rms(x2, g_ref[...])

    hi = h2.astype(jnp.bfloat16)
    lo = (h2 - hi.astype(jnp.float32)).astype(jnp.bfloat16)
    a = jnp.dot(hi, wr2_ref[...], preferred_element_type=jnp.float32)
    b = jnp.dot(lo, wr2_ref[:, :LANES], preferred_element_type=jnp.float32)
    logits = a[:, :LANES] + a[:, LANES:] + b + br_ref[...]
    lt_ref[...] = logits.T


def _outproj_call(x, om, os_, od, wm, ws, wd, g, wr2, br):
    T = x.shape[0]
    tm = PROJ_TILE
    tok = lambda i: (i, 0)
    const = lambda i: (0, 0)
    return pl.pallas_call(
        _outproj_kernel,
        grid=(T // tm,),
        in_specs=[
            pl.BlockSpec((tm, D_MODEL), tok),
            pl.BlockSpec((tm, 3 * LANES), tok),
            pl.BlockSpec((tm, 3 * LANES), tok),
            pl.BlockSpec((tm, 2 * LANES), tok),
            pl.BlockSpec((3 * LANES, D_MODEL), const),
            pl.BlockSpec((3 * LANES, D_MODEL), const),
            pl.BlockSpec((2 * LANES, D_MODEL), const),
            pl.BlockSpec((1, D_MODEL), const),
            pl.BlockSpec((D_MODEL, 2 * LANES), const),
            pl.BlockSpec((1, LANES), const),
        ],
        out_specs=[
            pl.BlockSpec((tm, D_MODEL), tok),
            pl.BlockSpec((LANES, tm), lambda i: (0, i)),
        ],
        out_shape=[
            jax.ShapeDtypeStruct((T, D_MODEL), jnp.float32),
            jax.ShapeDtypeStruct((LANES, T), jnp.float32),
        ],
        compiler_params=pltpu.CompilerParams(
            dimension_semantics=("arbitrary",), vmem_limit_bytes=VMEM_LIMIT),
        name="outproj",
    )(x, om, os_, od, wm, ws, wd, g, wr2, br)


def _route_kernel(lt_ref, ri_ref, rf_ref, cnt_ref, carry_ref):
    step = pl.program_id(0)

    @pl.when(step == 0)
    def _():
        carry_ref[...] = jnp.zeros_like(carry_ref)

    tc = lt_ref.shape[1]
    neg = -jnp.inf
    row8 = lax.broadcasted_iota(jnp.int32, (8, tc), 0)
    is_g = row8 < N_GROUPS
    gl = jnp.where(is_g, lt_ref[0:8, :], neg)
    gmax = jnp.max(gl, axis=0, keepdims=True)
    gidx = jnp.min(jnp.where(gl == gmax, row8, 8), axis=0, keepdims=True)
    gsum = jnp.sum(jnp.where(is_g, jnp.exp(gl - gmax), 0.0), axis=0, keepdims=True)
    g_gate = 1.0 / gsum
    gidx8 = jnp.broadcast_to(gidx, (8, tc))
    e_in = lt_ref[EXPERT_LANE0:EXPERT_LANE0 + 8, :]
    for g in range(1, N_GROUPS):
        lo = EXPERT_LANE0 + g * EXPERTS_PER_GROUP
        e_in = jnp.where(gidx8 == g, lt_ref[lo:lo + EXPERTS_PER_GROUP, :], e_in)
    m1 = jnp.max(e_in, axis=0, keepdims=True)
    i1 = jnp.min(jnp.where(e_in == m1, row8, 8), axis=0, keepdims=True)
    e_rest = jnp.where(row8 == i1, neg, e_in)
    m2 = jnp.max(e_rest, axis=0, keepdims=True)
    i2 = jnp.min(jnp.where(e_rest == m2, row8, 8), axis=0, keepdims=True)
    t = jnp.exp(m2 - m1)
    w1 = g_gate / (1.0 + t)
    w2 = g_gate * t / (1.0 + t)
    e1 = gidx * EXPERTS_PER_GROUP + i1
    e2 = gidx * EXPERTS_PER_GROUP + i2

    rowe = lax.broadcasted_iota(jnp.int32, (N_EXPERTS, tc), 0)
    hit1 = rowe == e1
    hit2 = rowe == e2
    onehot = jnp.where(hit1 | hit2, 1.0, 0.0)
    r = lax.broadcasted_iota(jnp.int32, (RANK_CHUNK, RANK_CHUNK), 0)
    c = lax.broadcasted_iota(jnp.int32, (RANK_CHUNK, RANK_CHUNK), 1)
    upper = jnp.where(r < c, 1.0, 0.0).astype(jnp.bfloat16)
    carry = carry_ref[:, 0:1]
    parts = []
    for j in range(tc // RANK_CHUNK):
        oh = onehot[:, j * RANK_CHUNK:(j + 1) * RANK_CHUNK]
        parts.append(jnp.dot(oh.astype(jnp.bfloat16), upper, preferred_element_type=jnp.float32) + carry)
        carry = carry + jnp.sum(oh, axis=1, keepdims=True)
    before = jnp.concatenate(parts, axis=1)
    rank1 = jnp.sum(jnp.where(hit1, before, 0.0), axis=0, keepdims=True).astype(jnp.int32)
    rank2 = jnp.sum(jnp.where(hit2, before, 0.0), axis=0, keepdims=True).astype(jnp.int32)
    carry_ref[...] = jnp.broadcast_to(carry, carry_ref.shape)
    cnt_ref[...] = jnp.broadcast_to(carry, cnt_ref.shape)

    ri_ref[...] = jnp.where(row8 == 0, e1, jnp.where(row8 == 1, e2, jnp.where(
        row8 == 2, rank1, jnp.where(row8 == 3, rank2, 0))))
    rf_ref[...] = jnp.where(row8 == 0, w1, jnp.where(row8 == 1, w2, 0.0))


def _route_call(lt):
    T = lt.shape[1]
    tc = ROUTE_CHUNK
    return pl.pallas_call(
        _route_kernel,
        grid=(T // tc,),
        in_specs=[pl.BlockSpec((LANES, tc), lambda i: (0, i))],
        out_specs=[
            pl.BlockSpec((8, tc), lambda i: (0, i)),
            pl.BlockSpec((8, tc), lambda i: (0, i)),
            pl.BlockSpec((N_EXPERTS, LANES), lambda i: (0, 0)),
        ],
        out_shape=[
            jax.ShapeDtypeStruct((8, T), jnp.int32),
            jax.ShapeDtypeStruct((8, T), jnp.float32),
            jax.ShapeDtypeStruct((N_EXPERTS, LANES), jnp.float32),
        ],
        scratch_shapes=[pltpu.VMEM((N_EXPERTS, LANES), jnp.float32)],
        compiler_params=pltpu.CompilerParams(
            dimension_semantics=("arbitrary",), vmem_limit_bytes=VMEM_LIMIT),
        name="route",
    )(lt)


def _plan_kernel(ri_ref, cnt_ref, pos_ref, te_ref, meta_ref):
    T = ri_ref.shape[1]
    counts = cnt_ref[...].astype(jnp.int32)
    padded = (counts + (MOE_TILE - 1)) // MOE_TILE * MOE_TILE
    r = lax.broadcasted_iota(jnp.int32, (N_EXPERTS, N_EXPERTS), 0)
    c = lax.broadcasted_iota(jnp.int32, (N_EXPERTS, N_EXPERTS), 1)
    lower = jnp.where(c < r, 1.0, 0.0)
    off = jnp.dot(lower, padded.astype(jnp.float32), preferred_element_type=jnp.float32,
                  precision=lax.Precision.HIGHEST).astype(jnp.int32)
    ends = off + padded
    total = jnp.max(ends, axis=0, keepdims=True)

    rowe = lax.broadcasted_iota(jnp.int32, (N_EXPERTS, T), 0)
    off_col = off[:, 0:1]
    row8 = lax.broadcasted_iota(jnp.int32, (8, T), 0)
    pos1 = jnp.sum(jnp.where(rowe == ri_ref[0:1, :], off_col, 0), axis=0, keepdims=True) + ri_ref[2:3, :]
    pos2 = jnp.sum(jnp.where(rowe == ri_ref[1:2, :], off_col, 0), axis=0, keepdims=True) + ri_ref[3:4, :]
    pos_ref[...] = jnp.where(row8 == 0, pos1, jnp.where(row8 == 1, pos2, 0))

    n_lanes = te_ref.shape[1]
    start = lax.broadcasted_iota(jnp.int32, (N_EXPERTS, n_lanes), 1) * MOE_TILE
    start = jnp.minimum(start, total[:, 0:1] - MOE_TILE)
    te = jnp.sum((ends[:, 0:1] <= start).astype(jnp.int32), axis=0, keepdims=True)
    te_ref[...] = jnp.broadcast_to(te, te_ref.shape)
    lane = lax.broadcasted_iota(jnp.int32, (N_EXPERTS, LANES), 1)
    fill = jnp.where(padded > 0, ends - MOE_TILE, -1)
    meta_ref[...] = jnp.where(lane == 0, fill, total // MOE_TILE)


def _plan_call(ri, cnt, max_tiles):
    T = ri.shape[1]
    n_lanes = -(-max_tiles // LANES) * LANES
    full = lambda i: (0, 0)
    return pl.pallas_call(
        _plan_kernel,
        grid=(1,),
        in_specs=[pl.BlockSpec((8, T), full), pl.BlockSpec((N_EXPERTS, LANES), full)],
        out_specs=[pl.BlockSpec((8, T), full), pl.BlockSpec((8, n_lanes), full),
                   pl.BlockSpec((N_EXPERTS, LANES), full)],
        out_shape=[jax.ShapeDtypeStruct((8, T), jnp.int32),
                   jax.ShapeDtypeStruct((8, n_lanes), jnp.int32),
                   jax.ShapeDtypeStruct((N_EXPERTS, LANES), jnp.int32)],
        compiler_params=pltpu.CompilerParams(
            dimension_semantics=("arbitrary",), vmem_limit_bytes=VMEM_LIMIT),
        name="slot_plan",
    )(ri, cnt)


def _push_kernel(pos_ref, fill_ref, nt_ref, x2_ref, g_ref, xs_ref, h_ref, zero_ref, sem):
    step = pl.program_id(0)
    tm = x2_ref.shape[0]
    n_tok = pl.num_programs(0) * tm
    _store_rows(h_ref, _rms(x2_ref[...], g_ref[...]))
    fill_rows = MOE_TILE * ROW_TILES
    max_tiles = xs_ref.shape[0] // fill_rows

    def fill_copy(slot):
        start = pl.multiple_of(slot * ROW_TILES, fill_rows)
        return pltpu.make_async_copy(zero_ref, xs_ref.at[pl.ds(start, fill_rows), :], sem)

    @pl.when(step == 0)
    def _():
        zero_ref[...] = jnp.zeros_like(zero_ref)
        for e in range(N_EXPERTS):
            @pl.when(fill_ref[e] >= 0)
            def _():
                fill_copy(fill_ref[e]).start()

        def tail_start(j, carry):
            fill_copy(j * MOE_TILE).start()
            return carry

        def tail_wait(j, carry):
            fill_copy(j * MOE_TILE).wait()
            return carry

        lax.fori_loop(nt_ref[0], max_tiles, tail_start, 0)
        for e in range(N_EXPERTS):
            @pl.when(fill_ref[e] >= 0)
            def _():
                fill_copy(fill_ref[e]).wait()
        lax.fori_loop(nt_ref[0], max_tiles, tail_wait, 0)

    def issue(r, carry):
        t = step * tm + r
        for k in range(2):
            pltpu.make_async_copy(_row(h_ref, r), _row(xs_ref, pos_ref[k * n_tok + t]), sem).start(priority=k)
        return carry

    lax.fori_loop(0, tm, issue, 0, unroll=8)

    def drain(r, carry):
        for k in range(2):
            pltpu.make_async_copy(_row(h_ref, 0), _row(xs_ref, 0), sem).wait()
        return carry

    lax.fori_loop(0, tm, drain, 0, unroll=8)


def _push_call(pos, fill, n_tiles, x2, g, n_slots):
    tm = MOVE_TILE
    n_tok = x2.shape[0]
    return pl.pallas_call(
        _push_kernel,
        grid_spec=pltpu.PrefetchScalarGridSpec(
            num_scalar_prefetch=3,
            grid=(n_tok // tm,),
            in_specs=[pl.BlockSpec((tm, D_MODEL), lambda i, pos, fill, nt: (i, 0)),
                      pl.BlockSpec((1, D_MODEL), lambda i, pos, fill, nt: (0, 0))],
            out_specs=pl.BlockSpec(memory_space=pl.ANY),
            scratch_shapes=[pltpu.VMEM((tm * ROW_TILES, LANES), jnp.float32),
                            pltpu.VMEM((MOE_TILE * ROW_TILES, LANES), jnp.float32),
                            pltpu.SemaphoreType.DMA(())],
        ),
        out_shape=jax.ShapeDtypeStruct((n_slots * ROW_TILES, LANES), jnp.float32),
        compiler_params=pltpu.CompilerParams(
            dimension_semantics=("arbitrary",), vmem_limit_bytes=VMEM_LIMIT),
        name="moe_push",
    )(pos, fill, n_tiles, x2, g)


def _expert_kernel(te_ref, nt_ref, xs_ref, wg_ref, wu_ref, wd_ref, ys_ref,
                   ring, obuf, wg_s, wu_s, wd_s, sem, osem):
    j = pl.program_id(0)
    nt = nt_ref[0]
    rows = MOE_TILE * ROW_TILES

    def tile_copy(tile):
        slot = tile % EXPERT_RING
        return pltpu.make_async_copy(xs_ref.at[pl.ds(pl.multiple_of(tile * rows, rows), rows), :],
                                     ring.at[slot], sem.at[slot])

    @pl.when(j == 0)
    def _():
        tile_copy(0).start()

        @pl.when(nt > 1)
        def _():
            tile_copy(1).start()

    @pl.when(j + 2 < nt)
    def _():
        tile_copy(j + 2).start()

    def out_copy(tile):
        slot = tile % 2
        return pltpu.make_async_copy(obuf.at[slot],
                                     ys_ref.at[pl.ds(pl.multiple_of(tile * rows, rows), rows), :],
                                     osem.at[slot])

    @pl.when(j < nt)
    def _():
        tile_copy(j).wait()

        @pl.when(j >= 2)
        def _():
            out_copy(j - 2).wait()

        changed = (j == 0) | (te_ref[j] != te_ref[jnp.maximum(j - 1, 0)])

        @pl.when(changed)
        def _():
            wg_s[...] = wg_ref[...].astype(jnp.bfloat16)
            wu_s[...] = wu_ref[...].astype(jnp.bfloat16)
            wd_s[...] = wd_ref[...].astype(jnp.bfloat16)

        xb = _load_rows(ring.at[j % EXPERT_RING], MOE_TILE).astype(jnp.bfloat16)
        a = jnp.dot(xb, wg_s[...], preferred_element_type=jnp.float32)
        u = jnp.dot(xb, wu_s[...], preferred_element_type=jnp.float32)
        hid = (a * jax.nn.sigmoid(a) * u).astype(jnp.bfloat16)
        _store_rows(obuf.at[j % 2], jnp.dot(hid, wd_s[...], preferred_element_type=jnp.float32))
        out_copy(j).start()

        @pl.when(j == nt - 1)
        def _():
            out_copy(j).wait()

            @pl.when(j >= 1)
            def _():
                out_copy(j - 1).wait()


def _expert_call(tile_expert, n_tiles, xs, wg, wu, wd, layer):
    n_slots = xs.shape[0] // ROW_TILES
    max_tiles = n_slots // MOE_TILE
    base = layer * N_EXPERTS
    wsel = lambda j, te, nt: (base + te[j], 0, 0)
    return pl.pallas_call(
        _expert_kernel,
        grid_spec=pltpu.PrefetchScalarGridSpec(
            num_scalar_prefetch=2,
            grid=(max_tiles,),
            in_specs=[
                pl.BlockSpec(memory_space=pl.ANY),
                pl.BlockSpec((None, D_MODEL, EXPERT_FF), wsel),
                pl.BlockSpec((None, D_MODEL, EXPERT_FF), wsel),
                pl.BlockSpec((None, EXPERT_FF, D_MODEL), wsel),
            ],
            out_specs=pl.BlockSpec(memory_space=pl.ANY),
            scratch_shapes=[pltpu.VMEM((EXPERT_RING, MOE_TILE * ROW_TILES, LANES), jnp.float32),
                            pltpu.VMEM((2, MOE_TILE * ROW_TILES, LANES), jnp.float32),
                            pltpu.VMEM((D_MODEL, EXPERT_FF), jnp.bfloat16),
                            pltpu.VMEM((D_MODEL, EXPERT_FF), jnp.bfloat16),
                            pltpu.VMEM((EXPERT_FF, D_MODEL), jnp.bfloat16),
                            pltpu.SemaphoreType.DMA((EXPERT_RING,)),
                            pltpu.SemaphoreType.DMA((2,))],
        ),
        out_shape=jax.ShapeDtypeStruct((n_slots * ROW_TILES, LANES), jnp.float32),
        input_output_aliases={2: 0},
        compiler_params=pltpu.CompilerParams(
            dimension_semantics=("arbitrary",), vmem_limit_bytes=VMEM_LIMIT),
        name="moe_experts",
    )(tile_expert, n_tiles, xs, wg, wu, wd)


def _combine_kernel(pos_ref, x2_ref, rf_ref, g_ref, ys_ref, o_ref, g0, g1, sem):
    step = pl.program_id(0)
    tm = x2_ref.shape[0]
    n_tok = pl.num_programs(0) * tm

    def issue(r, carry):
        t = step * tm + r
        pltpu.make_async_copy(_row(ys_ref, pos_ref[t]), _row(g0, r), sem).start(priority=0)
        pltpu.make_async_copy(_row(ys_ref, pos_ref[n_tok + t]), _row(g1, r), sem).start(priority=1)
        return carry

    lax.fori_loop(0, tm, issue, 0, unroll=8)

    def drain(r, carry):
        for k in range(2):
            pltpu.make_async_copy(_row(ys_ref, 0), _row(g0, 0), sem).wait()
        return carry

    lax.fori_loop(0, tm, drain, 0, unroll=8)

    rf = rf_ref[...]
    y = x2_ref[...] + rf[:, 0:1] * _load_rows(g0, tm) + rf[:, 1:2] * _load_rows(g1, tm)
    o_ref[...] = _rms(y, g_ref[...])


def _combine_call(pos, x2, rf, g, ys):
    T = x2.shape[0]
    tm = MOVE_TILE
    return pl.pallas_call(
        _combine_kernel,
        grid_spec=pltpu.PrefetchScalarGridSpec(
            num_scalar_prefetch=1,
            grid=(T // tm,),
            in_specs=[
                pl.BlockSpec((tm, D_MODEL), lambda i, pos: (i, 0)),
                pl.BlockSpec((tm, 2), lambda i, pos: (i, 0)),
                pl.BlockSpec((1, D_MODEL), lambda i, pos: (0, 0)),
                pl.BlockSpec(memory_space=pl.ANY),
            ],
            out_specs=pl.BlockSpec((tm, D_MODEL), lambda i, pos: (i, 0)),
            scratch_shapes=[pltpu.VMEM((tm * ROW_TILES, LANES), jnp.float32),
                            pltpu.VMEM((tm * ROW_TILES, LANES), jnp.float32),
                            pltpu.SemaphoreType.DMA(())],
        ),
        out_shape=jax.ShapeDtypeStruct((T, D_MODEL), jnp.float32),
        compiler_params=pltpu.CompilerParams(
            dimension_semantics=("arbitrary",), vmem_limit_bytes=VMEM_LIMIT),
        name="moe_combine",
    )(pos, x2, rf, g, ys)


def _rope_tables():
    def cos_sin(dim):
        inv = 1.0 / (ROPE_THETA ** (np.arange(0, dim, 2, dtype=np.float64) / dim))
        ang = np.arange(SEQ, dtype=np.float64)[:, None] * inv[None, :]
        return np.cos(ang), np.sin(ang)

    c16, s16 = cos_sin(MLA_ROPE_DIM)
    c32, s32 = cos_sin(HEAD_DIM)
    ones = np.ones((SEQ, 64))
    zeros = np.zeros((SEQ, 64))
    pad = np.zeros((SEQ, 32))
    cm = np.concatenate([ones, c16, c16, pad], axis=1)
    sm = np.concatenate([zeros, -s16, s16, pad], axis=1)
    cs = np.tile(np.concatenate([c32, c32], axis=1), (1, 2))
    ss = np.tile(np.concatenate([-s32, s32], axis=1), (1, 2))
    cd = np.tile(np.concatenate([c16, c16], axis=1), (1, 4))
    sd = np.tile(np.concatenate([-s16, s16], axis=1), (1, 4))
    return tuple(jnp.asarray(t, jnp.float32) for t in (cm, sm, cs, ss, cd, sd))


def _layer_weights(w_in, w_uq, w_ukv, w_out):
    f32 = jnp.float32
    cq = w_in[:, 0:384]
    ckv = w_in[:, 384:640]
    kr = w_in[:, 640:672]
    sq = w_in[:, 672:1056] * (LOG2E * HEAD_DIM ** -0.5)
    sk = w_in[:, 1056:1184]
    sv = w_in[:, 1184:1312]
    dq = w_in[:, 1312:1568] * (LOG2E * DIFF_QK_DIM ** -0.5)
    dk = w_in[:, 1568:1824]
    dv = w_in[:, 1824:2080]
    kr_pad = jnp.concatenate([jnp.zeros((D_MODEL, 64), f32), kr, jnp.zeros((D_MODEL, 32), f32)], axis=1)
    half = SWA_HEADS // 2
    sq_h = sq.reshape(D_MODEL, SWA_HEADS, HEAD_DIM)
    sq_perm = jnp.stack([sq_h[:, :half], sq_h[:, half:]], axis=2).reshape(D_MODEL, SWA_HEADS * HEAD_DIM)
    w1 = jnp.concatenate([cq, ckv, kr_pad, sq_perm, sk, sv, dq, dk, dv], axis=1).astype(jnp.bfloat16)

    scale = LOG2E * (MLA_NOPE_DIM + MLA_ROPE_DIM) ** -0.5
    uq = (w_uq * scale).reshape(MLA_Q_RANK, MLA_HEADS, MLA_NOPE_DIM + MLA_ROPE_DIM)
    uq = jnp.concatenate([uq, jnp.zeros((MLA_Q_RANK, MLA_HEADS, 32), f32)], axis=2)
    wuq = uq.reshape(MLA_Q_RANK, MLA_HEADS * LANES).astype(jnp.bfloat16)
    ukv = w_ukv.reshape(MLA_KV_RANK, MLA_HEADS, MLA_NOPE_DIM + MLA_V_DIM)
    uk = jnp.concatenate([ukv[:, :, :MLA_NOPE_DIM], jnp.zeros((MLA_KV_RANK, MLA_HEADS, 64), f32)], axis=2)
    uv = ukv[:, :, MLA_NOPE_DIM:]
    wukv = jnp.concatenate([uk.reshape(MLA_KV_RANK, MLA_HEADS * LANES),
                            uv.reshape(MLA_KV_RANK, MLA_HEADS * MLA_V_DIM)], axis=1).astype(jnp.bfloat16)

    wm = w_out[0:384].astype(jnp.bfloat16)
    wo_s = w_out[384:768].reshape(SWA_HEADS, HEAD_DIM, D_MODEL)
    ws = jnp.stack([wo_s[:half], wo_s[half:]], axis=1).reshape(SWA_HEADS * HEAD_DIM, D_MODEL).astype(jnp.bfloat16)
    wd = w_out[768:1024].astype(jnp.bfloat16)
    return w1, wuq, wukv, wm, ws, wd


def _router_weights(w_rg, b_rg, w_re, b_re):
    f32 = jnp.float32
    wr = jnp.concatenate([w_rg, jnp.zeros((D_MODEL, EXPERT_LANE0 - N_GROUPS), f32), w_re,
                          jnp.zeros((D_MODEL, LANES - EXPERT_LANE0 - N_EXPERTS), f32)], axis=1)
    br = jnp.concatenate([b_rg, jnp.zeros((EXPERT_LANE0 - N_GROUPS,), f32), b_re,
                          jnp.zeros((LANES - EXPERT_LANE0 - N_EXPERTS,), f32)])[None, :]
    hi = wr.astype(jnp.bfloat16)
    lo = (wr - hi.astype(f32)).astype(jnp.bfloat16)
    return jnp.concatenate([hi, lo], axis=1), br


def kernel(x, attn_norm, w_in, mla_q_norm, mla_kv_norm, mla_w_uq, mla_w_ukv, swa_sink, diff_lq1, diff_lk1, diff_lq2, diff_lk2, diff_subln, w_out, ffn_norm, router_group, router_group_bias, router_expert, router_expert_bias, w_gate, w_up, w_down, final_norm):
    batch, seq, d = x.shape
    assert (seq, d) == (SEQ, D_MODEL)
    depth = w_in.shape[0]
    n_tok = batch * seq
    tables = _rope_tables()
    wg = w_gate.reshape(depth * N_EXPERTS, D_MODEL, EXPERT_FF)
    wu = w_up.reshape(depth * N_EXPERTS, D_MODEL, EXPERT_FF)
    wdn = w_down.reshape(depth * N_EXPERTS, EXPERT_FF, D_MODEL)
    xt = x.reshape(n_tok, D_MODEL)
    pending = None
    for l in range(depth):
        w1, wuq, wukv, wm, ws, wd = _layer_weights(w_in[l], mla_w_uq[l], mla_w_ukv[l], w_out[l])
        outs = _proj_call(xt, attn_norm[l][None], w1, mla_q_norm[l][None], mla_kv_norm[l][None],
                          wuq, wukv, tables, combine=pending)
        qm, km, vm, qs, ks, vs, qd, kd, vd = outs[:9]
        if pending is not None:
            xt = outs[9]
        o_mla = _mla_call(qm, km, vm, batch)
        o_swa = _swa_call(swa_sink[l] * LOG2E, qs, ks, vs, batch)
        lam_init = 0.8 - 0.6 * math.exp(-0.3 * l)
        subln2 = jnp.concatenate([diff_subln[l], diff_subln[l]])[None]
        o_diff = _diff_call(lam_init, diff_lq1[l][None], diff_lk1[l][None], diff_lq2[l][None],
                            diff_lk2[l][None], subln2, qd, kd, vd, batch)
        wr2, br = _router_weights(router_group[l], router_group_bias[l],
                                  router_expert[l], router_expert_bias[l])
        x2, lt = _outproj_call(xt, o_mla, o_swa, o_diff, wm, ws, wd, ffn_norm[l][None], wr2, br)
        ri, rf, cnt = _route_call(lt)
        max_tiles = (2 * n_tok) // MOE_TILE + N_EXPERTS
        pos8, te8, meta = _plan_call(ri, cnt, max_tiles)
        pos = pos8[0:2].reshape(2 * n_tok)
        fill, n_tiles = meta[:, 0], meta[0, 1:2]
        xs = _push_call(pos, fill, n_tiles, x2, ffn_norm[l][None], max_tiles * MOE_TILE)
        ys = _expert_call(te8[0, :max_tiles], n_tiles, xs, wg, wu, wdn, l)
        xt, pending = x2, (pos, rf[0:2].T, ys)
    pos, rf2, ys = pending
    out = _combine_call(pos, xt, rf2, final_norm[None], ys)
    return out.reshape(batch, seq, d)
```
